```python
import math
import jax, jax.numpy as jnp
from jax import lax
import numpy as np

D_MODEL = 2048
BATCH = 2
SEQ = 4096
DEPTH = 2

GRID_W = 64
CTX_LEN = 256
EPS = 1e-6
H_A = 8
Q_LORA = 512
KV_LORA = 512
NOPE = 128
ROPE = 64
DV = 128
D_A = H_A * DV
Q_BLOCK = 128
ROPE_THETA = 10000.0
ATTN_SCALE = (NOPE + ROPE) ** -0.5
H_M = 8
DH_M = 128
D_M = H_M * DH_M
CONV_K = 3
CHUNK = 64
D_S = 1024
GROUP = 16
G_S = D_S // GROUP
P_S = 64
D_FF = ((8 * D_MODEL // 3 + 255) // 256) * 256
PARTS = (('cq', Q_LORA), ('ckv', KV_LORA), ('krope', ROPE), ('xm', D_M), ('om', D_M),
         ('gm', 4 * H_M), ('us', D_S), ('gates', 3 * D_MODEL))
ALL_PARTS = ('cq', 'ckv', 'krope', 'xm', 'om', 'gm', 'us', 'gates')
CTX_STATE_PARTS = ('ckv', 'krope', 'xm', 'gm', 'us')
N_IN = Q_LORA + KV_LORA + ROPE + 2 * D_M + 4 * H_M + D_S + 3 * D_MODEL

kernel_name = 'hybrid_mla_mlstm_s5_dit_trunk'


def _rmsnorm(x, g):
    xf = x.astype(jnp.float32)
    y = xf * lax.rsqrt(jnp.mean(xf * xf, axis=-1, keepdims=True) + EPS)
    return (y * g.astype(jnp.float32)).astype(x.dtype)


def _modulate(x, g, shift, scale):
    return _rmsnorm(x, g) * (1.0 + scale) + shift


def _project(h, w_in, names):
    offsets, o = {}, 0
    for name, width in PARTS:
        offsets[name] = (o, width)
        o += width
    if names == ALL_PARTS:
        w = w_in
    else:
        w = jnp.concatenate([w_in[:, offsets[n][0]:offsets[n][0] + offsets[n][1]] for n in names], axis=1)
    out = h @ w
    cuts = [int(s) for s in np.cumsum([offsets[n][1] for n in names])[:-1]]
    return dict(zip(names, jnp.split(out, cuts, axis=-1)))


def _axial_rope_tables(T):
    rows = T // GRID_W
    r, col = jnp.meshgrid(jnp.arange(rows, dtype=jnp.float32), jnp.arange(GRID_W, dtype=jnp.float32), indexing='ij')
    r, col = r.reshape(-1), col.reshape(-1)
    half = ROPE // 2
    inv = 1.0 / (ROPE_THETA ** (jnp.arange(0, half, 2, dtype=jnp.float32) / half))
    ang = jnp.concatenate([r[:, None] * inv, col[:, None] * inv], axis=-1)
    return jnp.cos(ang), jnp.sin(ang)


def _apply_rope(x, cos, sin):
    T = x.shape[-2]
    xs = x.astype(jnp.float32).reshape(x.shape[:-1] + (2, 2, ROPE // 4))
    c = cos.reshape(T, 2, ROPE // 4)
    s = sin.reshape(T, 2, ROPE // 4)
    x1, x2 = xs[..., 0, :], xs[..., 1, :]
    out = jnp.stack([x1 * c - x2 * s, x1 * s + x2 * c], axis=-2)
    return out.reshape(x.shape).astype(x.dtype)


def _mla_qkv(parts, g_cq, w_uq, g_ckv, w_ukv, cos, sin, need_q):
    B, T, _ = parts['ckv'].shape
    kv = (_rmsnorm(parts['ckv'], g_ckv) @ w_ukv).reshape(B, T, H_A, NOPE + DV).transpose(0, 2, 1, 3)
    k_nope, v = kv[..., :NOPE], kv[..., NOPE:]
    k_rope = parts['krope']
    if cos is not None:
        k_rope = _apply_rope(k_rope, cos, sin)
    k = jnp.concatenate([k_nope, jnp.broadcast_to(k_rope[:, None], (B, H_A, T, ROPE))], axis=-1)
    if not need_q:
        return None, k, v
    qf = (_rmsnorm(parts['cq'], g_cq) @ w_uq).reshape(B, T, H_A, NOPE + ROPE).transpose(0, 2, 1, 3)
    q_nope, q_rope = qf[..., :NOPE], qf[..., NOPE:]
    if cos is not None:
        q_rope = _apply_rope(q_rope, cos, sin)
    return jnp.concatenate([q_nope, q_rope], axis=-1), k, v


def _attend(q, k, v):
    s = jnp.einsum('bhqd,bhkd->bhqk', q, k).astype(jnp.float32) * ATTN_SCALE
    p = jax.nn.softmax(s, axis=-1)
    return jnp.einsum('bhqk,bhkd->bhqd', p.astype(v.dtype), v)


def _block_attention(q, k, v):
    B, H, T, dk = q.shape
    nb = T // Q_BLOCK
    qb = jnp.moveaxis(q.reshape(B, H, nb, Q_BLOCK, dk), 2, 0)
    out = lax.map(lambda qi: _attend(qi, k, v), qb)
    return out.transpose(1, 0, 3, 2, 4).reshape(B, T, H * DV)


def _depthwise_conv_centred(x, w):
    T = x.shape[1]
    pad = CONV_K // 2
    xp = jnp.pad(x, ((0, 0), (pad, pad), (0, 0)))
    return sum(xp[:, j:j + T] * w[j] for j in range(CONV_K))


def _mlstm_inputs(parts, conv_w, w_qkv, b_gate, need_q):
    xm = parts['xm']
    B, T, _ = xm.shape
    xc = jax.nn.silu(_depthwise_conv_centred(xm, conv_w)).reshape(B, T, H_M, DH_M)
    xh = xm.reshape(B, T, H_M, DH_M)
    k = jnp.einsum('bthd,hde->bhte', xc, w_qkv[1]).astype(jnp.float32) * DH_M ** -0.5
    v = jnp.einsum('bthd,hde->bhte', xh, w_qkv[2]).astype(jnp.float32)
    q = jnp.einsum('bthd,hde->bhte', xc, w_qkv[0]).astype(jnp.float32) if need_q else None
    gp = parts['gm'].astype(jnp.float32).reshape(B, T, 2, 2, H_M) + b_gate.astype(jnp.float32)
    gp = gp.transpose(2, 3, 0, 4, 1)
    return q, k, v, gp[:, 0], jax.nn.log_sigmoid(gp[:, 1])


def _mlstm_chunk_step(carry, inp):
    C, n, m = carry
    q, k, v, ig, lf = inp
    L = q.shape[2]
    b = jnp.cumsum(lf, axis=-1)
    d = b[..., :, None] - b[..., None, :] + ig[..., None, :]
    d = jnp.where(jnp.tril(jnp.ones((L, L), dtype=bool)), d, -jnp.inf)
    m_inter = b + m[..., None]
    m_t = jnp.maximum(jnp.max(d, axis=-1), m_inter)
    s = jnp.einsum('bhtd,bhsd->bhts', q, k) * jnp.exp(d - m_t[..., None])
    inter = jnp.exp(m_inter - m_t)
    num = jnp.einsum('bhts,bhsd->bhtd', s, v) + inter[..., None] * jnp.einsum('bhtd,bhde->bhte', q, C)
    den = jnp.sum(s, axis=-1) + inter * jnp.einsum('bhtd,bhd->bht', q, n)
    h = num / jnp.maximum(jnp.abs(den), jnp.exp(-m_t))[..., None]
    b_last = b[..., -1]
    g = b_last[..., None] - b + ig
    m_new = jnp.maximum(b_last + m, jnp.max(g, axis=-1))
    w = jnp.exp(g - m_new[..., None])
    decay = jnp.exp(b_last + m - m_new)
    C_new = decay[..., None, None] * C + jnp.einsum('bhs,bhsd,bhse->bhde', w, k, v)
    n_new = decay[..., None] * n + jnp.einsum('bhs,bhsd->bhd', w, k)
    return (C_new, n_new, m_new), h


def _chunks(a):
    B, H, T = a.shape[:3]
    return jnp.moveaxis(a.reshape((B, H, T // CHUNK, CHUNK) + a.shape[3:]), 2, 0)


def _mlstm_scan(q, k, v, ig, lf, state):
    B, H, T, dh = q.shape
    state, h = lax.scan(_mlstm_chunk_step, state, (_chunks(q), _chunks(k), _chunks(v), _chunks(ig), _chunks(lf)))
    return jnp.moveaxis(h, 0, 2).reshape(B, H, T, dh), state


def _mlstm_final_state(k, v, ig, lf):
    b = jnp.cumsum(lf, axis=-1)
    g = b[..., -1:] - b + ig
    m = jnp.max(g, axis=-1)
    w = jnp.exp(g - m[..., None])
    return (jnp.einsum('bhs,bhsd,bhse->bhde', w, k, v), jnp.einsum('bhs,bhsd->bhd', w, k), m)


def _mlstm_zero_state(B):
    return (jnp.zeros((B, H_M, DH_M, DH_M), jnp.float32), jnp.zeros((B, H_M, DH_M), jnp.float32),
            jnp.zeros((B, H_M), jnp.float32))


def _rev(a):
    return jnp.flip(a, axis=2)


def _mlstm_out(h_f, h_b, om, g_h):
    h = h_f + h_b
    hn = h * lax.rsqrt(jnp.mean(h * h, axis=-1, keepdims=True) + EPS) * g_h.astype(jnp.float32).reshape(H_M, 1, DH_M)
    B, _, T, _ = h.shape
    hn = hn.transpose(0, 2, 1, 3).reshape(B, T, D_M)
    return (jax.nn.sigmoid(om.astype(jnp.float32)) * hn).astype(om.dtype)


def _s5_discretise(a_re, a_im, log_dt):
    a_re = a_re.astype(jnp.float32)
    a_im = a_im.astype(jnp.float32)
    dt = jnp.exp(log_dt.astype(jnp.float32))[:, None]
    lam_r, lam_i = a_re * dt, a_im * dt
    mag = jnp.exp(lam_r)
    ar, ai = mag * jnp.cos(lam_i), mag * jnp.sin(lam_i)
    den = a_re * a_re + a_im * a_im
    nr, ni = ar - 1.0, ai
    cr = (nr * a_re + ni * a_im) / den
    ci = (ni * a_re - nr * a_im) / den
    return lam_r, lam_i, ar, ai, cr, ci


def _s5_drive(u, b_re, b_im):
    B, T, _ = u.shape
    ug = u.astype(jnp.float32).reshape(B, T, G_S, GROUP)
    return (jnp.einsum('btgc,gpc->btgp', ug, b_re.astype(jnp.float32)),
            jnp.einsum('btgc,gpc->btgp', ug, b_im.astype(jnp.float32)))


def _cplx_combine(e1, e2):
    a1r, a1i, b1r, b1i = e1
    a2r, a2i, b2r, b2i = e2
    return (a1r * a2r - a1i * a2i, a1r * a2i + a1i * a2r,
            a2r * b1r - a2i * b1i + b2r, a2r * b1i + a2i * b1r + b2i)


def _s5_scan(bur, bui, disc, s0, reverse):
    lam_r, lam_i, ar, ai, cr, ci = disc
    br = cr * bur - ci * bui
    bi = cr * bui + ci * bur
    if s0 is not None:
        s0r, s0i = s0
        idx = -1 if reverse else 0
        br = br.at[:, idx].add(ar * s0r - ai * s0i)
        bi = bi.at[:, idx].add(ar * s0i + ai * s0r)
    T = br.shape[1]
    a_r = jnp.broadcast_to(ar, (1, T) + ar.shape)
    a_i = jnp.broadcast_to(ai, (1, T) + ai.shape)
    _, _, sr, si = lax.associative_scan(_cplx_combine, (a_r, a_i, br, bi), reverse=reverse, axis=1)
    return sr, si


def _s5_final_state(bur, bui, disc, reverse):
    lam_r, lam_i, ar, ai, cr, ci = disc
    br = cr * bur - ci * bui
    bi = cr * bui + ci * bur
    T = br.shape[1]
    t = jnp.arange(T, dtype=jnp.float32)
    e = (t if reverse else (T - 1) - t)[:, None, None]
    mag = jnp.exp(lam_r * e)
    pr, pi = mag * jnp.cos(lam_i * e), mag * jnp.sin(lam_i * e)
    sr = jnp.einsum('tgp,btgp->bgp', pr, br) - jnp.einsum('tgp,btgp->bgp', pi, bi)
    si = jnp.einsum('tgp,btgp->bgp', pr, bi) + jnp.einsum('tgp,btgp->bgp', pi, br)
    return sr, si


def _s5_out(sr, si, c_re, c_im, d, u, w_glu):
    B, T = sr.shape[:2]
    y = jnp.einsum('btgp,gcp->btgc', sr, c_re.astype(jnp.float32)) - jnp.einsum('btgp,gcp->btgc', si, c_im.astype(jnp.float32))
    y = y.reshape(B, T, D_S) + d.astype(jnp.float32) * u.astype(jnp.float32)
    y = jax.nn.gelu(y).astype(u.dtype)
    a, gate = jnp.split(y @ w_glu, 2, axis=-1)
    return a * jax.nn.sigmoid(gate)


def _merge(gate_logits, a, m, s, w_br_a, w_br_m, w_br_s, w_out):
    ga, gm, gs = jnp.split(jax.nn.sigmoid(gate_logits), 3, axis=-1)
    return (ga * (a @ w_br_a) + gm * (m @ w_br_m) + gs * (s @ w_br_s)) @ w_out


def _swiglu(h, w_in, w_out):
    a, b = jnp.split(h @ w_in, 2, axis=-1)
    return (jax.nn.silu(a) * b) @ w_out


def _layer(x, ctx, mod, mod_ctx, p, cos, sin, last):
    (g_mix, g_ffn, w_in, g_cq, w_uq, g_ckv, w_ukv, conv_m, w_qkv_m, b_gate_m, g_h_m,
     s5_a_re, s5_a_im, s5_log_dt, s5_b_re, s5_b_im, s5_c_re, s5_c_im, s5_d, w_glu,
     w_br_a, w_br_m, w_br_s, w_out, w_ffn_in, w_ffn_out) = p
    B = x.shape[0]
    sh_m, sc_m, gt_m, sh_f, sc_f, gt_f = jnp.split(mod[:, None, :], 6, axis=-1)
    csh_m, csc_m, cgt_m, csh_f, csc_f, cgt_f = jnp.split(mod_ctx, 6, axis=-1)
    pl = _project(_modulate(x, g_mix, sh_m, sc_m), w_in, ALL_PARTS)
    pc = _project(_modulate(ctx, g_mix, csh_m, csc_m), w_in, CTX_STATE_PARTS if last else ALL_PARTS)

    q, k, v = _mla_qkv(pl, g_cq, w_uq, g_ckv, w_ukv, cos, sin, True)
    qc, kc, vc = _mla_qkv(pc, g_cq, w_uq, g_ckv, w_ukv, None, None, not last)
    a_lat = _block_attention(q, jnp.concatenate([kc, k], axis=2), jnp.concatenate([vc, v], axis=2))

    qmc, kmc, vmc, igc, lfc = _mlstm_inputs(pc, conv_m, w_qkv_m, b_gate_m, not last)
    if last:
        st_f = _mlstm_final_state(kmc, vmc, igc[0], lfc[0])
        st_b = _mlstm_final_state(_rev(kmc), _rev(vmc), _rev(igc[1]), _rev(lfc[1]))
    else:
        zero = _mlstm_zero_state(B)
        hcf, st_f = _mlstm_scan(qmc, kmc, vmc, igc[0], lfc[0], zero)
        hcb, st_b = _mlstm_scan(_rev(qmc), _rev(kmc), _rev(vmc), _rev(igc[1]), _rev(lfc[1]), zero)
        m_ctx = _mlstm_out(hcf, _rev(hcb), pc['om'], g_h_m)
    qm, km, vm, ig, lf = _mlstm_inputs(pl, conv_m, w_qkv_m, b_gate_m, True)
    hf, _ = _mlstm_scan(qm, km, vm, ig[0], lf[0], st_f)
    hb, _ = _mlstm_scan(_rev(qm), _rev(km), _rev(vm), _rev(ig[1]), _rev(lf[1]), st_b)
    m_lat = _mlstm_out(hf, _rev(hb), pl['om'], g_h_m)

    disc_f = _s5_discretise(s5_a_re[0], s5_a_im[0], s5_log_dt[0])
    disc_b = _s5_discretise(s5_a_re[1], s5_a_im[1], s5_log_dt[1])
    bur_c, bui_c = _s5_drive(pc['us'], s5_b_re, s5_b_im)
    if last:
        s0f = _s5_final_state(bur_c, bui_c, disc_f, False)
        s0b = _s5_final_state(bur_c, bui_c, disc_b, True)
    else:
        cfr, cfi = _s5_scan(bur_c, bui_c, disc_f, None, False)
        cbr, cbi = _s5_scan(bur_c, bui_c, disc_b, None, True)
        s0f = (cfr[:, -1], cfi[:, -1])
        s0b = (cbr[:, 0], cbi[:, 0])
        s_ctx = _s5_out(cfr + cbr, cfi + cbi, s5_c_re, s5_c_im, s5_d, pc['us'], w_glu)
    bur, bui = _s5_drive(pl['us'], s5_b_re, s5_b_im)
    sfr, sfi = _s5_scan(bur, bui, disc_f, s0f, False)
    sbr, sbi = _s5_scan(bur, bui, disc_b, s0b, True)
    s_lat = _s5_out(sfr + sbr, sfi + sbi, s5_c_re, s5_c_im, s5_d, pl['us'], w_glu)

    x = x + gt_m * _merge(pl['gates'], a_lat, m_lat, s_lat, w_br_a, w_br_m, w_br_s, w_out)
    x = x + gt_f * _swiglu(_modulate(x, g_ffn, sh_f, sc_f), w_ffn_in, w_ffn_out)
    if last:
        return x, None
    Bc, _, Tc, _ = qc.shape
    a_ctx = _attend(qc, kc, vc).transpose(0, 2, 1, 3).reshape(Bc, Tc, D_A)
    ctx = ctx + cgt_m * _merge(pc['gates'], a_ctx, m_ctx, s_ctx, w_br_a, w_br_m, w_br_s, w_out)
    ctx = ctx + cgt_f * _swiglu(_modulate(ctx, g_ffn, csh_f, csc_f), w_ffn_in, w_ffn_out)
    return x, ctx


def setup_inputs(seed: int = 0) -> dict:
    key = jax.random.key(seed)
    ks = jax.random.split(key, 34)
    L, D = DEPTH, D_MODEL

    def nrm(i, shape, scale=1.0):
        return scale * jax.random.normal(ks[i], shape, jnp.float32)

    def gain(i, n):
        return 1.0 + 0.02 * nrm(i, (L, n))

    b_gate_m = jnp.concatenate([nrm(15, (L, 2, 1, H_M), 0.1),
                                jnp.linspace(3.0, 6.0, H_M, dtype=jnp.float32) + nrm(16, (L, 2, 1, H_M), 0.1)], axis=2)
    return {
        'x': nrm(0, (BATCH, SEQ, D)),
        'c': nrm(1, (BATCH, D)),
        'ctx': nrm(2, (BATCH, CTX_LEN, D)),
        'c_ctx': nrm(3, (D,)),
        'w_ada': nrm(4, (L, D, 6 * D), D ** -0.5),
        'b_ada': nrm(5, (L, 6 * D), 0.02),
        'g_mix': gain(6, D),
        'g_ffn': gain(7, D),
        'w_in': nrm(8, (L, D, N_IN), D ** -0.5),
        'g_cq': gain(9, Q_LORA),
        'w_uq': nrm(10, (L, Q_LORA, H_A * (NOPE + ROPE)), Q_LORA ** -0.5),
        'g_ckv': gain(11, KV_LORA),
        'w_ukv': nrm(12, (L, KV_LORA, H_A * (NOPE + DV)), KV_LORA ** -0.5),
        'conv_m': nrm(13, (L, CONV_K, D_M), CONV_K ** -0.5),
        'w_qkv_m': nrm(14, (L, 3, H_M, DH_M, DH_M), DH_M ** -0.5),
        'b_gate_m': b_gate_m,
        'g_h_m': gain(17, D_M),
        's5_a_re': -0.5 + nrm(18, (L, 2, G_S, P_S), 0.01),
        's5_a_im': jnp.pi * jnp.arange(P_S, dtype=jnp.float32) + nrm(19, (L, 2, G_S, P_S), 0.01),
        's5_log_dt': jax.random.uniform(ks[20], (L, 2, G_S), jnp.float32, math.log(1e-3), math.log(1e-1)),
        's5_b_re': nrm(21, (L, G_S, P_S, GROUP), (2 * GROUP) ** -0.5),
        's5_b_im': nrm(22, (L, G_S, P_S, GROUP), (2 * GROUP) ** -0.5),
        's5_c_re': nrm(23, (L, G_S, GROUP, P_S), P_S ** -0.5),
        's5_c_im': nrm(24, (L, G_S, GROUP, P_S), P_S ** -0.5),
        's5_d': nrm(25, (L, D_S)),
        'w_glu': nrm(26, (L, D_S, 2 * D_S), D_S ** -0.5),
        'w_br_a': nrm(27, (L, D_A, D), D_A ** -0.5),
        'w_br_m': nrm(28, (L, D_M, D), D_M ** -0.5),
        'w_br_s': nrm(29, (L, D_S, D), D_S ** -0.5),
        'w_out': nrm(30, (L, D, D), D ** -0.5),
        'w_ffn_in': nrm(31, (L, D, 2 * D_FF), D ** -0.5),
        'w_ffn_out': nrm(32, (L, D_FF, D), D_FF ** -0.5),
        'g_final': 1.0 + 0.02 * nrm(33, (D,)),
    }


def reference(x, c, ctx, c_ctx, w_ada, b_ada, g_mix, g_ffn, w_in, g_cq, w_uq, g_ckv, w_ukv,
              conv_m, w_qkv_m, b_gate_m, g_h_m, s5_a_re, s5_a_im, s5_log_dt, s5_b_re, s5_b_im,
              s5_c_re, s5_c_im, s5_d, w_glu, w_br_a, w_br_m, w_br_s, w_out, w_ffn_in, w_ffn_out, g_final):
    cos, sin = _axial_rope_tables(x.shape[1])
    for l in range(DEPTH):
        last = l == DEPTH - 1
        mod = jax.nn.silu(c) @ w_ada[l] + b_ada[l]
        mod_ctx = jax.nn.silu(c_ctx) @ w_ada[l] + b_ada[l]
        p = (g_mix[l], g_ffn[l], w_in[l], g_cq[l], w_uq[l], g_ckv[l], w_ukv[l], conv_m[l], w_qkv_m[l],
             b_gate_m[l], g_h_m[l], s5_a_re[l], s5_a_im[l], s5_log_dt[l], s5_b_re[l], s5_b_im[l],
             s5_c_re[l], s5_c_im[l], s5_d[l], w_glu[l], w_br_a[l], w_br_m[l], w_br_s[l], w_out[l],
             w_ffn_in[l], w_ffn_out[l])
        x, ctx = _layer(x, ctx, mod, mod_ctx, p, cos, sin, last)
    return _rmsnorm(x, g_final)
```

```python
import functools
import math

import numpy as np
import jax
import jax.numpy as jnp
from jax import lax
from jax.experimental import pallas as pl
from jax.experimental.pallas import tpu as pltpu

F32 = jnp.float32
BF16 = jnp.bfloat16

D = 2048
NB_ = 2
T = 4096
TC = 256
GRID_W = 64
EPS = 1e-6
H_A, Q_LORA, KV_LORA, NOPE, ROPE, DV = 8, 512, 512, 128, 64, 128
ROPE_THETA = 10000.0
ATTN_SCALE = (NOPE + ROPE) ** -0.5
H_M, DH_M = 8, 128
D_M = H_M * DH_M
D_S, GROUP, P_S = 1024, 16, 64
G_S = D_S // GROUP
D_FF = ((8 * D // 3 + 255) // 256) * 256
OFF = {}
_o = 0
for _n, _w in (('cq', Q_LORA), ('ckv', KV_LORA), ('krope', ROPE), ('xm', D_M), ('om', D_M),
               ('gm', 4 * H_M), ('us', D_S), ('gates', 3 * D)):
    OFF[_n] = (_o, _w)
    _o += _w

NL = NB_ * T
NCX = NB_ * TC
R = NL + NCX
TM = 512
NBLK = R // TM
LC = 256
TQ = 256
TK = 512
SL = 16
NCH = R // SL
VMEM_LIMIT = 52 * 1024 * 1024
SEQ_STARTS = tuple(b * T for b in range(NB_)) + tuple(NL + b * TC for b in range(NB_))
SEQ_ENDS = tuple(b * T + T - 1 for b in range(NB_)) + tuple(NL + b * TC + TC - 1 for b in range(NB_))


def _cp(sem):
    return pltpu.CompilerParams(dimension_semantics=sem, vmem_limit_bytes=VMEM_LIMIT)


def _dot(a, b):
    return jnp.dot(a, b, preferred_element_type=F32)


def _dot_nt(a, b):
    return lax.dot_general(a, b, (((1,), (1,)), ((), ())), preferred_element_type=F32)


def _dot_tn(a, b):
    return lax.dot_general(a, b, (((0,), (0,)), ((), ())), preferred_element_type=F32)


def _sigmoid(x):
    return 1.0 / (1.0 + jnp.exp(-x))


def _log_sigmoid(x):
    return jnp.minimum(x, 0.0) - jnp.log(1.0 + jnp.exp(-jnp.abs(x)))


def _rms(x, g):
    xf = x.astype(F32)
    return xf * lax.rsqrt(jnp.mean(xf * xf, axis=-1, keepdims=True) + EPS) * g


def _ada_kernel(c_ref, w_ref, b_ref, o_ref):
    c = c_ref[...]
    s = (c * _sigmoid(c)).astype(BF16)
    o_ref[0] = _dot(s, w_ref[0].astype(BF16)) + b_ref[0]


def _ada(cvec, w_ada, b_ada):
    depth = w_ada.shape[0]
    tn = 1024
    return pl.pallas_call(
        _ada_kernel,
        grid=(depth, 6 * D // tn),
        in_specs=[pl.BlockSpec((8, D), lambda l, j: (0, 0)),
                  pl.BlockSpec((1, D, tn), lambda l, j: (l, 0, j)),
                  pl.BlockSpec((1, 1, tn), lambda l, j: (l, 0, j))],
        out_specs=pl.BlockSpec((1, 8, tn), lambda l, j: (l, 0, j)),
        out_shape=jax.ShapeDtypeStruct((depth, 8, 6 * D), F32),
        compiler_params=_cp(("arbitrary", "arbitrary")),
        name="ada",
    )(cvec, w_ada, b_ada.reshape(depth, 1, 6 * D))


def _modulate(x_ref, g_ref, sh_ref, sc_ref):
    return _rms(x_ref[...], g_ref[...]) * (1.0 + sc_ref[0]) + sh_ref[0]


def _inproj_kernel(x_ref, g_ref, sh_ref, sc_ref, w_ref, o_ref):
    h = _modulate(x_ref, g_ref, sh_ref, sc_ref).astype(BF16)
    o_ref[...] = _dot(h, w_ref[...]).astype(o_ref.dtype)


def _inproj(x, g, modrb, w, tn, out_dtype, name):
    n = w.shape[1]
    return pl.pallas_call(
        _inproj_kernel,
        grid=(n // tn, NBLK),
        in_specs=[pl.BlockSpec((TM, D), lambda j, i: (i, 0)),
                  pl.BlockSpec((1, D), lambda j, i: (0, 0)),
                  pl.BlockSpec((1, 1, D), lambda j, i: (i, 0, 0)),
                  pl.BlockSpec((1, 1, D), lambda j, i: (i, 0, 1)),
                  pl.BlockSpec((D, tn), lambda j, i: (0, j))],
        out_specs=pl.BlockSpec((TM, tn), lambda j, i: (i, j)),
        out_shape=jax.ShapeDtypeStruct((R, n), out_dtype),
        compiler_params=_cp(("arbitrary", "arbitrary")),
        name=name,
    )(x, g.reshape(1, D), modrb, modrb, w)


def _mla_proj_kernel(cq_ref, ckv_ref, kr_ref, kp_ref, cos_ref, sin_ref, gq_ref, gkv_ref,
                     wqn_ref, wqr_ref, wqp_ref, wkn_ref, wv_ref, q_ref, k_ref, v_ref):
    hq = _rms(cq_ref[...], gq_ref[...]).astype(BF16)
    hk = _rms(ckv_ref[...], gkv_ref[...]).astype(BF16)
    cos = cos_ref[...]
    sin = sin_ref[...]
    qn = _dot(hq, wqn_ref[...]) * ATTN_SCALE
    qr = _dot(hq, wqr_ref[...])
    qp = _dot(hq, wqp_ref[...])
    kn = _dot(hk, wkn_ref[...])
    v_ref[...] = _dot(hk, wv_ref[...]).astype(BF16)
    kr = (kr_ref[...] * cos + kp_ref[...] * sin).astype(BF16)
    for h in range(H_A):
        lo = slice(h * 256, h * 256 + 128)
        hi = slice(h * 256 + 128, (h + 1) * 256)
        sl = slice(h * 128, (h + 1) * 128)
        q_ref[:, lo] = qn[:, sl].astype(BF16)
        q_ref[:, hi] = ((qr[:, sl] * cos + qp[:, sl] * sin) * ATTN_SCALE).astype(BF16)
        k_ref[:, lo] = kn[:, sl].astype(BF16)
        k_ref[:, hi] = kr


def _mla_proj(p, psm, cosk, sink, gq, gkv, wqn, wqr, wqp, wkn, wv):
    full = lambda shp: pl.BlockSpec(shp, lambda i: (0, 0))
    return pl.pallas_call(
        _mla_proj_kernel,
        grid=(NBLK,),
        in_specs=[pl.BlockSpec((TM, 512), lambda i: (i, 0)),
                  pl.BlockSpec((TM, 512), lambda i: (i, 1)),
                  pl.BlockSpec((TM, 128), lambda i: (i, 0)),
                  pl.BlockSpec((TM, 128), lambda i: (i, 1)),
                  pl.BlockSpec((TM, 128), lambda i: (i, 0)),
                  pl.BlockSpec((TM, 128), lambda i: (i, 0)),
                  full((1, 512)), full((1, 512)),
                  full((512, 1024)), full((512, 1024)), full((512, 1024)),
                  full((512, 1024)), full((512, 1024))],
        out_specs=[pl.BlockSpec((TM, 2048), lambda i: (i, 0)),
                   pl.BlockSpec((TM, 2048), lambda i: (i, 0)),
                   pl.BlockSpec((TM, 1024), lambda i: (i, 0))],
        out_shape=[jax.ShapeDtypeStruct((R, 2048), BF16),
                   jax.ShapeDtypeStruct((R, 2048), BF16),
                   jax.ShapeDtypeStruct((R, 1024), BF16)],
        compiler_params=_cp(("arbitrary",)),
        name="mla_proj",
    )(p, p, psm, psm, cosk, sink, gq.reshape(1, 512), gkv.reshape(1, 512), wqn, wqr, wqp, wkn, wv)


def _attn_kernel(q_ref, kl_ref, kc_ref, vl_ref, vc_ref, o_ref):
    qi = pl.program_id(2)
    q = q_ref[...]

    def chunk(k, v, m, l, acc):
        s = _dot_nt(q, k)
        m_new = jnp.maximum(m, jnp.max(s, axis=1, keepdims=True))
        a = jnp.exp(m - m_new)
        p = jnp.exp(s - m_new)
        l = a * l + jnp.sum(p, axis=1, keepdims=True)
        acc = a * acc + _dot(p.astype(BF16), v)
        return m_new, l, acc

    m0 = jnp.full((TQ, 1), -jnp.inf, F32)
    l0 = jnp.zeros((TQ, 1), F32)
    a0 = jnp.zeros((TQ, DV), F32)
    carry = chunk(kc_ref[...], vc_ref[...], m0, l0, a0)

    def body(j, c):
        off = pl.multiple_of(j * TK, TK)
        return chunk(kl_ref[pl.ds(off, TK), :], vl_ref[pl.ds(off, TK), :], *c)

    n = jnp.where(qi < T // TQ, T // TK, 0)
    m, l, acc = lax.fori_loop(0, n, body, carry)
    o_ref[...] = (acc / l).astype(o_ref.dtype)


def _attention(q, k, v):
    nq = T // TQ
    qrow = lambda b, h, i: jnp.where(i < nq, b * nq + i, NL // TQ + b)
    return pl.pallas_call(
        _attn_kernel,
        grid=(NB_, H_A, nq + 1),
        in_specs=[pl.BlockSpec((TQ, 256), lambda b, h, i: (qrow(b, h, i), h)),
                  pl.BlockSpec((T, 256), lambda b, h, i: (b, h)),
                  pl.BlockSpec((TC, 256), lambda b, h, i: (NL // TC + b, h)),
                  pl.BlockSpec((T, DV), lambda b, h, i: (b, h)),
                  pl.BlockSpec((TC, DV), lambda b, h, i: (NL // TC + b, h))],
        out_specs=pl.BlockSpec((TQ, DV), lambda b, h, i: (qrow(b, h, i), h)),
        out_shape=jax.ShapeDtypeStruct((R, H_A * DV), BF16),
        compiler_params=_cp(("arbitrary", "arbitrary", "arbitrary")),
        name="attention",
    )(q, k, k, v, v)


def _row_in(grow, rows):
    hit = grow == rows[0]
    for r in rows[1:]:
        hit = jnp.logical_or(hit, grow == r)
    return hit


def _mprep_kernel(x_ref, xp_ref, xn_ref, cw_ref, wq_ref, wk_ref, wv_ref, q_ref, k_ref, v_ref):
    i = pl.program_id(0)
    xb = x_ref[...]
    x = xb.astype(F32)
    prev_row = xp_ref[15:16, :].astype(F32)
    next_row = xn_ref[0:1, :].astype(F32)
    row = lax.broadcasted_iota(jnp.int32, (TM, 1), 0)
    grow = row + i * TM
    xprev = jnp.where(row == 0, prev_row, pltpu.roll(x, 1, axis=0))
    xprev = jnp.where(_row_in(grow, SEQ_STARTS), 0.0, xprev)
    xnext = jnp.where(row == TM - 1, next_row, pltpu.roll(x, TM - 1, axis=0))
    xnext = jnp.where(_row_in(grow, SEQ_ENDS), 0.0, xnext)
    xc = xprev * cw_ref[0:1, :] + x * cw_ref[1:2, :] + xnext * cw_ref[2:3, :]
    xcb = (xc * _sigmoid(xc)).astype(BF16)
    for h in range(H_M):
        sl = slice(h * DH_M, (h + 1) * DH_M)
        q_ref[:, sl] = _dot(xcb[:, sl], wq_ref[h]).astype(BF16)
        k_ref[:, sl] = (_dot(xcb[:, sl], wk_ref[h]) * DH_M ** -0.5).astype(BF16)
        v_ref[:, sl] = _dot(xb[:, sl], wv_ref[h]).astype(BF16)


def _mprep(p, conv_w, wq, wk, wv):
    hb = TM // 16
    wspec = pl.BlockSpec((H_M, DH_M, DH_M), lambda i: (0, 0, 0))
    return pl.pallas_call(
        _mprep_kernel,
        grid=(NBLK,),
        in_specs=[pl.BlockSpec((TM, D_M), lambda i: (i, 1)),
                  pl.BlockSpec((16, D_M), lambda i: (jnp.maximum(i * hb - 1, 0), 1)),
                  pl.BlockSpec((16, D_M), lambda i: (jnp.minimum((i + 1) * hb, R // 16 - 1), 1)),
                  pl.BlockSpec((3, D_M), lambda i: (0, 0)),
                  wspec, wspec, wspec],
        out_specs=[pl.BlockSpec((TM, D_M), lambda i: (i, 0))] * 3,
        out_shape=[jax.ShapeDtypeStruct((R, D_M), BF16)] * 3,
        compiler_params=_cp(("arbitrary",)),
        name="mlstm_prep",
    )(p, p, p, conv_w, wq, wk, wv)


def _split3(x):
    h = x.astype(BF16)
    r = x - h.astype(F32)
    m = r.astype(BF16)
    l = (r - m.astype(F32)).astype(BF16)
    return h, m, l


def _mlstm_kernel(q_ref, k_ref, v_ref, g_ref, gt_ref, b_ref, bt_ref, h_ref, c_ref, n_ref, m_ref):
    d = pl.program_id(1)
    s = pl.program_id(2)

    @pl.when(s == 0)
    def _():
        c_ref[...] = jnp.zeros_like(c_ref)
        n_ref[...] = jnp.zeros_like(n_ref)
        m_ref[...] = jnp.zeros_like(m_ref)

    sign = 1 - 2 * d
    r = lax.broadcasted_iota(jnp.int32, (LC, LC), 0)
    c = lax.broadcasted_iota(jnp.int32, (LC, LC), 1)
    mask = (r - c) * sign >= 0
    tri = jnp.where(mask, 1.0, 0.0).astype(BF16)
    g = g_ref[0] + b_ref[0]
    gt = gt_ref[0] + bt_ref[0]
    ig_c = g[:, 0:H_M]
    lf_c = _log_sigmoid(g[:, H_M:2 * H_M])
    ig_r = gt[0:H_M, :]
    lf_r = _log_sigmoid(gt[H_M:2 * H_M, :])
    b_c = sum(_dot(tri, part) for part in _split3(lf_c))
    b_r = sum(_dot_nt(part, tri) for part in _split3(lf_r))
    tot = jnp.sum(lf_c, axis=0, keepdims=True)

    for h in range(H_M):
        sl = slice(h * DH_M, (h + 1) * DH_M)
        qh = q_ref[:, sl]
        kh = k_ref[:, sl]
        vh = v_ref[:, sl]
        bc = b_c[:, h:h + 1]
        br = b_r[h:h + 1, :]
        igr = ig_r[h:h + 1, :]
        igc = ig_c[:, h:h + 1]
        tt = tot[:, h:h + 1]
        mp = m_ref[h:h + 1, 0:1]
        dm = jnp.where(mask, bc - br + igr, -jnp.inf)
        m_inter = bc + mp
        m_t = jnp.maximum(jnp.max(dm, axis=1, keepdims=True), m_inter)
        sc = _dot_nt(qh, kh) * jnp.exp(dm - m_t)
        inter = jnp.exp(m_inter - m_t)
        cst = c_ref[h]
        nrow = n_ref[h:h + 1, :]
        num = _dot(sc.astype(BF16), vh) + inter * _dot(qh, cst.astype(BF16))
        den = (jnp.sum(sc, axis=1, keepdims=True)
               + inter * jnp.sum(qh.astype(F32) * nrow, axis=1, keepdims=True))
        h_ref[0, :, sl] = num / jnp.maximum(jnp.abs(den), jnp.exp(-m_t))
        gg = tt - bc + igc
        m_new = jnp.maximum(tt + mp, jnp.max(gg, axis=0, keepdims=True))
        w = jnp.exp(gg - m_new)
        decay = jnp.exp(tt + mp - m_new)
        kw = kh.astype(F32) * w
        c_ref[h] = decay * cst + _dot_tn(kw.astype(BF16), vh)
        n_ref[h:h + 1, :] = decay * nrow + jnp.sum(kw, axis=0, keepdims=True)
        m_ref[h:h + 1, :] = jnp.broadcast_to(m_new, (1, DH_M))


def _mlstm(q, k, v, gd, gtd, bd, btd):
    nl = T // LC
    nsteps = nl + TC // LC

    def rb(b, d, s):
        lat = b * nl + jnp.where(d == 0, s - 1, nl - s)
        return jnp.where(s == 0, NL // LC + b, lat)

    qspec = pl.BlockSpec((LC, D_M), lambda b, d, s: (rb(b, d, s), 0))
    return pl.pallas_call(
        _mlstm_kernel,
        grid=(NB_, 2, nsteps),
        in_specs=[qspec, qspec, qspec,
                  pl.BlockSpec((1, LC, 16), lambda b, d, s: (d, rb(b, d, s), 0)),
                  pl.BlockSpec((1, 16, LC), lambda b, d, s: (d, 0, rb(b, d, s))),
                  pl.BlockSpec((1, 1, 16), lambda b, d, s: (d, 0, 0)),
                  pl.BlockSpec((1, 16, 1), lambda b, d, s: (d, 0, 0))],
        out_specs=pl.BlockSpec((1, LC, D_M), lambda b, d, s: (d, rb(b, d, s), 0)),
        out_shape=jax.ShapeDtypeStruct((2, R, D_M), F32),
        scratch_shapes=[pltpu.VMEM((H_M, DH_M, DH_M), F32),
                        pltpu.VMEM((H_M, DH_M), F32),
                        pltpu.VMEM((H_M, DH_M), F32)],
        compiler_params=_cp(("arbitrary", "arbitrary", "arbitrary")),
        name="mlstm_scan",
    )(q, k, v, gd, gtd, bd, btd)


def _s5_state_in_kernel(x_ref, w_ref, fre_ref, fim_ref, gre_ref, gim_ref):
    e = _dot(x_ref[0], w_ref[0, 0]) + _dot(x_ref[1], w_ref[0, 1])
    fre_ref[...] = e[:, 0:128]
    fim_ref[...] = e[:, 128:256]
    gre_ref[...] = e[:, 256:384]
    gim_ref[...] = e[:, 384:512]


def _s5_state_in(xg, wst):
    ospec = pl.BlockSpec((NCH, 128), lambda g: (0, g))
    return pl.pallas_call(
        _s5_state_in_kernel,
        grid=(G_S // 2,),
        in_specs=[pl.BlockSpec((2, NCH, 256), lambda g: (g, 0, 0)),
                  pl.BlockSpec((1, 2, 256, 512), lambda g: (g, 0, 0, 0))],
        out_specs=[ospec] * 4,
        out_shape=[jax.ShapeDtypeStruct((NCH, G_S * P_S), F32)] * 4,
        compiler_params=_cp(("arbitrary",)),
        name="s5_state_in",
    )(xg, wst)


def _s5_scan_kernel(fre_ref, fim_ref, gre_ref, gim_ref, a_ref, sfr_ref, sfi_ref, sbr_ref, sbi_ref):
    afr, afi, abr, abi = a_ref[0:1, :], a_ref[1:2, :], a_ref[2:3, :], a_ref[3:4, :]
    w = fre_ref.shape[1]
    nlat = T // SL
    nctx = TC // SL

    def fwd(row, st):
        sr, si = st
        sfr_ref[pl.ds(row, 1), :] = sr
        sfi_ref[pl.ds(row, 1), :] = si
        fr = fre_ref[pl.ds(row, 1), :]
        fi = fim_ref[pl.ds(row, 1), :]
        return afr * sr - afi * si + fr, afr * si + afi * sr + fi

    def bwd(row, st):
        sr, si = st
        sbr_ref[pl.ds(row, 1), :] = sr
        sbi_ref[pl.ds(row, 1), :] = si
        gr = gre_ref[pl.ds(row, 1), :]
        gi = gim_ref[pl.ds(row, 1), :]
        return abr * sr - abi * si + gr, abr * si + abi * sr + gi

    z = jnp.zeros((1, w), F32)
    init = tuple((z, z) for _ in range(2 * NB_))

    def phase(nsteps, base, sts):
        def body(kk, sts):
            out = []
            for b in range(NB_):
                out.append(fwd(base(b) + kk, sts[2 * b]))
                out.append(bwd(base(b) + nsteps - 1 - kk, sts[2 * b + 1]))
            return tuple(out)
        return lax.fori_loop(0, nsteps, body, sts)

    sts = phase(nctx, lambda b: NL // SL + b * nctx, init)
    phase(nlat, lambda b: b * nlat, sts)


def _s5_scan(fre, fim, gre, gim, apow):
    wcol = 512
    spec = pl.BlockSpec((NCH, wcol), lambda j: (0, j))
    return pl.pallas_call(
        _s5_scan_kernel,
        grid=(G_S * P_S // wcol,),
        in_specs=[spec] * 4 + [pl.BlockSpec((8, wcol), lambda j: (0, j))],
        out_specs=[spec] * 4,
        out_shape=[jax.ShapeDtypeStruct((NCH, G_S * P_S), F32)] * 4,
        compiler_params=_cp(("arbitrary",)),
        name="s5_scan",
    )(fre, fim, gre, gim, apow)


def _s5_out_kernel(x_ref, t_ref, sfr_ref, sfi_ref, sbr_ref, sbi_ref, wo_ref, d_ref, y_ref):
    st = jnp.concatenate([sfr_ref[...], sfi_ref[...], sbr_ref[...], sbi_ref[...]], axis=1).astype(BF16)
    carry = _dot(st, wo_ref[0])
    for k in range(2):
        x = x_ref[k]
        y = _dot(x, t_ref[k]) + carry[:, k * 256:(k + 1) * 256] + d_ref[k] * x.astype(F32)
        y_ref[k] = jax.nn.gelu(y).astype(BF16)


def _s5_out(xg, toep, sfr, sfi, sbr, sbi, wout, dflat):
    sspec = pl.BlockSpec((NCH, 128), lambda g: (0, g))
    return pl.pallas_call(
        _s5_out_kernel,
        grid=(G_S // 2,),
        in_specs=[pl.BlockSpec((2, NCH, 256), lambda g: (g, 0, 0)),
                  pl.BlockSpec((2, 256, 256), lambda g: (g, 0, 0)),
                  sspec, sspec, sspec, sspec,
                  pl.BlockSpec((1, 512, 512), lambda g: (g, 0, 0)),
                  pl.BlockSpec((2, 1, 256), lambda g: (g, 0, 0))],
        out_specs=pl.BlockSpec((2, NCH, 256), lambda g: (g, 0, 0)),
        out_shape=jax.ShapeDtypeStruct((G_S, NCH, 256), BF16),
        compiler_params=_cp(("arbitrary",)),
        name="s5_out",
    )(xg, toep, sfr, sfi, sbr, sbi, wout, dflat)


def _glu_kernel(y_ref, w_ref, o_ref):
    z = _dot(y_ref[...], w_ref[...])
    o_ref[...] = (z[:, :D_S] * _sigmoid(z[:, D_S:])).astype(BF16)


def _glu(y, w):
    return pl.pallas_call(
        _glu_kernel,
        grid=(NBLK,),
        in_specs=[pl.BlockSpec((TM, D_S), lambda i: (i, 0)),
                  pl.BlockSpec((D_S, 2 * D_S), lambda i: (0, 0))],
        out_specs=pl.BlockSpec((TM, D_S), lambda i: (i, 0)),
        out_shape=jax.ShapeDtypeStruct((R, D_S), BF16),
        compiler_params=_cp(("arbitrary",)),
        name="s5_glu",
    )(y, w)


def _s5_params(a_re, a_im, log_dt, b_re, b_im, c_re, c_im, dskip):
    dt = jnp.exp(log_dt)[:, :, None]
    lam_r, lam_i = a_re * dt, a_im * dt
    mag = jnp.exp(lam_r)
    ar, ai = mag * jnp.cos(lam_i), mag * jnp.sin(lam_i)
    den = a_re * a_re + a_im * a_im
    nr, ni = ar - 1.0, ai
    cr = (nr * a_re + ni * a_im) / den
    ci = (ni * a_re - nr * a_im) / den
    cpr = c_re[None] * cr[:, :, None, :] - c_im[None] * ci[:, :, None, :]
    cpi = c_re[None] * ci[:, :, None, :] + c_im[None] * cr[:, :, None, :]
    j = jnp.arange(SL + 1, dtype=F32)[:, None, None, None]
    pm = jnp.exp(lam_r[None] * j)
    pr, pi = pm * jnp.cos(lam_i[None] * j), pm * jnp.sin(lam_i[None] * j)
    abr = pr[..., None] * b_re[None, None] - pi[..., None] * b_im[None, None]
    abi = pr[..., None] * b_im[None, None] + pi[..., None] * b_re[None, None]
    kern = (jnp.einsum('dgcp,jdgpe->jdgce', cpr, abr[:SL]) - jnp.einsum('dgcp,jdgpe->jdgce', cpi, abi[:SL]))
    s_idx = jnp.arange(SL)[:, None]
    t_idx = jnp.arange(SL)[None, :]
    lag = t_idx - s_idx
    kf = jnp.where((lag >= 0)[:, :, None, None, None], kern[jnp.clip(lag, 0, SL - 1), 0], 0.0)
    kb = jnp.where((lag <= 0)[:, :, None, None, None], kern[jnp.clip(-lag, 0, SL - 1), 1], 0.0)
    toep = (kf + kb).transpose(2, 0, 4, 1, 3).reshape(G_S, SL * GROUP, SL * GROUP)
    rev = jnp.arange(SL - 1, -1, -1)
    fcol = lambda a: a.transpose(1, 0, 3, 2).reshape(G_S, SL * GROUP, P_S)
    wf_re, wf_im = fcol(abr[rev, 0]), fcol(abi[rev, 0])
    wg_re, wg_im = fcol(abr[:SL, 1]), fcol(abi[:SL, 1])
    z = jnp.zeros_like(wf_re)
    ev = jnp.concatenate([wf_re, z, wf_im, z, wg_re, z, wg_im, z], axis=-1)
    od = jnp.concatenate([z, wf_re, z, wf_im, z, wg_re, z, wg_im], axis=-1)
    wst = jnp.stack([ev[0::2], od[1::2]], axis=1)
    def readout(pw_r, pw_i, d):
        re = cpr[d][None] * pw_r[:, :, None, :] - cpi[d][None] * pw_i[:, :, None, :]
        im = cpr[d][None] * pw_i[:, :, None, :] + cpi[d][None] * pw_r[:, :, None, :]
        shp = lambda a: a.transpose(1, 3, 0, 2).reshape(G_S, P_S, SL * GROUP)
        return shp(re), -shp(im)
    of_re, of_im = readout(pr[1:SL + 1, 0], pi[1:SL + 1, 0], 0)
    ob_re, ob_im = readout(pr[SL - jnp.arange(SL), 1], pi[SL - jnp.arange(SL), 1], 1)
    zo = jnp.zeros_like(of_re)
    rows = []
    for blk in (of_re, of_im, ob_re, ob_im):
        rows.append(jnp.concatenate([blk[0::2], zo[0::2]], axis=-1))
        rows.append(jnp.concatenate([zo[1::2], blk[1::2]], axis=-1))
    wout = jnp.concatenate(rows, axis=1)
    flat = lambda a: a.reshape(1, G_S * P_S)
    apow = jnp.concatenate([flat(pr[SL, 0]), flat(pi[SL, 0]), flat(pr[SL, 1]), flat(pi[SL, 1]),
                            jnp.zeros((4, G_S * P_S), F32)], axis=0)
    dflat = jnp.tile(dskip.reshape(G_S, 1, GROUP), (1, SL, 1)).reshape(G_S, 1, SL * GROUP)
    return toep.astype(BF16), wst.astype(BF16), wout.astype(BF16), apow, dflat


def _merge_kernel(a_ref, hf_ref, hb_ref, om_ref, gh_ref, s_ref, ga_ref, gm_ref, gs_ref,
                  wa_ref, wm_ref, ws_ref, o_ref):
    hsum = hf_ref[0] + hb_ref[0]
    parts = []
    for h in range(H_M):
        xh = hsum[:, h * DH_M:(h + 1) * DH_M]
        parts.append(xh * lax.rsqrt(jnp.mean(xh * xh, axis=-1, keepdims=True) + EPS))
    hn = jnp.concatenate(parts, axis=1) * gh_ref[...]
    m = (_sigmoid(om_ref[...].astype(F32)) * hn).astype(BF16)
    t = (_sigmoid(ga_ref[...].astype(F32)) * _dot(a_ref[...], wa_ref[...])
         + _sigmoid(gm_ref[...].astype(F32)) * _dot(m, wm_ref[...])
         + _sigmoid(gs_ref[...].astype(F32)) * _dot(s_ref[...], ws_ref[...]))
    o_ref[...] = t.astype(BF16)


def _merge(a, hdir, p, gh, s, wa, wm, ws):
    tn = 1024
    nj = D // tn
    gcol = 4096 // tn
    row = lambda shp, c: pl.BlockSpec(shp, lambda j, i: (i, c))
    wspec = pl.BlockSpec((1024, tn), lambda j, i: (0, j))
    return pl.pallas_call(
        _merge_kernel,
        grid=(nj, NBLK),
        in_specs=[row((TM, 1024), 0),
                  pl.BlockSpec((1, TM, D_M), lambda j, i: (0, i, 0)),
                  pl.BlockSpec((1, TM, D_M), lambda j, i: (1, i, 0)),
                  row((TM, 1024), 2),
                  pl.BlockSpec((1, D_M), lambda j, i: (0, 0)),
                  row((TM, 1024), 0),
                  pl.BlockSpec((TM, tn), lambda j, i: (i, gcol + j)),
                  pl.BlockSpec((TM, tn), lambda j, i: (i, gcol + nj + j)),
                  pl.BlockSpec((TM, tn), lambda j, i: (i, gcol + 2 * nj + j)),
                  wspec, wspec, wspec],
        out_specs=pl.BlockSpec((TM, tn), lambda j, i: (i, j)),
        out_shape=jax.ShapeDtypeStruct((R, D), BF16),
        compiler_params=_cp(("arbitrary", "arbitrary")),
        name="merge",
    )(a, hdir, hdir, p, gh.reshape(1, D_M), s, p, p, p, wa, wm, ws)


def _resid_kernel(t_ref, w_ref, x_ref, gt_ref, o_ref):
    o_ref[...] = x_ref[...] + gt_ref[0] * _dot(t_ref[...], w_ref[...])


def _resid(t, w, x, modrb, gate_chunk, tn, name):
    kdim = t.shape[1]
    nj = D // tn
    return pl.pallas_call(
        _resid_kernel,
        grid=(nj, NBLK),
        in_specs=[pl.BlockSpec((TM, kdim), lambda j, i: (i, 0)),
                  pl.BlockSpec((kdim, tn), lambda j, i: (0, j)),
                  pl.BlockSpec((TM, tn), lambda j, i: (i, j)),
                  pl.BlockSpec((1, 1, tn), lambda j, i: (i, 0, gate_chunk * nj + j))],
        out_specs=pl.BlockSpec((TM, tn), lambda j, i: (i, j)),
        out_shape=jax.ShapeDtypeStruct((R, D), F32),
        compiler_params=_cp(("arbitrary", "arbitrary")),
        name=name,
    )(t, w, x, modrb)


def _ffn_in_kernel(x_ref, g_ref, sh_ref, sc_ref, wa_ref, wb_ref, o_ref):
    h = _modulate(x_ref, g_ref, sh_ref, sc_ref).astype(BF16)
    a = _dot(h, wa_ref[...])
    b = _dot(h, wb_ref[...])
    o_ref[...] = (a * _sigmoid(a) * b).astype(BF16)


def _ffn_in(x, g, modrb, w):
    tn = 1408
    nj = D_FF // tn
    return pl.pallas_call(
        _ffn_in_kernel,
        grid=(nj, NBLK),
        in_specs=[pl.BlockSpec((TM, D), lambda j, i: (i, 0)),
                  pl.BlockSpec((1, D), lambda j, i: (0, 0)),
                  pl.BlockSpec((1, 1, D), lambda j, i: (i, 0, 3)),
                  pl.BlockSpec((1, 1, D), lambda j, i: (i, 0, 4)),
                  pl.BlockSpec((D, tn), lambda j, i: (0, j)),
                  pl.BlockSpec((D, tn), lambda j, i: (0, nj + j))],
        out_specs=pl.BlockSpec((TM, tn), lambda j, i: (i, j)),
        out_shape=jax.ShapeDtypeStruct((R, D_FF), BF16),
        compiler_params=_cp(("arbitrary", "arbitrary")),
        name="ffn_in",
    )(x, g.reshape(1, D), modrb, modrb, w, w)


def _final_norm_kernel(x_ref, g_ref, o_ref):
    o_ref[...] = _rms(x_ref[...], g_ref[...])


def _final_norm(x, g):
    return pl.pallas_call(
        _final_norm_kernel,
        grid=(NL // TM,),
        in_specs=[pl.BlockSpec((TM, D), lambda i: (i, 0)),
                  pl.BlockSpec((1, D), lambda i: (0, 0))],
        out_specs=pl.BlockSpec((TM, D), lambda i: (i, 0)),
        out_shape=jax.ShapeDtypeStruct((NL, D), F32),
        compiler_params=_cp(("arbitrary",)),
        name="final_norm",
    )(x, g.reshape(1, D))


def _rope_tables():
    rows = T // GRID_W
    rr, cc = jnp.meshgrid(jnp.arange(rows, dtype=F32), jnp.arange(GRID_W, dtype=F32), indexing='ij')
    rr, cc = rr.reshape(-1), cc.reshape(-1)
    half = ROPE // 2
    inv = 1.0 / (ROPE_THETA ** (jnp.arange(0, half, 2, dtype=F32) / half))
    ang = jnp.stack([rr[:, None] * inv, cc[:, None] * inv], axis=1)
    cos = jnp.cos(ang)
    sin = jnp.sin(ang)
    cos_f = jnp.stack([cos, cos], axis=2).reshape(T, ROPE)
    sin_f = jnp.stack([-sin, sin], axis=2).reshape(T, ROPE)
    pad = lambda a: jnp.concatenate([a, jnp.zeros((a.shape[0], 128 - ROPE), F32)], axis=1)
    cos_l, sin_l = pad(cos_f), pad(sin_f)
    cos_c = pad(jnp.ones((NCX, ROPE), F32))
    sin_c = jnp.zeros((NCX, 128), F32)
    return (jnp.concatenate([cos_l] * NB_ + [cos_c], axis=0),
            jnp.concatenate([sin_l] * NB_ + [sin_c], axis=0))


_ROPE_PARTNER = np.arange(ROPE).reshape(2, 2, ROPE // 4)[:, ::-1, :].reshape(-1)


def _cols(w, name):
    o, n = OFF[name]
    return w[:, o:o + n]


def _layer_weights(w_in, w_uq, w_ukv):
    wmain = jnp.concatenate([_cols(w_in, n) for n in ('cq', 'ckv', 'xm', 'om', 'us', 'gates')], axis=1).astype(BF16)
    kr = _cols(w_in, 'krope')
    z64 = jnp.zeros((D, 64), F32)
    wsmall = jnp.concatenate([kr, z64, kr[:, _ROPE_PARTNER], z64, _cols(w_in, 'gm'),
                              jnp.zeros((D, 128 - 4 * H_M), F32)], axis=1).astype(BF16)
    uq = w_uq.reshape(Q_LORA, H_A, NOPE + ROPE)
    wqn = uq[:, :, :NOPE].reshape(Q_LORA, H_A * NOPE)
    qr = uq[:, :, NOPE:]
    zq = jnp.zeros_like(qr)
    wqr = jnp.concatenate([qr, zq], axis=2).reshape(Q_LORA, H_A * 128)
    wqp = jnp.concatenate([qr[:, :, _ROPE_PARTNER], zq], axis=2).reshape(Q_LORA, H_A * 128)
    ukv = w_ukv.reshape(KV_LORA, H_A, NOPE + DV)
    wkn = ukv[:, :, :NOPE].reshape(KV_LORA, H_A * NOPE)
    wv = ukv[:, :, NOPE:].reshape(KV_LORA, H_A * DV)
    return wmain, wsmall, tuple(a.astype(BF16) for a in (wqn, wqr, wqp, wkn, wv))


def kernel(x, c, ctx, c_ctx, w_ada, b_ada, g_mix, g_ffn, w_in, g_cq, w_uq, g_ckv, w_ukv, conv_m, w_qkv_m, b_gate_m, g_h_m, s5_a_re, s5_a_im, s5_log_dt, s5_b_re, s5_b_im, s5_c_re, s5_c_im, s5_d, w_glu, w_br_a, w_br_m, w_br_s, w_out, w_ffn_in, w_ffn_out, g_final):
    depth = w_ada.shape[0]
    xs = jnp.concatenate([x.reshape(NL, D), ctx.reshape(NCX, D)], axis=0)
    cvec = jnp.concatenate([c, c_ctx[None], jnp.zeros((8 - NB_ - 1, D), F32)], axis=0)
    mod = _ada(cvec, w_ada, b_ada)
    blocks_per_batch = T // TM
    rowsel = np.array([b for b in range(NB_) for _ in range(blocks_per_batch)] + [NB_] * (NCX // TM))
    cosk, sink = _rope_tables()

    for l in range(depth):
        modrb = mod[l][rowsel].reshape(NBLK, 1, 6 * D)
        wmain, wsmall, mla_w = _layer_weights(w_in[l], w_uq[l], w_ukv[l])
        p = _inproj(xs, g_mix[l], modrb, wmain, 1024, BF16, "inproj")
        psm = _inproj(xs, g_mix[l], modrb, wsmall, 384, F32, "inproj_small")

        q, k, v = _mla_proj(p, psm, cosk, sink, g_cq[l], g_ckv[l], *mla_w)
        a = _attention(q, k, v)

        wq, wk, wv = (w_qkv_m[l, i].astype(BF16) for i in range(3))
        qm, km, vm = _mprep(p, conv_m[l], wq, wk, wv)
        gm = psm[:, 256:256 + 4 * H_M]
        gd = jnp.stack([gm[:, :2 * H_M], gm[:, 2 * H_M:]], axis=0)
        bd = b_gate_m[l].reshape(2, 1, 2 * H_M)
        hdir = _mlstm(qm, km, vm, gd, gd.transpose(0, 2, 1), bd, bd.transpose(0, 2, 1))

        toep, wst, wout, apow, dflat = _s5_params(s5_a_re[l], s5_a_im[l], s5_log_dt[l], s5_b_re[l], s5_b_im[l],
                                                  s5_c_re[l], s5_c_im[l], s5_d[l])
        us = p[:, 3072:4096]
        xg = us.reshape(NCH, SL, G_S, GROUP).transpose(2, 0, 1, 3).reshape(G_S, NCH, SL * GROUP)
        fre, fim, gre, gim = _s5_state_in(xg, wst)
        sfr, sfi, sbr, sbi = _s5_scan(fre, fim, gre, gim, apow)
        yg = _s5_out(xg, toep, sfr, sfi, sbr, sbi, wout, dflat)
        y = yg.reshape(G_S, NCH, SL, GROUP).transpose(1, 2, 0, 3).reshape(R, D_S)
        s = _glu(y, w_glu[l].astype(BF16))

        t = _merge(a, hdir, p, g_h_m[l], s, w_br_a[l].astype(BF16), w_br_m[l].astype(BF16), w_br_s[l].astype(BF16))
        xs = _resid(t, w_out[l].astype(BF16), xs, modrb, 2, 1024, "resid_mix")
        u = _ffn_in(xs, g_ffn[l], modrb, w_ffn_in[l].astype(BF16))
        xs = _resid(u, w_ffn_out[l].astype(BF16), xs, modrb, 5, 512, "resid_ffn")

    return _final_norm(xs, g_final).reshape(NB_, T, D)
```

```python
import math

import numpy as np
import jax
import jax.numpy as jnp
from jax import lax
from jax.experimental import pallas as pl
from jax.experimental.pallas import tpu as pltpu

F32 = jnp.float32
BF16 = jnp.bfloat16

D = 2048
NB_ = 2
T = 4096
TC = 256
GRID_W = 64
EPS = 1e-6
H_A, Q_LORA, KV_LORA, NOPE, ROPE, DV = 8, 512, 512, 128, 64, 128
ROPE_THETA = 10000.0
ATTN_SCALE = (NOPE + ROPE) ** -0.5
H_M, DH_M = 8, 128
D_M = H_M * DH_M
D_S, GROUP, P_S = 1024, 16, 64
G_S = D_S // GROUP
D_FF = ((8 * D // 3 + 255) // 256) * 256
OFF = {}
_o = 0
for _n, _w in (('cq', Q_LORA), ('ckv', KV_LORA), ('krope', ROPE), ('xm', D_M), ('om', D_M),
               ('gm', 4 * H_M), ('us', D_S), ('gates', 3 * D)):
    OFF[_n] = (_o, _w)
    _o += _w

NL = NB_ * T
NCX = NB_ * TC
R = NL + NCX
TM = 512
NBLK = R // TM
LC = 256
TQ = 256
TK = 512
SL = 8
NCH = R // SL
NSEG = 8
SLB = 128 // GROUP
NLB = D_S // 128
SW = SLB * P_S
LOG2E = math.log2(math.e)
PM_CQ, PM_CKV, PM_XM, PM_OM, PM_GATES, PM_W = 0, 512, 1024, 2048, 3072, 3072 + 3 * D
SM_US, SM_KR, SM_KP, SM_GM, SM_W = 0, D_S, D_S + 128, D_S + 256, D_S + 384
VMEM_LIMIT = 52 * 1024 * 1024
SEQ_STARTS = tuple(b * T for b in range(NB_)) + tuple(NL + b * TC for b in range(NB_))
SEQ_ENDS = tuple(b * T + T - 1 for b in range(NB_)) + tuple(NL + b * TC + TC - 1 for b in range(NB_))


def _cp(sem):
    return pltpu.CompilerParams(dimension_semantics=sem, vmem_limit_bytes=VMEM_LIMIT)


def _dot(a, b):
    return jnp.dot(a, b, preferred_element_type=F32)


def _dot_nt(a, b):
    return lax.dot_general(a, b, (((1,), (1,)), ((), ())), preferred_element_type=F32)


def _dot_tn(a, b):
    return lax.dot_general(a, b, (((0,), (0,)), ((), ())), preferred_element_type=F32)


def _sigmoid(x):
    return 1.0 / (1.0 + jnp.exp(-x))


def _log_sigmoid(x):
    return jnp.minimum(x, 0.0) - jnp.log(1.0 + jnp.exp(-jnp.abs(x)))


def _rms(x, g):
    xf = x.astype(F32)
    return xf * lax.rsqrt(jnp.mean(xf * xf, axis=-1, keepdims=True) + EPS) * g


def _ada_kernel(c_ref, w_ref, b_ref, o_ref):
    c = c_ref[...]
    s = (c * _sigmoid(c)).astype(BF16)
    o_ref[0] = _dot(s, w_ref[0].astype(BF16)) + b_ref[0]


def _ada(cvec, w_ada, b_ada):
    depth = w_ada.shape[0]
    tn = 1024
    return pl.pallas_call(
        _ada_kernel,
        grid=(depth, 6 * D // tn),
        in_specs=[pl.BlockSpec((8, D), lambda l, j: (0, 0)),
                  pl.BlockSpec((1, D, tn), lambda l, j: (l, 0, j)),
                  pl.BlockSpec((1, 1, tn), lambda l, j: (l, 0, j))],
        out_specs=pl.BlockSpec((1, 8, tn), lambda l, j: (l, 0, j)),
        out_shape=jax.ShapeDtypeStruct((depth, 8, 6 * D), F32),
        compiler_params=_cp(("arbitrary", "arbitrary")),
        name="ada",
    )(cvec, w_ada, b_ada.reshape(depth, 1, 6 * D))


def _modulate(x_ref, g_ref, sh_ref, sc_ref):
    return _rms(x_ref[...], g_ref[0]) * (1.0 + sc_ref[0]) + sh_ref[0]


def _inproj_kernel(x_ref, g_ref, sh_ref, sc_ref, w_ref, o_ref):
    h = _modulate(x_ref, g_ref, sh_ref, sc_ref).astype(BF16)
    o_ref[...] = _dot(h, w_ref[0]).astype(o_ref.dtype)


def _inproj(x, g, modrb, w, l, tn, out_dtype, name):
    n = w.shape[2]
    return pl.pallas_call(
        _inproj_kernel,
        grid=(n // tn, NBLK),
        in_specs=[pl.BlockSpec((TM, D), lambda j, i: (i, 0)),
                  pl.BlockSpec((1, 1, D), lambda j, i: (l, 0, 0)),
                  pl.BlockSpec((1, 1, D), lambda j, i: (i, 0, 0)),
                  pl.BlockSpec((1, 1, D), lambda j, i: (i, 0, 1)),
                  pl.BlockSpec((1, D, tn), lambda j, i: (l, 0, j))],
        out_specs=pl.BlockSpec((TM, tn), lambda j, i: (i, j)),
        out_shape=jax.ShapeDtypeStruct((R, n), out_dtype),
        compiler_params=_cp(("arbitrary", "arbitrary")),
        name=name,
    )(x, g, modrb, modrb, w)


def _mla_proj_kernel(cq_ref, ckv_ref, kr_ref, kp_ref, cos_ref, sin_ref, gq_ref, gkv_ref,
                     wqn_ref, wqr_ref, wqp_ref, wkn_ref, wvt_ref, q_ref, k_ref, vt_ref):
    hq = _rms(cq_ref[...], gq_ref[0]).astype(BF16)
    hk = _rms(ckv_ref[...], gkv_ref[0]).astype(BF16)
    cos = cos_ref[...]
    sin = sin_ref[...]
    qscale = ATTN_SCALE * LOG2E
    qn = _dot(hq, wqn_ref[0]) * qscale
    qr = _dot(hq, wqr_ref[0])
    qp = _dot(hq, wqp_ref[0])
    kn = _dot(hk, wkn_ref[0])
    vt = _dot_nt(wvt_ref[0], hk)
    kr = (kr_ref[...] * cos + kp_ref[...] * sin).astype(BF16)
    ones = jnp.ones((DV, TM), BF16)
    for h in range(H_A):
        lo = slice(h * 256, h * 256 + 128)
        hi = slice(h * 256 + 128, (h + 1) * 256)
        sl = slice(h * 128, (h + 1) * 128)
        q_ref[:, lo] = qn[:, sl].astype(BF16)
        q_ref[:, hi] = ((qr[:, sl] * cos + qp[:, sl] * sin) * qscale).astype(BF16)
        k_ref[:, lo] = kn[:, sl].astype(BF16)
        k_ref[:, hi] = kr
        vt_ref[lo, :] = vt[sl, :].astype(BF16)
        vt_ref[hi, :] = ones


def _mla_proj(p, psm, cosk, sink, gq, gkv, wqn, wqr, wqp, wkn, wvt, l):
    lw = lambda shp: pl.BlockSpec((1,) + shp, lambda i: (l, 0, 0))
    return pl.pallas_call(
        _mla_proj_kernel,
        grid=(NBLK,),
        in_specs=[pl.BlockSpec((TM, 512), lambda i: (i, PM_CQ // 512)),
                  pl.BlockSpec((TM, 512), lambda i: (i, PM_CKV // 512)),
                  pl.BlockSpec((TM, 128), lambda i: (i, SM_KR // 128)),
                  pl.BlockSpec((TM, 128), lambda i: (i, SM_KP // 128)),
                  pl.BlockSpec((TM, 128), lambda i: (i, 0)),
                  pl.BlockSpec((TM, 128), lambda i: (i, 0)),
                  pl.BlockSpec((1, 1, 512), lambda i: (l, 0, 0)),
                  pl.BlockSpec((1, 1, 512), lambda i: (l, 0, 0)),
                  lw((512, 1024)), lw((512, 1024)), lw((512, 1024)), lw((512, 1024)), lw((1024, 512))],
        out_specs=[pl.BlockSpec((TM, 2048), lambda i: (i, 0)),
                   pl.BlockSpec((TM, 2048), lambda i: (i, 0)),
                   pl.BlockSpec((2048, TM), lambda i: (0, i))],
        out_shape=[jax.ShapeDtypeStruct((R, 2048), BF16),
                   jax.ShapeDtypeStruct((R, 2048), BF16),
                   jax.ShapeDtypeStruct((2048, R), BF16)],
        compiler_params=_cp(("arbitrary",)),
        name="mla_proj",
    )(p, p, psm, psm, cosk, sink, gq, gkv, wqn, wqr, wqp, wkn, wvt)


def _attn_kernel(q_ref, kl_ref, kc_ref, vl_ref, vc_ref, o_ref, m_ref, acc_ref):
    qi = pl.program_id(2)
    q = q_ref[...]

    def chunk(k, vt, m, acc):
        st = _dot_nt(k, q)
        m_new = jnp.maximum(m, jnp.max(st, axis=0, keepdims=True))
        p = jnp.exp2(st - m_new).astype(BF16)
        return m_new, jnp.exp2(m - m_new) * acc + _dot(vt, p)

    m0 = jnp.full((1, TQ), -jnp.inf, F32)
    a0 = jnp.zeros((2 * DV, TQ), F32)
    m, acc = chunk(kc_ref[...], vc_ref[...], m0, a0)
    m_ref[...] = m
    acc_ref[...] = acc

    @pl.when(qi < T // TQ)
    def _():
        m = m_ref[...]
        acc = acc_ref[...]
        for j in range(T // TK):
            m, acc = chunk(kl_ref[j * TK:(j + 1) * TK, :], vl_ref[:, j * TK:(j + 1) * TK], m, acc)
        acc_ref[...] = acc

    acc = acc_ref[...]
    o_ref[...] = jnp.transpose(acc[0:DV, :] / acc[DV:2 * DV, :]).astype(o_ref.dtype)


def _attention(q, k, vt):
    nq = T // TQ
    qrow = lambda b, h, i: jnp.where(i < nq, b * nq + i, NL // TQ + b)
    return pl.pallas_call(
        _attn_kernel,
        grid=(NB_, H_A, nq + 1),
        in_specs=[pl.BlockSpec((TQ, 256), lambda b, h, i: (qrow(b, h, i), h)),
                  pl.BlockSpec((T, 256), lambda b, h, i: (b, h)),
                  pl.BlockSpec((TC, 256), lambda b, h, i: (NL // TC + b, h)),
                  pl.BlockSpec((256, T), lambda b, h, i: (h, b)),
                  pl.BlockSpec((256, TC), lambda b, h, i: (h, NL // TC + b))],
        out_specs=pl.BlockSpec((TQ, DV), lambda b, h, i: (qrow(b, h, i), h)),
        out_shape=jax.ShapeDtypeStruct((R, H_A * DV), BF16),
        scratch_shapes=[pltpu.VMEM((1, TQ), F32), pltpu.VMEM((2 * DV, TQ), F32)],
        compiler_params=_cp(("arbitrary", "arbitrary", "arbitrary")),
        name="attention",
    )(q, k, k, vt, vt)


def _row_in(grow, rows):
    hit = grow == rows[0]
    for r in rows[1:]:
        hit = jnp.logical_or(hit, grow == r)
    return hit


def _mprep_kernel(x_ref, xp_ref, xn_ref, cw_ref, wq_ref, q_ref, k_ref, v_ref):
    i = pl.program_id(0)
    xb = x_ref[...]
    x = xb.astype(F32)
    prev_row = xp_ref[15:16, :].astype(F32)
    next_row = xn_ref[0:1, :].astype(F32)
    row = lax.broadcasted_iota(jnp.int32, (TM, 1), 0)
    grow = row + i * TM
    xprev = jnp.where(row == 0, prev_row, pltpu.roll(x, 1, axis=0))
    xprev = jnp.where(_row_in(grow, SEQ_STARTS), 0.0, xprev)
    xnext = jnp.where(row == TM - 1, next_row, pltpu.roll(x, TM - 1, axis=0))
    xnext = jnp.where(_row_in(grow, SEQ_ENDS), 0.0, xnext)
    cw = cw_ref[0]
    xc = xprev * cw[0:1, :] + x * cw[1:2, :] + xnext * cw[2:3, :]
    xcb = (xc * _sigmoid(xc)).astype(BF16)
    for h in range(H_M):
        sl = slice(h * DH_M, (h + 1) * DH_M)
        q_ref[:, sl] = _dot(xcb[:, sl], wq_ref[0, 0, h]).astype(BF16)
        k_ref[:, sl] = (_dot(xcb[:, sl], wq_ref[0, 1, h]) * DH_M ** -0.5).astype(BF16)
        v_ref[:, sl] = _dot(xb[:, sl], wq_ref[0, 2, h]).astype(BF16)


def _mprep(p, conv_w, w_qkv, l):
    hb = TM // 16
    xcol = PM_XM // D_M
    return pl.pallas_call(
        _mprep_kernel,
        grid=(NBLK,),
        in_specs=[pl.BlockSpec((TM, D_M), lambda i: (i, xcol)),
                  pl.BlockSpec((16, D_M), lambda i: (jnp.maximum(i * hb - 1, 0), xcol)),
                  pl.BlockSpec((16, D_M), lambda i: (jnp.minimum((i + 1) * hb, R // 16 - 1), xcol)),
                  pl.BlockSpec((1, 3, D_M), lambda i: (l, 0, 0)),
                  pl.BlockSpec((1, 3, H_M, DH_M, DH_M), lambda i: (l, 0, 0, 0, 0))],
        out_specs=[pl.BlockSpec((TM, D_M), lambda i: (i, 0))] * 3,
        out_shape=[jax.ShapeDtypeStruct((R, D_M), BF16)] * 3,
        compiler_params=_cp(("arbitrary",)),
        name="mlstm_prep",
    )(p, p, p, conv_w, w_qkv)


def _split3(x):
    h = x.astype(BF16)
    r = x - h.astype(F32)
    m = r.astype(BF16)
    l = (r - m.astype(F32)).astype(BF16)
    return h, m, l


def _mlstm_kernel(q_ref, k_ref, v_ref, g_ref, gt_ref, b_ref, bt_ref, h_ref, c_ref, n_ref, m_ref):
    d = pl.program_id(1)
    s = pl.program_id(2)

    @pl.when(s == 0)
    def _():
        c_ref[...] = jnp.zeros_like(c_ref)
        n_ref[...] = jnp.zeros_like(n_ref)
        m_ref[...] = jnp.zeros_like(m_ref)

    sign = 1 - 2 * d
    r = lax.broadcasted_iota(jnp.int32, (LC, LC), 0)
    c = lax.broadcasted_iota(jnp.int32, (LC, LC), 1)
    mask = (r - c) * sign >= 0
    tri = jnp.where(mask, 1.0, 0.0).astype(BF16)
    g = g_ref[0] + b_ref[0]
    gt = gt_ref[0] + bt_ref[0]
    ig_c = g[:, 0:H_M]
    lf_c = _log_sigmoid(g[:, H_M:2 * H_M])
    ig_r = gt[0:H_M, :]
    lf_r = _log_sigmoid(gt[H_M:2 * H_M, :])
    b_c = sum(_dot(tri, part) for part in _split3(lf_c))
    b_r = sum(_dot_nt(part, tri) for part in _split3(lf_r))
    tot = jnp.sum(lf_c, axis=0, keepdims=True)

    for h in range(H_M):
        sl = slice(h * DH_M, (h + 1) * DH_M)
        qh = q_ref[:, sl]
        kh = k_ref[:, sl]
        vh = v_ref[:, sl]
        bc = b_c[:, h:h + 1]
        br = b_r[h:h + 1, :]
        igr = ig_r[h:h + 1, :]
        igc = ig_c[:, h:h + 1]
        tt = tot[:, h:h + 1]
        mp = m_ref[h:h + 1, 0:1]
        dm = jnp.where(mask, bc - br + igr, -jnp.inf)
        m_inter = bc + mp
        m_t = jnp.maximum(jnp.max(dm, axis=1, keepdims=True), m_inter)
        sc = _dot_nt(qh, kh) * jnp.exp(dm - m_t)
        inter = jnp.exp(m_inter - m_t)
        cst = c_ref[h]
        nrow = n_ref[h:h + 1, :]
        num = _dot(sc.astype(BF16), vh) + inter * _dot(qh, cst.astype(BF16))
        den = (jnp.sum(sc, axis=1, keepdims=True)
               + inter * jnp.sum(qh.astype(F32) * nrow, axis=1, keepdims=True))
        h_ref[0, :, sl] = num / jnp.maximum(jnp.abs(den), jnp.exp(-m_t))
        gg = tt - bc + igc
        m_new = jnp.maximum(tt + mp, jnp.max(gg, axis=0, keepdims=True))
        w = jnp.exp(gg - m_new)
        decay = jnp.exp(tt + mp - m_new)
        kw = kh.astype(F32) * w
        c_ref[h] = decay * cst + _dot_tn(kw.astype(BF16), vh)
        n_ref[h:h + 1, :] = decay * nrow + jnp.sum(kw, axis=0, keepdims=True)
        m_ref[h:h + 1, :] = jnp.broadcast_to(m_new, (1, DH_M))


def _mlstm(q, k, v, gd, gtd, bd, btd):
    nl = T // LC
    nsteps = nl + TC // LC

    def rb(b, d, s):
        lat = b * nl + jnp.where(d == 0, s - 1, nl - s)
        return jnp.where(s == 0, NL // LC + b, lat)

    qspec = pl.BlockSpec((LC, D_M), lambda b, d, s: (rb(b, d, s), 0))
    return pl.pallas_call(
        _mlstm_kernel,
        grid=(NB_, 2, nsteps),
        in_specs=[qspec, qspec, qspec,
                  pl.BlockSpec((1, LC, 16), lambda b, d, s: (d, rb(b, d, s), 0)),
                  pl.BlockSpec((1, 16, LC), lambda b, d, s: (d, 0, rb(b, d, s))),
                  pl.BlockSpec((1, 1, 16), lambda b, d, s: (d, 0, 0)),
                  pl.BlockSpec((1, 16, 1), lambda b, d, s: (d, 0, 0))],
        out_specs=pl.BlockSpec((1, LC, D_M), lambda b, d, s: (d, rb(b, d, s), 0)),
        out_shape=jax.ShapeDtypeStruct((2, R, D_M), F32),
        scratch_shapes=[pltpu.VMEM((H_M, DH_M, DH_M), F32),
                        pltpu.VMEM((H_M, DH_M), F32),
                        pltpu.VMEM((H_M, DH_M), F32)],
        compiler_params=_cp(("arbitrary", "arbitrary", "arbitrary")),
        name="mlstm_scan",
    )(q, k, v, gd, gtd, bd, btd)


def _fold_rows(us_ref):
    return jnp.concatenate([us_ref[pl.ds(t, NCH, stride=SL), :] for t in range(SL)], axis=1)


def _cmul(ar, ai, sr, si):
    return ar * sr - ai * si, ar * si + ai * sr


def _s5_state_kernel(us_ref, wst_ref, a_ref, s_ref, e_ref):
    e = _dot(_fold_rows(us_ref).astype(BF16), wst_ref[0])
    ncb = SW // 128
    for cb in range(4 * ncb):
        e_ref[cb] = e[:, cb * 128:(cb + 1) * 128]
    apow = a_ref[0]
    step_a = ((apow[0:1, :], apow[1:2, :]), (apow[2:3, :], apow[3:4, :]))
    nctx, nlat = TC // SL, T // SL
    seg = nlat // NSEG

    def load(rows, comp):
        return jnp.concatenate([e_ref[comp * ncb + i, rows, :] for i in range(ncb)], axis=1)

    def store(rows, comp, val):
        for i in range(ncb):
            e_ref[comp * ncb + i, rows, :] = val[:, i * 128:(i + 1) * 128]

    def advance(rows, d, st, write):
        dr = load(rows, 2 * d)
        di = load(rows, 2 * d + 1)
        if write:
            store(rows, 2 * d, st[0])
            store(rows, 2 * d + 1, st[1])
        nr, ni = _cmul(step_a[d][0], step_a[d][1], st[0], st[1])
        return nr + dr, ni + di

    def sweep(nsteps, rows_of, init, write):
        def body(kk, sts):
            out = []
            for b in range(NB_):
                out.append(advance(rows_of(b, kk), 0, sts[2 * b], write))
                out.append(advance(rows_of(b, nsteps - 1 - kk), 1, sts[2 * b + 1], write))
            return tuple(out)
        return lax.fori_loop(0, nsteps, body, init)

    z1 = jnp.zeros((1, SW), F32)
    carry = sweep(nctx, lambda b, kk: pl.ds(NL // SL + b * nctx + kk, 1), ((z1, z1),) * (2 * NB_), True)

    seg_rows = lambda b, kk: pl.ds(b * nlat + kk, NSEG, stride=seg)
    z8 = jnp.zeros((NSEG, SW), F32)
    ends = sweep(seg, seg_rows, ((z8, z8),) * (2 * NB_), False)
    inits = []
    for ch in range(2 * NB_):
        d = ch % 2
        ar, ai = step_a[d]
        for _ in range(int(math.log2(seg))):
            ar, ai = ar * ar - ai * ai, 2.0 * ar * ai
        cur = carry[ch]
        rows_r, rows_i = [None] * NSEG, [None] * NSEG
        for kseg in (range(NSEG) if d == 0 else range(NSEG - 1, -1, -1)):
            rows_r[kseg], rows_i[kseg] = cur
            nr, ni = _cmul(ar, ai, cur[0], cur[1])
            cur = (nr + ends[ch][0][kseg:kseg + 1, :], ni + ends[ch][1][kseg:kseg + 1, :])
        inits.append((jnp.concatenate(rows_r, axis=0), jnp.concatenate(rows_i, axis=0)))
    sweep(seg, seg_rows, tuple(inits), True)
    for cb in range(4 * ncb):
        s_ref[0, :, cb * 128:(cb + 1) * 128] = e_ref[cb].astype(BF16)


def _s5_state(psm, wst, apow, l):
    return pl.pallas_call(
        _s5_state_kernel,
        grid=(NLB,),
        in_specs=[pl.BlockSpec((R, 128), lambda g: (0, SM_US // 128 + g), pipeline_mode=pl.Buffered(1)),
                  pl.BlockSpec((1, SL * 128, 4 * SW), lambda g: (l * NLB + g, 0, 0), pipeline_mode=pl.Buffered(1)),
                  pl.BlockSpec((1, 8, SW), lambda g: (l * NLB + g, 0, 0))],
        out_specs=pl.BlockSpec((1, NCH, 4 * SW), lambda g: (g, 0, 0)),
        out_shape=jax.ShapeDtypeStruct((NLB, NCH, 4 * SW), BF16),
        scratch_shapes=[pltpu.VMEM((4 * SW // 128, NCH, 128), F32)],
        compiler_params=_cp(("arbitrary",)),
        name="s5_state",
    )(psm, wst, apow)


def _s5_out_kernel(us_ref, toep_ref, s_ref, wo_ref, d_ref, y_ref):
    x = _fold_rows(us_ref)
    y = _dot(x.astype(BF16), toep_ref[0]) + _dot(s_ref[0], wo_ref[0]) + d_ref[0] * x
    y = jax.nn.gelu(y)
    for t in range(SL):
        y_ref[pl.ds(t, NCH, stride=SL), :] = y[:, t * 128:(t + 1) * 128]


def _s5_out(psm, toep, s, wout, dflat, l):
    return pl.pallas_call(
        _s5_out_kernel,
        grid=(NLB,),
        in_specs=[pl.BlockSpec((R, 128), lambda g: (0, SM_US // 128 + g), pipeline_mode=pl.Buffered(1)),
                  pl.BlockSpec((1, SL * 128, SL * 128), lambda g: (l * NLB + g, 0, 0)),
                  pl.BlockSpec((1, NCH, 4 * SW), lambda g: (g, 0, 0)),
                  pl.BlockSpec((1, 4 * SW, SL * 128), lambda g: (l * NLB + g, 0, 0)),
                  pl.BlockSpec((1, 1, SL * 128), lambda g: (l * NLB + g, 0, 0))],
        out_specs=pl.BlockSpec((R, 128), lambda g: (0, g)),
        out_shape=jax.ShapeDtypeStruct((R, D_S), F32),
        compiler_params=_cp(("arbitrary",)),
        name="s5_out",
    )(psm, toep, s, wout, dflat)


def _glu_kernel(y_ref, w_ref, o_ref):
    z = _dot(y_ref[...].astype(BF16), w_ref[0])
    o_ref[...] = (z[:, :D_S] * _sigmoid(z[:, D_S:])).astype(BF16)


def _glu(y, w, l):
    return pl.pallas_call(
        _glu_kernel,
        grid=(NBLK,),
        in_specs=[pl.BlockSpec((TM, D_S), lambda i: (i, 0)),
                  pl.BlockSpec((1, D_S, 2 * D_S), lambda i: (l, 0, 0))],
        out_specs=pl.BlockSpec((TM, D_S), lambda i: (i, 0)),
        out_shape=jax.ShapeDtypeStruct((R, D_S), BF16),
        compiler_params=_cp(("arbitrary",)),
        name="s5_glu",
    )(y, w)


def _s5_params(a_re, a_im, log_dt, b_re, b_im, c_re, c_im, dskip):
    dt = jnp.exp(log_dt)[:, :, None]
    lam_r, lam_i = a_re * dt, a_im * dt
    mag = jnp.exp(lam_r)
    ar, ai = mag * jnp.cos(lam_i), mag * jnp.sin(lam_i)
    den = a_re * a_re + a_im * a_im
    nr, ni = ar - 1.0, ai
    cr = (nr * a_re + ni * a_im) / den
    ci = (ni * a_re - nr * a_im) / den
    cpr = c_re[None] * cr[:, :, None, :] - c_im[None] * ci[:, :, None, :]
    cpi = c_re[None] * ci[:, :, None, :] + c_im[None] * cr[:, :, None, :]
    j = jnp.arange(SL + 1, dtype=F32)[:, None, None, None]
    pm = jnp.exp(lam_r[None] * j)
    pr, pi = pm * jnp.cos(lam_i[None] * j), pm * jnp.sin(lam_i[None] * j)
    abr = pr[..., None] * b_re[None, None] - pi[..., None] * b_im[None, None]
    abi = pr[..., None] * b_im[None, None] + pi[..., None] * b_re[None, None]
    kern = (jnp.einsum('dgcp,jdgpe->jdgce', cpr, abr[:SL]) - jnp.einsum('dgcp,jdgpe->jdgce', cpi, abi[:SL]))
    eye = jnp.eye(SLB, dtype=F32)
    zero = jnp.zeros_like(kern[0, 0])
    rows = []
    for t_in in range(SL):
        blks = []
        for t_out in range(SL):
            blk = zero
            if t_out >= t_in:
                blk = blk + kern[t_out - t_in, 0]
            if t_in >= t_out:
                blk = blk + kern[t_in - t_out, 1]
            blks.append(blk)
        rows.append(jnp.stack(blks, axis=0))
    t5 = jnp.stack(rows, axis=0).reshape(SL, SL, NLB, SLB, GROUP, GROUP)
    toep = jnp.einsum('iobgcd,gh->bigdohc', t5, eye).reshape(NLB, SL * 128, SL * 128)
    rev = jnp.arange(SL - 1, -1, -1)
    ws = jnp.stack([abr[rev, 0], abi[rev, 0], abr[:SL, 1], abi[:SL, 1]], axis=0)
    ws = ws.reshape(4, SL, NLB, SLB, P_S, GROUP)
    wst = jnp.einsum('mtbgpd,gh->btgdmhp', ws, eye).reshape(NLB, SL * 128, 4 * SW)
    def readout(pw_r, pw_i, d):
        re = cpr[d][None] * pw_r[:, :, None, :] - cpi[d][None] * pw_i[:, :, None, :]
        im = cpr[d][None] * pw_i[:, :, None, :] + cpi[d][None] * pw_r[:, :, None, :]
        return re, -im
    of_re, of_im = readout(pr[1:SL + 1, 0], pi[1:SL + 1, 0], 0)
    ob_re, ob_im = readout(pr[SL - np.arange(SL), 1], pi[SL - np.arange(SL), 1], 1)
    wo = jnp.stack([of_re, of_im, ob_re, ob_im], axis=0).reshape(4, SL, NLB, SLB, GROUP, P_S)
    wout = jnp.einsum('mtbgcp,gh->bmgpthc', wo, eye).reshape(NLB, 4 * SW, SL * 128)
    blk = lambda a: a.reshape(NLB, 1, SW)
    apow = jnp.concatenate([blk(pr[SL, 0]), blk(pi[SL, 0]), blk(pr[SL, 1]), blk(pi[SL, 1]),
                            jnp.zeros((NLB, 4, SW), F32)], axis=1)
    dflat = jnp.tile(dskip.reshape(NLB, 1, 128), (1, 1, SL))
    return toep.astype(BF16), wst.astype(BF16), wout.astype(BF16), apow, dflat


def _merge_kernel(a_ref, hf_ref, hb_ref, om_ref, gh_ref, s_ref, ga_ref, gm_ref, gs_ref,
                  wa_ref, wm_ref, ws_ref, o_ref):
    hsum = hf_ref[0] + hb_ref[0]
    parts = []
    for h in range(H_M):
        xh = hsum[:, h * DH_M:(h + 1) * DH_M]
        parts.append(xh * lax.rsqrt(jnp.mean(xh * xh, axis=-1, keepdims=True) + EPS))
    hn = jnp.concatenate(parts, axis=1) * gh_ref[0]
    m = (_sigmoid(om_ref[...].astype(F32)) * hn).astype(BF16)
    t = (_sigmoid(ga_ref[...].astype(F32)) * _dot(a_ref[...], wa_ref[0])
         + _sigmoid(gm_ref[...].astype(F32)) * _dot(m, wm_ref[0])
         + _sigmoid(gs_ref[...].astype(F32)) * _dot(s_ref[...], ws_ref[0]))
    o_ref[...] = t.astype(BF16)


def _merge(a, hdir, p, gh, s, wa, wm, ws, l):
    tn = 1024
    nj = D // tn
    gcol = PM_GATES // tn
    row = lambda shp, c: pl.BlockSpec(shp, lambda j, i: (i, c))
    wspec = pl.BlockSpec((1, 1024, tn), lambda j, i: (l, 0, j))
    return pl.pallas_call(
        _merge_kernel,
        grid=(nj, NBLK),
        in_specs=[row((TM, 1024), 0),
                  pl.BlockSpec((1, TM, D_M), lambda j, i: (0, i, 0)),
                  pl.BlockSpec((1, TM, D_M), lambda j, i: (1, i, 0)),
                  row((TM, 1024), PM_OM // 1024),
                  pl.BlockSpec((1, 1, D_M), lambda j, i: (l, 0, 0)),
                  row((TM, 1024), 0),
                  pl.BlockSpec((TM, tn), lambda j, i: (i, gcol + j)),
                  pl.BlockSpec((TM, tn), lambda j, i: (i, gcol + nj + j)),
                  pl.BlockSpec((TM, tn), lambda j, i: (i, gcol + 2 * nj + j)),
                  wspec, wspec, wspec],
        out_specs=pl.BlockSpec((TM, tn), lambda j, i: (i, j)),
        out_shape=jax.ShapeDtypeStruct((R, D), BF16),
        compiler_params=_cp(("arbitrary", "arbitrary")),
        name="merge",
    )(a, hdir, hdir, p, gh, s, p, p, p, wa, wm, ws)


def _resid_kernel(t_ref, w_ref, x_ref, gt_ref, o_ref, wb_ref):
    @pl.when(pl.program_id(1) == 0)
    def _():
        wb_ref[...] = w_ref[0].astype(BF16)

    o_ref[...] = x_ref[...] + gt_ref[0] * _dot(t_ref[...], wb_ref[...])


def _resid(t, w, x, modrb, gate_chunk, tn, l, name):
    kdim = t.shape[1]
    nj = D // tn
    return pl.pallas_call(
        _resid_kernel,
        grid=(nj, NBLK),
        in_specs=[pl.BlockSpec((TM, kdim), lambda j, i: (i, 0)),
                  pl.BlockSpec((1, kdim, tn), lambda j, i: (l, 0, j)),
                  pl.BlockSpec((TM, tn), lambda j, i: (i, j)),
                  pl.BlockSpec((1, 1, tn), lambda j, i: (i, 0, gate_chunk * nj + j))],
        out_specs=pl.BlockSpec((TM, tn), lambda j, i: (i, j)),
        out_shape=jax.ShapeDtypeStruct((R, D), F32),
        scratch_shapes=[pltpu.VMEM((kdim, tn), BF16)],
        compiler_params=_cp(("arbitrary", "arbitrary")),
        name=name,
    )(t, w, x, modrb)


def _ffn_in_kernel(x_ref, g_ref, sh_ref, sc_ref, wa_ref, wb_ref, o_ref):
    h = _modulate(x_ref, g_ref, sh_ref, sc_ref).astype(BF16)
    a = _dot(h, wa_ref[0])
    b = _dot(h, wb_ref[0])
    o_ref[...] = (a * _sigmoid(a) * b).astype(BF16)


def _ffn_in(x, g, modrb, w, l):
    tn = 1408
    nj = D_FF // tn
    return pl.pallas_call(
        _ffn_in_kernel,
        grid=(nj, NBLK),
        in_specs=[pl.BlockSpec((TM, D), lambda j, i: (i, 0)),
                  pl.BlockSpec((1, 1, D), lambda j, i: (l, 0, 0)),
                  pl.BlockSpec((1, 1, D), lambda j, i: (i, 0, 3)),
                  pl.BlockSpec((1, 1, D), lambda j, i: (i, 0, 4)),
                  pl.BlockSpec((1, D, tn), lambda j, i: (l, 0, j)),
                  pl.BlockSpec((1, D, tn), lambda j, i: (l, 0, nj + j))],
        out_specs=pl.BlockSpec((TM, tn), lambda j, i: (i, j)),
        out_shape=jax.ShapeDtypeStruct((R, D_FF), BF16),
        compiler_params=_cp(("arbitrary", "arbitrary")),
        name="ffn_in",
    )(x, g, modrb, modrb, w, w)


def _final_norm_kernel(x_ref, g_ref, o_ref):
    o_ref[...] = _rms(x_ref[...], g_ref[...])


def _final_norm(x, g):
    return pl.pallas_call(
        _final_norm_kernel,
        grid=(NL // TM,),
        in_specs=[pl.BlockSpec((TM, D), lambda i: (i, 0)),
                  pl.BlockSpec((1, D), lambda i: (0, 0))],
        out_specs=pl.BlockSpec((TM, D), lambda i: (i, 0)),
        out_shape=jax.ShapeDtypeStruct((NL, D), F32),
        compiler_params=_cp(("arbitrary",)),
        name="final_norm",
    )(x, g.reshape(1, D))


def _rope_tables():
    rows = T // GRID_W
    rr, cc = jnp.meshgrid(jnp.arange(rows, dtype=F32), jnp.arange(GRID_W, dtype=F32), indexing='ij')
    rr, cc = rr.reshape(-1), cc.reshape(-1)
    half = ROPE // 2
    inv = 1.0 / (ROPE_THETA ** (jnp.arange(0, half, 2, dtype=F32) / half))
    ang = jnp.stack([rr[:, None] * inv, cc[:, None] * inv], axis=1)
    cos = jnp.cos(ang)
    sin = jnp.sin(ang)
    cos_f = jnp.stack([cos, cos], axis=2).reshape(T, ROPE)
    sin_f = jnp.stack([-sin, sin], axis=2).reshape(T, ROPE)
    pad = lambda a: jnp.concatenate([a, jnp.zeros((a.shape[0], 128 - ROPE), F32)], axis=1)
    cos_l, sin_l = pad(cos_f), pad(sin_f)
    cos_c = pad(jnp.ones((NCX, ROPE), F32))
    sin_c = jnp.zeros((NCX, 128), F32)
    return (jnp.concatenate([cos_l] * NB_ + [cos_c], axis=0),
            jnp.concatenate([sin_l] * NB_ + [sin_c], axis=0))


def _rope_partner(w):
    s = w.shape[:-1]
    w4 = w.reshape(s + (2, 2, ROPE // 4))
    return jnp.concatenate([w4[..., 1:2, :], w4[..., 0:1, :]], axis=-2).reshape(s + (ROPE,))


def _cols(w, name):
    o, n = OFF[name]
    return w[..., o:o + n]


def _prep_weights(w_in, w_uq, w_ukv):
    depth = w_in.shape[0]
    wmain = jnp.concatenate([_cols(w_in, n) for n in ('cq', 'ckv', 'xm', 'om', 'gates')], axis=-1).astype(BF16)
    kr = _cols(w_in, 'krope')
    z64 = jnp.zeros((depth, D, 64), F32)
    wsmall = jnp.concatenate([_cols(w_in, 'us'), kr, z64, _rope_partner(kr), z64, _cols(w_in, 'gm'),
                              jnp.zeros((depth, D, 128 - 4 * H_M), F32)], axis=-1).astype(BF16)
    uq = w_uq.reshape(depth, Q_LORA, H_A, NOPE + ROPE)
    wqn = uq[..., :NOPE].reshape(depth, Q_LORA, H_A * NOPE)
    qr = uq[..., NOPE:]
    zq = jnp.zeros_like(qr)
    wqr = jnp.concatenate([qr, zq], axis=-1).reshape(depth, Q_LORA, H_A * 128)
    wqp = jnp.concatenate([_rope_partner(qr), zq], axis=-1).reshape(depth, Q_LORA, H_A * 128)
    ukv = w_ukv.reshape(depth, KV_LORA, H_A, NOPE + DV)
    wkn = ukv[..., :NOPE].reshape(depth, KV_LORA, H_A * NOPE)
    wvt = ukv[..., NOPE:].reshape(depth, KV_LORA, H_A * DV).transpose(0, 2, 1)
    return wmain, wsmall, tuple(a.astype(BF16) for a in (wqn, wqr, wqp, wkn, wvt))


def kernel(x, c, ctx, c_ctx, w_ada, b_ada, g_mix, g_ffn, w_in, g_cq, w_uq, g_ckv, w_ukv, conv_m, w_qkv_m, b_gate_m, g_h_m, s5_a_re, s5_a_im, s5_log_dt, s5_b_re, s5_b_im, s5_c_re, s5_c_im, s5_d, w_glu, w_br_a, w_br_m, w_br_s, w_out, w_ffn_in, w_ffn_out, g_final):
    depth = w_ada.shape[0]
    xs = jnp.concatenate([x.reshape(NL, D), ctx.reshape(NCX, D)], axis=0)
    cvec = jnp.concatenate([c, c_ctx[None], jnp.zeros((8 - NB_ - 1, D), F32)], axis=0)
    mod = _ada(cvec, w_ada, b_ada)
    blocks_per_batch = T // TM
    cosk, sink = _rope_tables()
    wmain, wsmall, mla_w = _prep_weights(w_in, w_uq, w_ukv)
    toep, wst, wout, apow, dflat = (a.reshape((depth * NLB,) + a.shape[2:]) for a in jax.vmap(_s5_params)(
        s5_a_re, s5_a_im, s5_log_dt, s5_b_re, s5_b_im, s5_c_re, s5_c_im, s5_d))
    w_ffn_in_b = w_ffn_in.astype(BF16)
    w_qkv_b, w_glu_b = w_qkv_m.astype(BF16), w_glu.astype(BF16)
    w_br_a_b, w_br_m_b, w_br_s_b = w_br_a.astype(BF16), w_br_m.astype(BF16), w_br_s.astype(BF16)
    gain = lambda g: g.reshape(depth, 1, g.shape[-1])
    g_mix, g_ffn, g_cq, g_ckv, g_h_m = gain(g_mix), gain(g_ffn), gain(g_cq), gain(g_ckv), gain(g_h_m)

    for l in range(depth):
        modrb = jnp.concatenate(
            [jnp.broadcast_to(mod[l, b:b + 1], (blocks_per_batch, 6 * D)) for b in range(NB_)]
            + [jnp.broadcast_to(mod[l, NB_:NB_ + 1], (NCX // TM, 6 * D))], axis=0).reshape(NBLK, 1, 6 * D)
        p = _inproj(xs, g_mix, modrb, wmain, l, 1024, BF16, "inproj")
        psm = _inproj(xs, g_mix, modrb, wsmall, l, SM_W, F32, "inproj_small")

        q, k, vt = _mla_proj(p, psm, cosk, sink, g_cq, g_ckv, *mla_w, l)
        a = _attention(q, k, vt)

        qm, km, vm = _mprep(p, conv_m, w_qkv_b, l)
        gm = psm[:, SM_GM:SM_GM + 4 * H_M]
        gd = jnp.stack([gm[:, :2 * H_M], gm[:, 2 * H_M:]], axis=0)
        bd = b_gate_m[l].reshape(2, 1, 2 * H_M)
        hdir = _mlstm(qm, km, vm, gd, gd.transpose(0, 2, 1), bd, bd.transpose(0, 2, 1))

        st = _s5_state(psm, wst, apow, l)
        y = _s5_out(psm, toep, st, wout, dflat, l)
        s = _glu(y, w_glu_b, l)

        t = _merge(a, hdir, p, g_h_m, s, w_br_a_b, w_br_m_b, w_br_s_b, l)
        xs = _resid(t, w_out, xs, modrb, 2, 1024, l, "resid_mix")
        u = _ffn_in(xs, g_ffn, modrb, w_ffn_in_b, l)
        xs = _resid(u, w_ffn_out, xs, modrb, 5, 512, l, "resid_ffn")

    return _final_norm(xs, g_final).reshape(NB_, T, D)
```

```python
import math

import numpy as np
import jax
import jax.numpy as jnp
from jax import lax
from jax.experimental import pallas as pl
from jax.experimental.pallas import tpu as pltpu

F32 = jnp.float32
BF16 = jnp.bfloat16

D = 2048
NB_ = 2
T = 4096
TC = 256
GRID_W = 64
EPS = 1e-6
H_A, Q_LORA, KV_LORA, NOPE, ROPE, DV = 8, 512, 512, 128, 64, 128
ROPE_THETA = 10000.0
ATTN_SCALE = (NOPE + ROPE) ** -0.5
H_M, DH_M = 8, 128
D_M = H_M * DH_M
D_S, GROUP, P_S = 1024, 16, 64
G_S = D_S // GROUP
D_FF = ((8 * D // 3 + 255) // 256) * 256
OFF = {}
_o = 0
for _n, _w in (('cq', Q_LORA), ('ckv', KV_LORA), ('krope', ROPE), ('xm', D_M), ('om', D_M),
               ('gm', 4 * H_M), ('us', D_S), ('gates', 3 * D)):
    OFF[_n] = (_o, _w)
    _o += _w

NL = NB_ * T
NCX = NB_ * TC
R = NL + NCX
TM = 512
NBLK = R // TM
LC = 256
TQ = 256
TK = 512
SL = 8
NCH = R // SL
NSEG = 8
SLB = 128 // GROUP
NLB = D_S // 128
SW = SLB * P_S
LOG2E = math.log2(math.e)
PM_CQ, PM_CKV, PM_XM, PM_OM, PM_GATES, PM_W = 0, 512, 1024, 2048, 3072, 3072 + 3 * D
SM_KR, SM_KP, SM_GM, SM_W = 0, 128, 256, 384
VMEM_LIMIT = 56 * 1024 * 1024
SEQ_STARTS = tuple(b * T for b in range(NB_)) + tuple(NL + b * TC for b in range(NB_))
SEQ_ENDS = tuple(b * T + T - 1 for b in range(NB_)) + tuple(NL + b * TC + TC - 1 for b in range(NB_))


def _cp(sem):
    return pltpu.CompilerParams(dimension_semantics=sem, vmem_limit_bytes=VMEM_LIMIT)


def _dot(a, b):
    return jnp.dot(a, b, preferred_element_type=F32)


def _dot_nt(a, b):
    return lax.dot_general(a, b, (((1,), (1,)), ((), ())), preferred_element_type=F32)


def _dot_tn(a, b):
    return lax.dot_general(a, b, (((0,), (0,)), ((), ())), preferred_element_type=F32)


def _sigmoid(x):
    return 1.0 / (1.0 + jnp.exp(-x))


def _log_sigmoid(x):
    return jnp.minimum(x, 0.0) - jnp.log(1.0 + jnp.exp(-jnp.abs(x)))


def _rms(x, g):
    xf = x.astype(F32)
    return xf * lax.rsqrt(jnp.mean(xf * xf, axis=-1, keepdims=True) + EPS) * g


def _ada_kernel(c_ref, w_ref, b_ref, o_ref):
    c = c_ref[...]
    s = (c * _sigmoid(c)).astype(BF16)
    o_ref[0] = _dot(s, w_ref[0].astype(BF16)) + b_ref[0]


def _ada(cvec, w_ada, b_ada):
    depth = w_ada.shape[0]
    tn = 1024
    return pl.pallas_call(
        _ada_kernel,
        grid=(depth, 6 * D // tn),
        in_specs=[pl.BlockSpec((8, D), lambda l, j: (0, 0)),
                  pl.BlockSpec((1, D, tn), lambda l, j: (l, 0, j)),
                  pl.BlockSpec((1, 1, tn), lambda l, j: (l, 0, j))],
        out_specs=pl.BlockSpec((1, 8, tn), lambda l, j: (l, 0, j)),
        out_shape=jax.ShapeDtypeStruct((depth, 8, 6 * D), F32),
        compiler_params=_cp(("arbitrary", "arbitrary")),
        name="ada",
    )(cvec, w_ada, b_ada.reshape(depth, 1, 6 * D))


def _modulate(x_ref, g_ref, sh_ref, sc_ref):
    return _rms(x_ref[...], g_ref[0]) * (1.0 + sc_ref[0]) + sh_ref[0]


def _inproj_kernel(x_ref, g_ref, sh_ref, sc_ref, w_ref, o_ref):
    h = _modulate(x_ref, g_ref, sh_ref, sc_ref).astype(BF16)
    o_ref[...] = _dot(h, w_ref[0]).astype(o_ref.dtype)


def _inproj(x, g, modrb, w, l, tn, out_dtype, name):
    n = w.shape[2]
    return pl.pallas_call(
        _inproj_kernel,
        grid=(n // tn, NBLK),
        in_specs=[pl.BlockSpec((TM, D), lambda j, i: (i, 0)),
                  pl.BlockSpec((1, 1, D), lambda j, i: (l, 0, 0)),
                  pl.BlockSpec((1, 1, D), lambda j, i: (i, 0, 0)),
                  pl.BlockSpec((1, 1, D), lambda j, i: (i, 0, 1)),
                  pl.BlockSpec((1, D, tn), lambda j, i: (l, 0, j))],
        out_specs=pl.BlockSpec((TM, tn), lambda j, i: (i, j)),
        out_shape=jax.ShapeDtypeStruct((R, n), out_dtype),
        compiler_params=_cp(("arbitrary", "arbitrary")),
        name=name,
    )(x, g, modrb, modrb, w)


def _inproj_side_kernel(x_ref, g_ref, sh_ref, sc_ref, w_ref, us_ref, o_ref):
    h = _modulate(x_ref, g_ref, sh_ref, sc_ref).astype(BF16)
    res = _dot(h, w_ref[0])
    for cb in range(NLB):
        us_ref[cb] = res[:, cb * 128:(cb + 1) * 128]
    o_ref[...] = res[:, D_S:]


def _inproj_side(x, g, modrb, w, l):
    n = D_S + SM_W
    return pl.pallas_call(
        _inproj_side_kernel,
        grid=(NBLK,),
        in_specs=[pl.BlockSpec((TM, D), lambda i: (i, 0)),
                  pl.BlockSpec((1, 1, D), lambda i: (l, 0, 0)),
                  pl.BlockSpec((1, 1, D), lambda i: (i, 0, 0)),
                  pl.BlockSpec((1, 1, D), lambda i: (i, 0, 1)),
                  pl.BlockSpec((1, D, n), lambda i: (l, 0, 0))],
        out_specs=[pl.BlockSpec((NLB, TM, 128), lambda i: (0, i, 0)),
                   pl.BlockSpec((TM, SM_W), lambda i: (i, 0))],
        out_shape=[jax.ShapeDtypeStruct((NLB, R, 128), F32),
                   jax.ShapeDtypeStruct((R, SM_W), F32)],
        compiler_params=_cp(("arbitrary",)),
        name="inproj_side",
    )(x, g, modrb, modrb, w)


def _mla_proj_kernel(cq_ref, ckv_ref, kr_ref, kp_ref, cos_ref, sin_ref, gq_ref, gkv_ref,
                     wqn_ref, wqr_ref, wqp_ref, wkn_ref, wvt_ref, q_ref, k_ref, vt_ref):
    hq = _rms(cq_ref[...], gq_ref[0]).astype(BF16)
    hk = _rms(ckv_ref[...], gkv_ref[0]).astype(BF16)
    cos = cos_ref[...]
    sin = sin_ref[...]
    qscale = ATTN_SCALE * LOG2E
    qn = _dot(hq, wqn_ref[0]) * qscale
    qr = _dot(hq, wqr_ref[0])
    qp = _dot(hq, wqp_ref[0])
    kn = _dot(hk, wkn_ref[0])
    vt = _dot_nt(wvt_ref[0], hk)
    kr = (kr_ref[...] * cos + kp_ref[...] * sin).astype(BF16)
    ones = jnp.ones((DV, TM), BF16)
    for h in range(H_A):
        lo = slice(h * 256, h * 256 + 128)
        hi = slice(h * 256 + 128, (h + 1) * 256)
        sl = slice(h * 128, (h + 1) * 128)
        q_ref[:, lo] = qn[:, sl].astype(BF16)
        q_ref[:, hi] = ((qr[:, sl] * cos + qp[:, sl] * sin) * qscale).astype(BF16)
        k_ref[:, lo] = kn[:, sl].astype(BF16)
        k_ref[:, hi] = kr
        vt_ref[lo, :] = vt[sl, :].astype(BF16)
        vt_ref[hi, :] = ones


def _mla_proj(p, psm, cosk, sink, gq, gkv, wqn, wqr, wqp, wkn, wvt, l):
    lw = lambda shp: pl.BlockSpec((1,) + shp, lambda i: (l, 0, 0))
    return pl.pallas_call(
        _mla_proj_kernel,
        grid=(NBLK,),
        in_specs=[pl.BlockSpec((TM, 512), lambda i: (i, PM_CQ // 512)),
                  pl.BlockSpec((TM, 512), lambda i: (i, PM_CKV // 512)),
                  pl.BlockSpec((TM, 128), lambda i: (i, SM_KR // 128)),
                  pl.BlockSpec((TM, 128), lambda i: (i, SM_KP // 128)),
                  pl.BlockSpec((TM, 128), lambda i: (i, 0)),
                  pl.BlockSpec((TM, 128), lambda i: (i, 0)),
                  pl.BlockSpec((1, 1, 512), lambda i: (l, 0, 0)),
                  pl.BlockSpec((1, 1, 512), lambda i: (l, 0, 0)),
                  lw((512, 1024)), lw((512, 1024)), lw((512, 1024)), lw((512, 1024)), lw((1024, 512))],
        out_specs=[pl.BlockSpec((TM, 2048), lambda i: (i, 0)),
                   pl.BlockSpec((TM, 2048), lambda i: (i, 0)),
                   pl.BlockSpec((2048, TM), lambda i: (0, i))],
        out_shape=[jax.ShapeDtypeStruct((R, 2048), BF16),
                   jax.ShapeDtypeStruct((R, 2048), BF16),
                   jax.ShapeDtypeStruct((2048, R), BF16)],
        compiler_params=_cp(("arbitrary",)),
        name="mla_proj",
    )(p, p, psm, psm, cosk, sink, gq, gkv, wqn, wqr, wqp, wkn, wvt)


def _attn_kernel(q_ref, kl_ref, kc_ref, vl_ref, vc_ref, o_ref, m_ref, acc_ref):
    qi = pl.program_id(2)
    q = q_ref[...]

    def chunk(k, vt, m, acc):
        st = _dot_nt(k, q)
        m_new = jnp.maximum(m, jnp.max(st, axis=0, keepdims=True))
        p = jnp.exp2(st - m_new).astype(BF16)
        return m_new, jnp.exp2(m - m_new) * acc + _dot(vt, p)

    m0 = jnp.full((1, TQ), -jnp.inf, F32)
    a0 = jnp.zeros((2 * DV, TQ), F32)
    m, acc = chunk(kc_ref[...], vc_ref[...], m0, a0)
    m_ref[...] = m
    acc_ref[...] = acc

    @pl.when(qi < T // TQ)
    def _():
        m = m_ref[...]
        acc = acc_ref[...]
        for j in range(T // TK):
            m, acc = chunk(kl_ref[j * TK:(j + 1) * TK, :], vl_ref[:, j * TK:(j + 1) * TK], m, acc)
        acc_ref[...] = acc

    acc = acc_ref[...]
    o_ref[...] = jnp.transpose(acc[0:DV, :] / acc[DV:2 * DV, :]).astype(o_ref.dtype)


def _attention(q, k, vt):
    nq = T // TQ
    qrow = lambda b, h, i: jnp.where(i < nq, b * nq + i, NL // TQ + b)
    return pl.pallas_call(
        _attn_kernel,
        grid=(NB_, H_A, nq + 1),
        in_specs=[pl.BlockSpec((TQ, 256), lambda b, h, i: (qrow(b, h, i), h)),
                  pl.BlockSpec((T, 256), lambda b, h, i: (b, h)),
                  pl.BlockSpec((TC, 256), lambda b, h, i: (NL // TC + b, h)),
                  pl.BlockSpec((256, T), lambda b, h, i: (h, b)),
                  pl.BlockSpec((256, TC), lambda b, h, i: (h, NL // TC + b))],
        out_specs=pl.BlockSpec((TQ, DV), lambda b, h, i: (qrow(b, h, i), h)),
        out_shape=jax.ShapeDtypeStruct((R, H_A * DV), BF16),
        scratch_shapes=[pltpu.VMEM((1, TQ), F32), pltpu.VMEM((2 * DV, TQ), F32)],
        compiler_params=_cp(("arbitrary", "arbitrary", "arbitrary")),
        name="attention",
    )(q, k, k, vt, vt)


def _row_in(grow, rows):
    hit = grow == rows[0]
    for r in rows[1:]:
        hit = jnp.logical_or(hit, grow == r)
    return hit


def _mprep_kernel(x_ref, xp_ref, xn_ref, cw_ref, wq_ref, q_ref, k_ref, v_ref):
    i = pl.program_id(0)
    xb = x_ref[...]
    x = xb.astype(F32)
    prev_row = xp_ref[15:16, :].astype(F32)
    next_row = xn_ref[0:1, :].astype(F32)
    row = lax.broadcasted_iota(jnp.int32, (TM, 1), 0)
    grow = row + i * TM
    xprev = jnp.where(row == 0, prev_row, pltpu.roll(x, 1, axis=0))
    xprev = jnp.where(_row_in(grow, SEQ_STARTS), 0.0, xprev)
    xnext = jnp.where(row == TM - 1, next_row, pltpu.roll(x, TM - 1, axis=0))
    xnext = jnp.where(_row_in(grow, SEQ_ENDS), 0.0, xnext)
    cw = cw_ref[0]
    xc = xprev * cw[0:1, :] + x * cw[1:2, :] + xnext * cw[2:3, :]
    xcb = (xc * _sigmoid(xc)).astype(BF16)
    ones = jnp.ones((DH_M, TM), BF16)
    for h in range(H_M):
        sl = slice(h * DH_M, (h + 1) * DH_M)
        q_ref[:, sl] = _dot(xcb[:, sl], wq_ref[0, 0, h]).astype(BF16)
        k_ref[:, sl] = (_dot(xcb[:, sl], wq_ref[0, 1, h]) * DH_M ** -0.5).astype(BF16)
        v_ref[2 * h * DH_M:(2 * h + 1) * DH_M, :] = _dot_nt(wq_ref[0, 2, h], xb[:, sl]).astype(BF16)
        v_ref[(2 * h + 1) * DH_M:(2 * h + 2) * DH_M, :] = ones


def _mprep(p, conv_w, w_qkv, l):
    hb = TM // 16
    xcol = PM_XM // D_M
    return pl.pallas_call(
        _mprep_kernel,
        grid=(NBLK,),
        in_specs=[pl.BlockSpec((TM, D_M), lambda i: (i, xcol)),
                  pl.BlockSpec((16, D_M), lambda i: (jnp.maximum(i * hb - 1, 0), xcol)),
                  pl.BlockSpec((16, D_M), lambda i: (jnp.minimum((i + 1) * hb, R // 16 - 1), xcol)),
                  pl.BlockSpec((1, 3, D_M), lambda i: (l, 0, 0)),
                  pl.BlockSpec((1, 3, H_M, DH_M, DH_M), lambda i: (l, 0, 0, 0, 0))],
        out_specs=[pl.BlockSpec((TM, D_M), lambda i: (i, 0)),
                   pl.BlockSpec((TM, D_M), lambda i: (i, 0)),
                   pl.BlockSpec((2 * D_M, TM), lambda i: (0, i))],
        out_shape=[jax.ShapeDtypeStruct((R, D_M), BF16),
                   jax.ShapeDtypeStruct((R, D_M), BF16),
                   jax.ShapeDtypeStruct((2 * D_M, R), BF16)],
        compiler_params=_cp(("arbitrary",)),
        name="mlstm_prep",
    )(p, p, p, conv_w, w_qkv)


def _split3(x):
    h = x.astype(BF16)
    r = x - h.astype(F32)
    m = r.astype(BF16)
    l = (r - m.astype(F32)).astype(BF16)
    return h, m, l


def _mlstm_kernel(q_ref, k_ref, v_ref, g_ref, gt_ref, b_ref, bt_ref, h_ref, c_ref, m_ref):
    d = pl.program_id(1)
    s = pl.program_id(2)

    @pl.when(s == 0)
    def _():
        c_ref[...] = jnp.zeros_like(c_ref)
        m_ref[...] = jnp.zeros_like(m_ref)

    sign = 1 - 2 * d
    r = lax.broadcasted_iota(jnp.int32, (LC, LC), 0)
    c = lax.broadcasted_iota(jnp.int32, (LC, LC), 1)
    before = (c - r) * sign >= 0
    tri = jnp.where((r - c) * sign >= 0, 1.0, 0.0).astype(BF16)
    g = g_ref[0] + b_ref[0]
    gt = gt_ref[0] + bt_ref[0]
    ig_c = g[:, 0:H_M]
    lf_c = _log_sigmoid(g[:, H_M:2 * H_M])
    lf_r = _log_sigmoid(gt[H_M:2 * H_M, :])
    b_c = sum(_dot(tri, part) for part in _split3(lf_c))
    b_r = sum(_dot_nt(part, tri) for part in _split3(lf_r))
    tot = jnp.sum(lf_c, axis=0, keepdims=True)
    src = ig_c - b_c

    for h in range(H_M):
        sl = slice(h * DH_M, (h + 1) * DH_M)
        qh = q_ref[:, sl]
        kh = k_ref[:, sl]
        vth = v_ref[2 * h * DH_M:(2 * h + 2) * DH_M, :]
        bt = b_r[h:h + 1, :]
        tt = tot[:, h:h + 1]
        mp = m_ref[h:h + 1, 0:1]
        dm = jnp.where(before, bt + src[:, h:h + 1], -jnp.inf)
        m_inter = bt + mp
        m_t = jnp.maximum(jnp.max(dm, axis=0, keepdims=True), m_inter)
        sc = _dot_nt(kh, qh) * jnp.exp(dm - m_t)
        inter = jnp.exp(m_inter - m_t)
        st = c_ref[h]
        both = _dot(vth, sc.astype(BF16)) + inter * _dot_nt(st.astype(BF16), qh)
        den = both[DH_M:DH_M + 1, :]
        ht = both[0:DH_M, :] / jnp.maximum(jnp.abs(den), jnp.exp(-m_t))
        h_ref[0, :, sl] = jnp.transpose(ht)
        gg = tt + src[:, h:h + 1]
        m_new = jnp.maximum(tt + mp, jnp.max(gg, axis=0, keepdims=True))
        kw = (kh.astype(F32) * jnp.exp(gg - m_new)).astype(BF16)
        c_ref[h] = jnp.exp(tt + mp - m_new) * st + _dot(vth, kw)
        m_ref[h:h + 1, :] = jnp.broadcast_to(m_new, (1, DH_M))


def _mlstm(q, k, v, gd, gtd, bd, btd):
    nl = T // LC
    nsteps = nl + TC // LC

    def rb(b, d, s):
        lat = b * nl + jnp.where(d == 0, s - 1, nl - s)
        return jnp.where(s == 0, NL // LC + b, lat)

    qspec = pl.BlockSpec((LC, D_M), lambda b, d, s: (rb(b, d, s), 0))
    return pl.pallas_call(
        _mlstm_kernel,
        grid=(NB_, 2, nsteps),
        in_specs=[qspec, qspec,
                  pl.BlockSpec((2 * D_M, LC), lambda b, d, s: (0, rb(b, d, s))),
                  pl.BlockSpec((1, LC, 16), lambda b, d, s: (d, rb(b, d, s), 0)),
                  pl.BlockSpec((1, 16, LC), lambda b, d, s: (d, 0, rb(b, d, s))),
                  pl.BlockSpec((1, 1, 16), lambda b, d, s: (d, 0, 0)),
                  pl.BlockSpec((1, 16, 1), lambda b, d, s: (d, 0, 0))],
        out_specs=pl.BlockSpec((1, LC, D_M), lambda b, d, s: (d, rb(b, d, s), 0)),
        out_shape=jax.ShapeDtypeStruct((2, R, D_M), F32),
        scratch_shapes=[pltpu.VMEM((H_M, 2 * DH_M, DH_M), F32),
                        pltpu.VMEM((H_M, DH_M), F32)],
        compiler_params=_cp(("arbitrary", "arbitrary", "arbitrary")),
        name="mlstm_scan",
    )(q, k, v, gd, gtd, bd, btd)


S5_NLAT = T // SL
S5_NCTX = TC // SL
S5_SEG = S5_NLAT // NSEG


def _fold_copy(tok_ref, fold_ref, to_fold):
    def move(tok_idx, fold_rows, t):
        lanes = slice(t * 128, (t + 1) * 128)
        if to_fold:
            fold_ref[fold_rows, lanes] = tok_ref[tok_idx, :]
        else:
            tok_ref[tok_idx, :] = fold_ref[fold_rows, lanes]

    def body(j, carry):
        for b in range(NB_):
            rows = pl.ds(pl.multiple_of(b * S5_NLAT + j * NSEG, NSEG), NSEG)
            for t in range(SL):
                move(pl.ds(b * T + j * SL + t, NSEG, stride=S5_SEG * SL), rows, t)
        return carry

    lax.fori_loop(0, S5_SEG, body, 0)
    for t in range(SL):
        move(pl.ds(NL + t, NCX // SL, stride=SL), slice(NL // SL, NCH), t)


def _expand_block_diag(compact_ref, dense_ref, key_shift, key_mul, key_mask, row_gshift, col_gshift):
    comp = compact_ref[0]
    nrows, kc = comp.shape
    chunk = 512
    r = lax.broadcasted_iota(jnp.int32, (kc, chunk), 0)
    rg = (lax.broadcasted_iota(jnp.int32, (nrows, 1), 0) >> row_gshift) & (SLB - 1)
    for c0 in range(0, dense_ref.shape[1], chunk):
        q = lax.broadcasted_iota(jnp.int32, (kc, chunk), 1) + c0
        sel = jnp.where(r == (q >> key_shift) * key_mul + (q & key_mask), 1.0, 0.0).astype(BF16)
        cg = ((lax.broadcasted_iota(jnp.int32, (1, chunk), 1) + c0) >> col_gshift) & (SLB - 1)
        dense_ref[:, c0:c0 + chunk] = jnp.where(rg == cg, _dot(comp, sel), 0.0).astype(BF16)


def _cmul(ar, ai, sr, si):
    return ar * sr - ai * si, ar * si + ai * sr


def _s5_state_kernel(us_ref, wst_ref, a_ref, s_ref, e_ref, x_ref, w_ref):
    _fold_copy(us_ref.at[0], x_ref, True)
    _expand_block_diag(wst_ref, w_ref, 9, P_S, P_S - 1, 4, 6)
    e_ref[...] = _dot(x_ref[...].astype(BF16), w_ref[...])
    apow = a_ref[0]
    step_a = ((apow[0:1, :], apow[1:2, :]), (apow[2:3, :], apow[3:4, :]))
    nctx, seg = S5_NCTX, S5_SEG

    def advance(rows, d, st, write):
        cr = slice(2 * d * SW, (2 * d + 1) * SW)
        ci = slice((2 * d + 1) * SW, (2 * d + 2) * SW)
        dr = e_ref[rows, cr]
        di = e_ref[rows, ci]
        if write:
            e_ref[rows, cr] = st[0]
            e_ref[rows, ci] = st[1]
        nr, ni = _cmul(step_a[d][0], step_a[d][1], st[0], st[1])
        return nr + dr, ni + di

    def sweep(nsteps, rows_of, init, write):
        def body(kk, sts):
            out = []
            for b in range(NB_):
                out.append(advance(rows_of(b, kk), 0, sts[2 * b], write))
                out.append(advance(rows_of(b, nsteps - 1 - kk), 1, sts[2 * b + 1], write))
            return tuple(out)
        return lax.fori_loop(0, nsteps, body, init)

    z1 = jnp.zeros((1, SW), F32)
    carry = sweep(nctx, lambda b, kk: pl.ds(NL // SL + b * nctx + kk, 1), ((z1, z1),) * (2 * NB_), True)

    seg_rows = lambda b, kk: pl.ds(pl.multiple_of(b * S5_NLAT + kk * NSEG, NSEG), NSEG)
    z8 = jnp.zeros((NSEG, SW), F32)
    ends = sweep(seg, seg_rows, ((z8, z8),) * (2 * NB_), False)
    inits = []
    for ch in range(2 * NB_):
        d = ch % 2
        ar, ai = step_a[d]
        for _ in range(int(math.log2(seg))):
            ar, ai = ar * ar - ai * ai, 2.0 * ar * ai
        cur = carry[ch]
        rows_r, rows_i = [None] * NSEG, [None] * NSEG
        for kseg in (range(NSEG) if d == 0 else range(NSEG - 1, -1, -1)):
            rows_r[kseg], rows_i[kseg] = cur
            nr, ni = _cmul(ar, ai, cur[0], cur[1])
            cur = (nr + ends[ch][0][kseg:kseg + 1, :], ni + ends[ch][1][kseg:kseg + 1, :])
        inits.append((jnp.concatenate(rows_r, axis=0), jnp.concatenate(rows_i, axis=0)))
    sweep(seg, seg_rows, tuple(inits), True)
    s_ref[0] = e_ref[...].astype(BF16)


def _s5_state(us, wst, apow, l):
    return pl.pallas_call(
        _s5_state_kernel,
        grid=(NLB,),
        in_specs=[pl.BlockSpec((1, R, 128), lambda g: (g, 0, 0), pipeline_mode=pl.Buffered(1)),
                  pl.BlockSpec((1, SL * 128, 4 * P_S), lambda g: (l * NLB + g, 0, 0)),
                  pl.BlockSpec((1, 8, SW), lambda g: (l * NLB + g, 0, 0))],
        out_specs=pl.BlockSpec((1, NCH, 4 * SW), lambda g: (g, 0, 0)),
        out_shape=jax.ShapeDtypeStruct((NLB, NCH, 4 * SW), BF16),
        scratch_shapes=[pltpu.VMEM((NCH, 4 * SW), F32),
                        pltpu.VMEM((NCH, SL * 128), F32),
                        pltpu.VMEM((SL * 128, 4 * SW), BF16)],
        compiler_params=_cp(("arbitrary",)),
        name="s5_state",
    )(us, wst, apow)


def _s5_out_kernel(us_ref, toep_ref, s_ref, wo_ref, d_ref, y_ref, x_ref, tw_ref, ow_ref):
    _fold_copy(us_ref.at[0], x_ref, True)
    _expand_block_diag(toep_ref, tw_ref, 7, GROUP, GROUP - 1, 4, 4)
    _expand_block_diag(wo_ref, ow_ref, 7, GROUP, GROUP - 1, 6, 4)
    x = x_ref[...]
    y = _dot(x.astype(BF16), tw_ref[...]) + _dot(s_ref[0], ow_ref[...]) + d_ref[0] * x
    x_ref[...] = jax.nn.gelu(y)
    _fold_copy(y_ref.at[0], x_ref, False)


def _s5_out(us, toep, s, wout, dflat, l):
    return pl.pallas_call(
        _s5_out_kernel,
        grid=(NLB,),
        in_specs=[pl.BlockSpec((1, R, 128), lambda g: (g, 0, 0), pipeline_mode=pl.Buffered(1)),
                  pl.BlockSpec((1, SL * 128, SL * GROUP), lambda g: (l * NLB + g, 0, 0)),
                  pl.BlockSpec((1, NCH, 4 * SW), lambda g: (g, 0, 0), pipeline_mode=pl.Buffered(1)),
                  pl.BlockSpec((1, 4 * SW, SL * GROUP), lambda g: (l * NLB + g, 0, 0)),
                  pl.BlockSpec((1, 1, SL * 128), lambda g: (l * NLB + g, 0, 0))],
        out_specs=pl.BlockSpec((1, R, 128), lambda g: (g, 0, 0)),
        out_shape=jax.ShapeDtypeStruct((NLB, R, 128), F32),
        scratch_shapes=[pltpu.VMEM((NCH, SL * 128), F32),
                        pltpu.VMEM((SL * 128, SL * 128), BF16),
                        pltpu.VMEM((4 * SW, SL * 128), BF16)],
        compiler_params=_cp(("arbitrary",)),
        name="s5_out",
    )(us, toep, s, wout, dflat)


def _glu_kernel(y_ref, w_ref, o_ref):
    y = jnp.concatenate([y_ref[cb] for cb in range(NLB)], axis=1).astype(BF16)
    z = _dot(y, w_ref[0])
    o_ref[...] = (z[:, :D_S] * _sigmoid(z[:, D_S:])).astype(BF16)


def _glu(y, w, l):
    return pl.pallas_call(
        _glu_kernel,
        grid=(NBLK,),
        in_specs=[pl.BlockSpec((NLB, TM, 128), lambda i: (0, i, 0)),
                  pl.BlockSpec((1, D_S, 2 * D_S), lambda i: (l, 0, 0))],
        out_specs=pl.BlockSpec((TM, D_S), lambda i: (i, 0)),
        out_shape=jax.ShapeDtypeStruct((R, D_S), BF16),
        compiler_params=_cp(("arbitrary",)),
        name="s5_glu",
    )(y, w)


def _s5_params(a_re, a_im, log_dt, b_re, b_im, c_re, c_im, dskip):
    dt = jnp.exp(log_dt)[:, :, None]
    lam_r, lam_i = a_re * dt, a_im * dt
    mag = jnp.exp(lam_r)
    ar, ai = mag * jnp.cos(lam_i), mag * jnp.sin(lam_i)
    den = a_re * a_re + a_im * a_im
    nr, ni = ar - 1.0, ai
    cr = (nr * a_re + ni * a_im) / den
    ci = (ni * a_re - nr * a_im) / den
    cpr = c_re[None] * cr[:, :, None, :] - c_im[None] * ci[:, :, None, :]
    cpi = c_re[None] * ci[:, :, None, :] + c_im[None] * cr[:, :, None, :]
    j = jnp.arange(SL + 1, dtype=F32)[:, None, None, None]
    pm = jnp.exp(lam_r[None] * j)
    pr, pi = pm * jnp.cos(lam_i[None] * j), pm * jnp.sin(lam_i[None] * j)
    abr = pr[..., None] * b_re[None, None] - pi[..., None] * b_im[None, None]
    abi = pr[..., None] * b_im[None, None] + pi[..., None] * b_re[None, None]
    kern = (jnp.einsum('dgcp,jdgpe->jdgce', cpr, abr[:SL]) - jnp.einsum('dgcp,jdgpe->jdgce', cpi, abi[:SL]))
    zero = jnp.zeros_like(kern[0, 0])
    rows = []
    for t_in in range(SL):
        blks = []
        for t_out in range(SL):
            blk = zero
            if t_out >= t_in:
                blk = blk + kern[t_out - t_in, 0]
            if t_in >= t_out:
                blk = blk + kern[t_in - t_out, 1]
            blks.append(blk)
        rows.append(jnp.stack(blks, axis=0))
    t5 = jnp.stack(rows, axis=0).reshape(SL, SL, NLB, SLB, GROUP, GROUP)
    toep = t5.transpose(2, 0, 3, 5, 1, 4).reshape(NLB, SL * 128, SL * GROUP)
    rev = lambda a, lo: jnp.stack([a[lo + SL - 1 - t] for t in range(SL)], axis=0)
    ws = jnp.stack([rev(abr[:, 0], 0), rev(abi[:, 0], 0), abr[:SL, 1], abi[:SL, 1]], axis=0)
    ws = ws.reshape(4, SL, NLB, SLB, P_S, GROUP)
    wst = ws.transpose(2, 1, 3, 5, 0, 4).reshape(NLB, SL * 128, 4 * P_S)
    def readout(pw_r, pw_i, d):
        re = cpr[d][None] * pw_r[:, :, None, :] - cpi[d][None] * pw_i[:, :, None, :]
        im = cpr[d][None] * pw_i[:, :, None, :] + cpi[d][None] * pw_r[:, :, None, :]
        return re, -im
    of_re, of_im = readout(pr[1:SL + 1, 0], pi[1:SL + 1, 0], 0)
    ob_re, ob_im = readout(rev(pr[:, 1], 1), rev(pi[:, 1], 1), 1)
    wo = jnp.stack([of_re, of_im, ob_re, ob_im], axis=0).reshape(4, SL, NLB, SLB, GROUP, P_S)
    wout = wo.transpose(2, 0, 3, 5, 1, 4).reshape(NLB, 4 * SW, SL * GROUP)
    blk = lambda a: a.reshape(NLB, 1, SW)
    apow = jnp.concatenate([blk(pr[SL, 0]), blk(pi[SL, 0]), blk(pr[SL, 1]), blk(pi[SL, 1]),
                            jnp.zeros((NLB, 4, SW), F32)], axis=1)
    dflat = jnp.tile(dskip.reshape(NLB, 1, 128), (1, 1, SL))
    return toep.astype(BF16), wst.astype(BF16), wout.astype(BF16), apow, dflat


def _merge_kernel(a_ref, hf_ref, hb_ref, om_ref, gh_ref, s_ref, ga_ref, gm_ref, gs_ref,
                  wa_ref, wm_ref, ws_ref, o_ref):
    hsum = hf_ref[0] + hb_ref[0]
    parts = []
    for h in range(H_M):
        xh = hsum[:, h * DH_M:(h + 1) * DH_M]
        parts.append(xh * lax.rsqrt(jnp.mean(xh * xh, axis=-1, keepdims=True) + EPS))
    hn = jnp.concatenate(parts, axis=1) * gh_ref[0]
    m = (_sigmoid(om_ref[...].astype(F32)) * hn).astype(BF16)
    t = (_sigmoid(ga_ref[...].astype(F32)) * _dot(a_ref[...], wa_ref[0])
         + _sigmoid(gm_ref[...].astype(F32)) * _dot(m, wm_ref[0])
         + _sigmoid(gs_ref[...].astype(F32)) * _dot(s_ref[...], ws_ref[0]))
    o_ref[...] = t.astype(BF16)


def _merge(a, hdir, p, gh, s, wa, wm, ws, l):
    tn = 1024
    nj = D // tn
    gcol = PM_GATES // tn
    row = lambda shp, c: pl.BlockSpec(shp, lambda j, i: (i, c))
    wspec = pl.BlockSpec((1, 1024, tn), lambda j, i: (l, 0, j))
    return pl.pallas_call(
        _merge_kernel,
        grid=(nj, NBLK),
        in_specs=[row((TM, 1024), 0),
                  pl.BlockSpec((1, TM, D_M), lambda j, i: (0, i, 0)),
                  pl.BlockSpec((1, TM, D_M), lambda j, i: (1, i, 0)),
                  row((TM, 1024), PM_OM // 1024),
                  pl.BlockSpec((1, 1, D_M), lambda j, i: (l, 0, 0)),
                  row((TM, 1024), 0),
                  pl.BlockSpec((TM, tn), lambda j, i: (i, gcol + j)),
                  pl.BlockSpec((TM, tn), lambda j, i: (i, gcol + nj + j)),
                  pl.BlockSpec((TM, tn), lambda j, i: (i, gcol + 2 * nj + j)),
                  wspec, wspec, wspec],
        out_specs=pl.BlockSpec((TM, tn), lambda j, i: (i, j)),
        out_shape=jax.ShapeDtypeStruct((R, D), BF16),
        compiler_params=_cp(("arbitrary", "arbitrary")),
        name="merge",
    )(a, hdir, hdir, p, gh, s, p, p, p, wa, wm, ws)


def _resid_kernel(t_ref, w_ref, x_ref, gt_ref, o_ref, wb_ref):
    @pl.when(pl.program_id(1) == 0)
    def _():
        wb_ref[...] = w_ref[0].astype(BF16)

    o_ref[...] = x_ref[...] + gt_ref[0] * _dot(t_ref[...], wb_ref[...])


def _resid(t, w, x, modrb, gate_chunk, tn, l, name):
    kdim = t.shape[1]
    nj = D // tn
    return pl.pallas_call(
        _resid_kernel,
        grid=(nj, NBLK),
        in_specs=[pl.BlockSpec((TM, kdim), lambda j, i: (i, 0)),
                  pl.BlockSpec((1, kdim, tn), lambda j, i: (l, 0, j)),
                  pl.BlockSpec((TM, tn), lambda j, i: (i, j)),
                  pl.BlockSpec((1, 1, tn), lambda j, i: (i, 0, gate_chunk * nj + j))],
        out_specs=pl.BlockSpec((TM, tn), lambda j, i: (i, j)),
        out_shape=jax.ShapeDtypeStruct((R, D), F32),
        scratch_shapes=[pltpu.VMEM((kdim, tn), BF16)],
        compiler_params=_cp(("arbitrary", "arbitrary")),
        name=name,
    )(t, w, x, modrb)


def _ffn_in_kernel(x_ref, g_ref, sh_ref, sc_ref, wa_ref, wb_ref, o_ref):
    h = _modulate(x_ref, g_ref, sh_ref, sc_ref).astype(BF16)
    a = _dot(h, wa_ref[0])
    b = _dot(h, wb_ref[0])
    o_ref[...] = (a * _sigmoid(a) * b).astype(BF16)


def _ffn_in(x, g, modrb, w, l):
    tn = 1408
    nj = D_FF // tn
    return pl.pallas_call(
        _ffn_in_kernel,
        grid=(nj, NBLK),
        in_specs=[pl.BlockSpec((TM, D), lambda j, i: (i, 0)),
                  pl.BlockSpec((1, 1, D), lambda j, i: (l, 0, 0)),
                  pl.BlockSpec((1, 1, D), lambda j, i: (i, 0, 3)),
                  pl.BlockSpec((1, 1, D), lambda j, i: (i, 0, 4)),
                  pl.BlockSpec((1, D, tn), lambda j, i: (l, 0, j)),
                  pl.BlockSpec((1, D, tn), lambda j, i: (l, 0, nj + j))],
        out_specs=pl.BlockSpec((TM, tn), lambda j, i: (i, j)),
        out_shape=jax.ShapeDtypeStruct((R, D_FF), BF16),
        compiler_params=_cp(("arbitrary", "arbitrary")),
        name="ffn_in",
    )(x, g, modrb, modrb, w, w)


def _final_norm_kernel(x_ref, g_ref, o_ref):
    o_ref[...] = _rms(x_ref[...], g_ref[...])


def _final_norm(x, g):
    return pl.pallas_call(
        _final_norm_kernel,
        grid=(NL // TM,),
        in_specs=[pl.BlockSpec((TM, D), lambda i: (i, 0)),
                  pl.BlockSpec((1, D), lambda i: (0, 0))],
        out_specs=pl.BlockSpec((TM, D), lambda i: (i, 0)),
        out_shape=jax.ShapeDtypeStruct((NL, D), F32),
        compiler_params=_cp(("arbitrary",)),
        name="final_norm",
    )(x, g.reshape(1, D))


def _rope_tables():
    rows = T // GRID_W
    rr, cc = jnp.meshgrid(jnp.arange(rows, dtype=F32), jnp.arange(GRID_W, dtype=F32), indexing='ij')
    rr, cc = rr.reshape(-1), cc.reshape(-1)
    half = ROPE // 2
    inv = 1.0 / (ROPE_THETA ** (jnp.arange(0, half, 2, dtype=F32) / half))
    ang = jnp.stack([rr[:, None] * inv, cc[:, None] * inv], axis=1)
    cos = jnp.cos(ang)
    sin = jnp.sin(ang)
    cos_f = jnp.stack([cos, cos], axis=2).reshape(T, ROPE)
    sin_f = jnp.stack([-sin, sin], axis=2).reshape(T, ROPE)
    pad = lambda a: jnp.concatenate([a, jnp.zeros((a.shape[0], 128 - ROPE), F32)], axis=1)
    cos_l, sin_l = pad(cos_f), pad(sin_f)
    cos_c = pad(jnp.ones((NCX, ROPE), F32))
    sin_c = jnp.zeros((NCX, 128), F32)
    return (jnp.concatenate([cos_l] * NB_ + [cos_c], axis=0),
            jnp.concatenate([sin_l] * NB_ + [sin_c], axis=0))


def _rope_partner(w):
    s = w.shape[:-1]
    w4 = w.reshape(s + (2, 2, ROPE // 4))
    return jnp.concatenate([w4[..., 1:2, :], w4[..., 0:1, :]], axis=-2).reshape(s + (ROPE,))


def _cols(w, name):
    o, n = OFF[name]
    return w[..., o:o + n]


def _prep_weights(w_in, w_uq, w_ukv):
    depth = w_in.shape[0]
    wmain = jnp.concatenate([_cols(w_in, n) for n in ('cq', 'ckv', 'xm', 'om', 'gates')], axis=-1).astype(BF16)
    kr = _cols(w_in, 'krope')
    z64 = jnp.zeros((depth, D, 64), F32)
    wsmall = jnp.concatenate([_cols(w_in, 'us'), kr, z64, _rope_partner(kr), z64, _cols(w_in, 'gm'),
                              jnp.zeros((depth, D, 128 - 4 * H_M), F32)], axis=-1).astype(BF16)
    uq = w_uq.reshape(depth, Q_LORA, H_A, NOPE + ROPE)
    wqn = uq[..., :NOPE].reshape(depth, Q_LORA, H_A * NOPE)
    qr = uq[..., NOPE:]
    zq = jnp.zeros_like(qr)
    wqr = jnp.concatenate([qr, zq], axis=-1).reshape(depth, Q_LORA, H_A * 128)
    wqp = jnp.concatenate([_rope_partner(qr), zq], axis=-1).reshape(depth, Q_LORA, H_A * 128)
    ukv = w_ukv.reshape(depth, KV_LORA, H_A, NOPE + DV)
    wkn = ukv[..., :NOPE].reshape(depth, KV_LORA, H_A * NOPE)
    wvt = ukv[..., NOPE:].reshape(depth, KV_LORA, H_A * DV).transpose(0, 2, 1)
    return wmain, wsmall, tuple(a.astype(BF16) for a in (wqn, wqr, wqp, wkn, wvt))


def kernel(x, c, ctx, c_ctx, w_ada, b_ada, g_mix, g_ffn, w_in, g_cq, w_uq, g_ckv, w_ukv, conv_m, w_qkv_m, b_gate_m, g_h_m, s5_a_re, s5_a_im, s5_log_dt, s5_b_re, s5_b_im, s5_c_re, s5_c_im, s5_d, w_glu, w_br_a, w_br_m, w_br_s, w_out, w_ffn_in, w_ffn_out, g_final):
    depth = w_ada.shape[0]
    xs = jnp.concatenate([x.reshape(NL, D), ctx.reshape(NCX, D)], axis=0)
    cvec = jnp.concatenate([c, c_ctx[None], jnp.zeros((8 - NB_ - 1, D), F32)], axis=0)
    mod = _ada(cvec, w_ada, b_ada)
    blocks_per_batch = T // TM
    cosk, sink = _rope_tables()
    wmain, wsmall, mla_w = _prep_weights(w_in, w_uq, w_ukv)
    toep, wst, wout, apow, dflat = (a.reshape((depth * NLB,) + a.shape[2:]) for a in jax.vmap(_s5_params)(
        s5_a_re, s5_a_im, s5_log_dt, s5_b_re, s5_b_im, s5_c_re, s5_c_im, s5_d))
    w_ffn_in_b = w_ffn_in.astype(BF16)
    w_qkv_b = jnp.concatenate([w_qkv_m[:, :2], jnp.swapaxes(w_qkv_m[:, 2:], -1, -2)], axis=1).astype(BF16)
    w_glu_b = w_glu.astype(BF16)
    w_br_a_b, w_br_m_b, w_br_s_b = w_br_a.astype(BF16), w_br_m.astype(BF16), w_br_s.astype(BF16)
    gain = lambda g: g.reshape(depth, 1, g.shape[-1])
    g_mix, g_ffn, g_cq, g_ckv, g_h_m = gain(g_mix), gain(g_ffn), gain(g_cq), gain(g_ckv), gain(g_h_m)

    for l in range(depth):
        modrb = jnp.concatenate(
            [jnp.broadcast_to(mod[l, b:b + 1], (blocks_per_batch, 6 * D)) for b in range(NB_)]
            + [jnp.broadcast_to(mod[l, NB_:NB_ + 1], (NCX // TM, 6 * D))], axis=0).reshape(NBLK, 1, 6 * D)
        p = _inproj(xs, g_mix, modrb, wmain, l, 1024, BF16, "inproj")
        us, psm = _inproj_side(xs, g_mix, modrb, wsmall, l)

        q, k, vt = _mla_proj(p, psm, cosk, sink, g_cq, g_ckv, *mla_w, l)
        a = _attention(q, k, vt)

        qm, km, vm = _mprep(p, conv_m, w_qkv_b, l)
        gm = psm[:, SM_GM:SM_GM + 4 * H_M]
        gd = jnp.stack([gm[:, :2 * H_M], gm[:, 2 * H_M:]], axis=0)
        bd = b_gate_m[l].reshape(2, 1, 2 * H_M)
        hdir = _mlstm(qm, km, vm, gd, gd.transpose(0, 2, 1), bd, bd.transpose(0, 2, 1))

        st = _s5_state(us, wst, apow, l)
        y = _s5_out(us, toep, st, wout, dflat, l)
        s = _glu(y, w_glu_b, l)

        t = _merge(a, hdir, p, g_h_m, s, w_br_a_b, w_br_m_b, w_br_s_b, l)
        xs = _resid(t, w_out, xs, modrb, 2, 1024, l, "resid_mix")
        u = _ffn_in(xs, g_ffn, modrb, w_ffn_in_b, l)
        xs = _resid(u, w_ffn_out, xs, modrb, 5, 512, l, "resid_ffn")

    return _final_norm(xs, g_final).reshape(NB_, T, D)
```

```python
import math

import numpy as np
import jax
import jax.numpy as jnp
from jax import lax
from jax.experimental import pallas as pl
from jax.experimental.pallas import tpu as pltpu

F32 = jnp.float32
BF16 = jnp.bfloat16

D = 2048
NB_ = 2
T = 4096
TC = 256
GRID_W = 64
EPS = 1e-6
H_A, Q_LORA, KV_LORA, NOPE, ROPE, DV = 8, 512, 512, 128, 64, 128
ROPE_THETA = 10000.0
ATTN_SCALE = (NOPE + ROPE) ** -0.5
H_M, DH_M = 8, 128
D_M = H_M * DH_M
D_S, GROUP, P_S = 1024, 16, 64
G_S = D_S // GROUP
D_FF = ((8 * D // 3 + 255) // 256) * 256
OFF = {}
_o = 0
for _n, _w in (('cq', Q_LORA), ('ckv', KV_LORA), ('krope', ROPE), ('xm', D_M), ('om', D_M),
               ('gm', 4 * H_M), ('us', D_S), ('gates', 3 * D)):
    OFF[_n] = (_o, _w)
    _o += _w

NL = NB_ * T
NCX = NB_ * TC
R = NL + NCX
TM = 512
NBLK = R // TM
LC = 256
TQ = 256
TK = 256
AH = 4
SL = 8
NCH = R // SL
NSEG = 8
SLB = 128 // GROUP
NLB = D_S // 128
SW = SLB * P_S
LOG2E = math.log2(math.e)
PM_CQ, PM_CKV, PM_XM, PM_GATES, PM_OM, PM_W = 0, 512, 1024, 2048, 2048 + 3 * D, 3072 + 3 * D
SM_KR, SM_KP, SM_GM, SM_W = 0, 128, 256, 384
VMEM_LIMIT = 56 * 1024 * 1024
SEQ_STARTS = tuple(b * T for b in range(NB_)) + tuple(NL + b * TC for b in range(NB_))
SEQ_ENDS = tuple(b * T + T - 1 for b in range(NB_)) + tuple(NL + b * TC + TC - 1 for b in range(NB_))


def _cp(sem):
    return pltpu.CompilerParams(dimension_semantics=sem, vmem_limit_bytes=VMEM_LIMIT)


def _dot(a, b):
    return jnp.dot(a, b, preferred_element_type=F32)


def _dot_nt(a, b):
    return lax.dot_general(a, b, (((1,), (1,)), ((), ())), preferred_element_type=F32)


def _dot_tn(a, b):
    return lax.dot_general(a, b, (((0,), (0,)), ((), ())), preferred_element_type=F32)


def _sigmoid(x):
    return 1.0 / (1.0 + jnp.exp(-x))


def _log_sigmoid(x):
    return jnp.minimum(x, 0.0) - jnp.log(1.0 + jnp.exp(-jnp.abs(x)))


def _rms(x, g):
    xf = x.astype(F32)
    return xf * lax.rsqrt(jnp.mean(xf * xf, axis=-1, keepdims=True) + EPS) * g


def _ada_kernel(c_ref, w_ref, b_ref, o_ref):
    c = c_ref[...]
    s = (c * _sigmoid(c)).astype(BF16)
    o_ref[0] = _dot(s, w_ref[0].astype(BF16)) + b_ref[0]


def _ada(cvec, w_ada, b_ada):
    depth = w_ada.shape[0]
    tn = 1024
    return pl.pallas_call(
        _ada_kernel,
        grid=(depth, 6 * D // tn),
        in_specs=[pl.BlockSpec((8, D), lambda l, j: (0, 0)),
                  pl.BlockSpec((1, D, tn), lambda l, j: (l, 0, j)),
                  pl.BlockSpec((1, 1, tn), lambda l, j: (l, 0, j))],
        out_specs=pl.BlockSpec((1, 8, tn), lambda l, j: (l, 0, j)),
        out_shape=jax.ShapeDtypeStruct((depth, 8, 6 * D), F32),
        compiler_params=_cp(("arbitrary", "arbitrary")),
        name="ada",
    )(cvec, w_ada, b_ada.reshape(depth, 1, 6 * D))


def _modulate(x_ref, g_ref, sh_ref, sc_ref):
    return _rms(x_ref[...], g_ref[0]) * (1.0 + sc_ref[0]) + sh_ref[0]


def _inproj_kernel(x_ref, g_ref, sh_ref, sc_ref, w_ref, o_ref):
    h = _modulate(x_ref, g_ref, sh_ref, sc_ref).astype(BF16)
    o_ref[...] = _dot(h, w_ref[0]).astype(o_ref.dtype)


def _inproj(x, g, modrb, w, l, tn, out_dtype, name):
    n = w.shape[2]
    return pl.pallas_call(
        _inproj_kernel,
        grid=(n // tn, NBLK),
        in_specs=[pl.BlockSpec((TM, D), lambda j, i: (i, 0)),
                  pl.BlockSpec((1, 1, D), lambda j, i: (l, 0, 0)),
                  pl.BlockSpec((1, 1, D), lambda j, i: (i, 0, 0)),
                  pl.BlockSpec((1, 1, D), lambda j, i: (i, 0, 1)),
                  pl.BlockSpec((1, D, tn), lambda j, i: (l, 0, j))],
        out_specs=pl.BlockSpec((TM, tn), lambda j, i: (i, j)),
        out_shape=jax.ShapeDtypeStruct((R, n), out_dtype),
        compiler_params=_cp(("arbitrary", "arbitrary")),
        name=name,
    )(x, g, modrb, modrb, w)


def _inproj_side_kernel(x_ref, g_ref, sh_ref, sc_ref, w_ref, us_ref, o_ref):
    h = _modulate(x_ref, g_ref, sh_ref, sc_ref).astype(BF16)
    res = _dot(h, w_ref[0])
    for cb in range(NLB):
        us_ref[cb] = res[:, cb * 128:(cb + 1) * 128]
    o_ref[...] = res[:, D_S:]


def _inproj_side(x, g, modrb, w, l):
    n = D_S + SM_W
    return pl.pallas_call(
        _inproj_side_kernel,
        grid=(NBLK,),
        in_specs=[pl.BlockSpec((TM, D), lambda i: (i, 0)),
                  pl.BlockSpec((1, 1, D), lambda i: (l, 0, 0)),
                  pl.BlockSpec((1, 1, D), lambda i: (i, 0, 0)),
                  pl.BlockSpec((1, 1, D), lambda i: (i, 0, 1)),
                  pl.BlockSpec((1, D, n), lambda i: (l, 0, 0))],
        out_specs=[pl.BlockSpec((NLB, TM, 128), lambda i: (0, i, 0)),
                   pl.BlockSpec((TM, SM_W), lambda i: (i, 0))],
        out_shape=[jax.ShapeDtypeStruct((NLB, R, 128), F32),
                   jax.ShapeDtypeStruct((R, SM_W), F32)],
        compiler_params=_cp(("arbitrary",)),
        name="inproj_side",
    )(x, g, modrb, modrb, w)


def _mla_proj_kernel(cq_ref, ckv_ref, kr_ref, kp_ref, cos_ref, sin_ref, gq_ref, gkv_ref,
                     wqn_ref, wqr_ref, wqp_ref, wkn_ref, wvt_ref, q_ref, k_ref, vt_ref):
    hq = _rms(cq_ref[...], gq_ref[0]).astype(BF16)
    hk = _rms(ckv_ref[...], gkv_ref[0]).astype(BF16)
    cos = cos_ref[...]
    sin = sin_ref[...]
    qscale = ATTN_SCALE * LOG2E
    qn = _dot(hq, wqn_ref[0]) * qscale
    qr = _dot(hq, wqr_ref[0])
    qp = _dot(hq, wqp_ref[0])
    kn = _dot(hk, wkn_ref[0])
    vt = _dot_nt(wvt_ref[0], hk)
    kr = (kr_ref[...] * cos + kp_ref[...] * sin).astype(BF16)
    ones = jnp.ones((DV, TM), BF16)
    for h in range(H_A):
        lo = slice(h * 256, h * 256 + 128)
        hi = slice(h * 256 + 128, (h + 1) * 256)
        sl = slice(h * 128, (h + 1) * 128)
        q_ref[:, lo] = qn[:, sl].astype(BF16)
        q_ref[:, hi] = ((qr[:, sl] * cos + qp[:, sl] * sin) * qscale).astype(BF16)
        k_ref[:, lo] = kn[:, sl].astype(BF16)
        k_ref[:, hi] = kr
        vt_ref[lo, :] = vt[sl, :].astype(BF16)
        vt_ref[hi, :] = ones


def _mla_proj(p, psm, cosk, sink, gq, gkv, wqn, wqr, wqp, wkn, wvt, l):
    lw = lambda shp: pl.BlockSpec((1,) + shp, lambda i: (l, 0, 0))
    return pl.pallas_call(
        _mla_proj_kernel,
        grid=(NBLK,),
        in_specs=[pl.BlockSpec((TM, 512), lambda i: (i, PM_CQ // 512)),
                  pl.BlockSpec((TM, 512), lambda i: (i, PM_CKV // 512)),
                  pl.BlockSpec((TM, 128), lambda i: (i, SM_KR // 128)),
                  pl.BlockSpec((TM, 128), lambda i: (i, SM_KP // 128)),
                  pl.BlockSpec((TM, 128), lambda i: (i, 0)),
                  pl.BlockSpec((TM, 128), lambda i: (i, 0)),
                  pl.BlockSpec((1, 1, 512), lambda i: (l, 0, 0)),
                  pl.BlockSpec((1, 1, 512), lambda i: (l, 0, 0)),
                  lw((512, 1024)), lw((512, 1024)), lw((512, 1024)), lw((512, 1024)), lw((1024, 512))],
        out_specs=[pl.BlockSpec((TM, 2048), lambda i: (i, 0)),
                   pl.BlockSpec((TM, 2048), lambda i: (i, 0)),
                   pl.BlockSpec((2048, TM), lambda i: (0, i))],
        out_shape=[jax.ShapeDtypeStruct((R, 2048), BF16),
                   jax.ShapeDtypeStruct((R, 2048), BF16),
                   jax.ShapeDtypeStruct((2048, R), BF16)],
        compiler_params=_cp(("arbitrary",)),
        name="mla_proj",
    )(p, p, psm, psm, cosk, sink, gq, gkv, wqn, wqr, wqp, wkn, wvt)


def _attn_kernel(q_ref, kl_ref, kc_ref, vl_ref, vc_ref, o_ref, m_ref, acc_ref):
    qi = pl.program_id(2)
    heads = [slice(hh * 256, (hh + 1) * 256) for hh in range(AH)]
    qs = [q_ref[:, hs] for hs in heads]

    def chunk(hh, k, vt, m, acc):
        st = _dot_nt(k, qs[hh])
        m_new = jnp.maximum(m, jnp.max(st, axis=0, keepdims=True))
        p = jnp.exp2(st - m_new).astype(BF16)
        return m_new, jnp.exp2(m - m_new) * acc + _dot(vt, p)

    for hh, hs in enumerate(heads):
        m, acc = chunk(hh, kc_ref[:, hs], vc_ref[hs, :], jnp.full((1, TQ), -jnp.inf, F32),
                       jnp.zeros((2 * DV, TQ), F32))
        m_ref[hh] = m
        acc_ref[hh] = acc

    @pl.when(qi < T // TQ)
    def _():
        st = [(m_ref[hh], acc_ref[hh]) for hh in range(AH)]
        for j in range(T // TK):
            ks = slice(j * TK, (j + 1) * TK)
            st = [chunk(hh, kl_ref[ks, hs], vl_ref[hs, ks], *st[hh]) for hh, hs in enumerate(heads)]
        for hh in range(AH):
            acc_ref[hh] = st[hh][1]

    for hh in range(AH):
        acc = acc_ref[hh]
        o_ref[:, hh * DV:(hh + 1) * DV] = jnp.transpose(acc[0:DV, :] / acc[DV:2 * DV, :]).astype(o_ref.dtype)


def _attention(q, k, vt, ctx_queries):
    nq = T // TQ
    qrow = lambda b, h, i: jnp.where(i < nq, b * nq + i, NL // TQ + b)
    return pl.pallas_call(
        _attn_kernel,
        grid=(NB_, H_A // AH, nq + 1 if ctx_queries else nq),
        in_specs=[pl.BlockSpec((TQ, AH * 256), lambda b, h, i: (qrow(b, h, i), h)),
                  pl.BlockSpec((T, AH * 256), lambda b, h, i: (b, h)),
                  pl.BlockSpec((TC, AH * 256), lambda b, h, i: (NL // TC + b, h)),
                  pl.BlockSpec((AH * 256, T), lambda b, h, i: (h, b)),
                  pl.BlockSpec((AH * 256, TC), lambda b, h, i: (h, NL // TC + b))],
        out_specs=pl.BlockSpec((TQ, AH * DV), lambda b, h, i: (qrow(b, h, i), h)),
        out_shape=jax.ShapeDtypeStruct((R if ctx_queries else NL, H_A * DV), BF16),
        scratch_shapes=[pltpu.VMEM((AH, 1, TQ), F32), pltpu.VMEM((AH, 2 * DV, TQ), F32)],
        compiler_params=_cp(("arbitrary", "arbitrary", "arbitrary")),
        name="attention",
    )(q, k, k, vt, vt)


def _row_in(grow, rows):
    hit = grow == rows[0]
    for r in rows[1:]:
        hit = jnp.logical_or(hit, grow == r)
    return hit


def _mprep_kernel(x_ref, xp_ref, xn_ref, cw_ref, wq_ref, q_ref, k_ref, v_ref):
    i = pl.program_id(0)
    xb = x_ref[...]
    x = xb.astype(F32)
    prev_row = xp_ref[15:16, :].astype(F32)
    next_row = xn_ref[0:1, :].astype(F32)
    row = lax.broadcasted_iota(jnp.int32, (TM, 1), 0)
    grow = row + i * TM
    xprev = jnp.where(row == 0, prev_row, pltpu.roll(x, 1, axis=0))
    xprev = jnp.where(_row_in(grow, SEQ_STARTS), 0.0, xprev)
    xnext = jnp.where(row == TM - 1, next_row, pltpu.roll(x, TM - 1, axis=0))
    xnext = jnp.where(_row_in(grow, SEQ_ENDS), 0.0, xnext)
    cw = cw_ref[0]
    xc = xprev * cw[0:1, :] + x * cw[1:2, :] + xnext * cw[2:3, :]
    xcb = (xc * _sigmoid(xc)).astype(BF16)
    ones = jnp.ones((DH_M, TM), BF16)
    for h in range(H_M):
        sl = slice(h * DH_M, (h + 1) * DH_M)
        q_ref[:, sl] = _dot(xcb[:, sl], wq_ref[0, 0, h]).astype(BF16)
        k_ref[:, sl] = (_dot(xcb[:, sl], wq_ref[0, 1, h]) * DH_M ** -0.5).astype(BF16)
        v_ref[2 * h * DH_M:(2 * h + 1) * DH_M, :] = _dot_nt(wq_ref[0, 2, h], xb[:, sl]).astype(BF16)
        v_ref[(2 * h + 1) * DH_M:(2 * h + 2) * DH_M, :] = ones


def _mprep(p, conv_w, w_qkv, l):
    hb = TM // 16
    xcol = PM_XM // D_M
    return pl.pallas_call(
        _mprep_kernel,
        grid=(NBLK,),
        in_specs=[pl.BlockSpec((TM, D_M), lambda i: (i, xcol)),
                  pl.BlockSpec((16, D_M), lambda i: (jnp.maximum(i * hb - 1, 0), xcol)),
                  pl.BlockSpec((16, D_M), lambda i: (jnp.minimum((i + 1) * hb, R // 16 - 1), xcol)),
                  pl.BlockSpec((1, 3, D_M), lambda i: (l, 0, 0)),
                  pl.BlockSpec((1, 3, H_M, DH_M, DH_M), lambda i: (l, 0, 0, 0, 0))],
        out_specs=[pl.BlockSpec((TM, D_M), lambda i: (i, 0)),
                   pl.BlockSpec((TM, D_M), lambda i: (i, 0)),
                   pl.BlockSpec((2 * D_M, TM), lambda i: (0, i))],
        out_shape=[jax.ShapeDtypeStruct((R, D_M), BF16),
                   jax.ShapeDtypeStruct((R, D_M), BF16),
                   jax.ShapeDtypeStruct((2 * D_M, R), BF16)],
        compiler_params=_cp(("arbitrary",)),
        name="mlstm_prep",
    )(p, p, p, conv_w, w_qkv)


def _split3(x):
    h = x.astype(BF16)
    r = x - h.astype(F32)
    m = r.astype(BF16)
    l = (r - m.astype(F32)).astype(BF16)
    return h, m, l


def _mlstm_kernel(q_ref, k_ref, v_ref, g_ref, gt_ref, b_ref, bt_ref, h_ref, c_ref, m_ref):
    d = pl.program_id(1)
    s = pl.program_id(2)

    @pl.when(s == 0)
    def _():
        c_ref[...] = jnp.zeros_like(c_ref)
        m_ref[...] = jnp.zeros_like(m_ref)

    sign = 1 - 2 * d
    r = lax.broadcasted_iota(jnp.int32, (LC, LC), 0)
    c = lax.broadcasted_iota(jnp.int32, (LC, LC), 1)
    before = (c - r) * sign >= 0
    tri = jnp.where((r - c) * sign >= 0, 1.0, 0.0).astype(BF16)
    g = g_ref[0] + b_ref[0]
    gt = gt_ref[0] + bt_ref[0]
    ig_c = g[:, 0:H_M]
    lf_c = _log_sigmoid(g[:, H_M:2 * H_M])
    lf_r = _log_sigmoid(gt[H_M:2 * H_M, :])
    b_c = sum(_dot(tri, part) for part in _split3(lf_c))
    b_r = sum(_dot_nt(part, tri) for part in _split3(lf_r))
    tot = jnp.sum(lf_c, axis=0, keepdims=True)
    src = ig_c - b_c

    for h in range(H_M):
        sl = slice(h * DH_M, (h + 1) * DH_M)
        qh = q_ref[:, sl]
        kh = k_ref[:, sl]
        vth = v_ref[2 * h * DH_M:(2 * h + 2) * DH_M, :]
        bt = b_r[h:h + 1, :]
        tt = tot[:, h:h + 1]
        mp = m_ref[h:h + 1, 0:1]
        dm = jnp.where(before, bt + src[:, h:h + 1], -jnp.inf)
        m_inter = bt + mp
        m_t = jnp.maximum(jnp.max(dm, axis=0, keepdims=True), m_inter)
        sc = _dot_nt(kh, qh) * jnp.exp(dm - m_t)
        inter = jnp.exp(m_inter - m_t)
        st = c_ref[h]
        both = _dot(vth, sc.astype(BF16)) + inter * _dot_nt(st.astype(BF16), qh)
        den = both[DH_M:DH_M + 1, :]
        ht = both[0:DH_M, :] / jnp.maximum(jnp.abs(den), jnp.exp(-m_t))
        h_ref[0, :, sl] = jnp.transpose(ht)
        gg = tt + src[:, h:h + 1]
        m_new = jnp.maximum(tt + mp, jnp.max(gg, axis=0, keepdims=True))
        kw = (kh.astype(F32) * jnp.exp(gg - m_new)).astype(BF16)
        c_ref[h] = jnp.exp(tt + mp - m_new) * st + _dot(vth, kw)
        m_ref[h:h + 1, :] = jnp.broadcast_to(m_new, (1, DH_M))


def _mlstm(q, k, v, gd, gtd, bd, btd):
    nl = T // LC
    nsteps = nl + TC // LC

    def rb(b, d, s):
        lat = b * nl + jnp.where(d == 0, s - 1, nl - s)
        return jnp.where(s == 0, NL // LC + b, lat)

    qspec = pl.BlockSpec((LC, D_M), lambda b, d, s: (rb(b, d, s), 0))
    return pl.pallas_call(
        _mlstm_kernel,
        grid=(NB_, 2, nsteps),
        in_specs=[qspec, qspec,
                  pl.BlockSpec((2 * D_M, LC), lambda b, d, s: (0, rb(b, d, s))),
                  pl.BlockSpec((1, LC, 16), lambda b, d, s: (d, rb(b, d, s), 0)),
                  pl.BlockSpec((1, 16, LC), lambda b, d, s: (d, 0, rb(b, d, s))),
                  pl.BlockSpec((1, 1, 16), lambda b, d, s: (d, 0, 0)),
                  pl.BlockSpec((1, 16, 1), lambda b, d, s: (d, 0, 0))],
        out_specs=pl.BlockSpec((1, LC, D_M), lambda b, d, s: (d, rb(b, d, s), 0)),
        out_shape=jax.ShapeDtypeStruct((2, R, D_M), F32),
        scratch_shapes=[pltpu.VMEM((H_M, 2 * DH_M, DH_M), F32),
                        pltpu.VMEM((H_M, DH_M), F32)],
        compiler_params=_cp(("arbitrary", "arbitrary", "arbitrary")),
        name="mlstm_scan",
    )(q, k, v, gd, gtd, bd, btd)


S5_NLAT = T // SL
S5_NCTX = TC // SL
S5_SEG = S5_NLAT // NSEG


def _fold_copy(tok_ref, fold_ref, to_fold):
    def move(tok_idx, fold_rows, t):
        lanes = slice(t * 128, (t + 1) * 128)
        if to_fold:
            fold_ref[fold_rows, lanes] = tok_ref[tok_idx, :]
        else:
            tok_ref[tok_idx, :] = fold_ref[fold_rows, lanes]

    def body(j, carry):
        for b in range(NB_):
            rows = pl.ds(pl.multiple_of(b * S5_NLAT + j * NSEG, NSEG), NSEG)
            for t in range(SL):
                move(pl.ds(b * T + j * SL + t, NSEG, stride=S5_SEG * SL), rows, t)
        return carry

    lax.fori_loop(0, S5_SEG, body, 0)
    for t in range(SL):
        move(pl.ds(NL + t, NCX // SL, stride=SL), slice(NL // SL, NCH), t)


def _expand_block_diag(compact_ref, dense_ref, key_shift, key_mul, key_mask, row_gshift, col_gshift):
    comp = compact_ref[0]
    nrows, kc = comp.shape
    chunk = 512
    r = lax.broadcasted_iota(jnp.int32, (kc, chunk), 0)
    rg = (lax.broadcasted_iota(jnp.int32, (nrows, 1), 0) >> row_gshift) & (SLB - 1)
    for c0 in range(0, dense_ref.shape[1], chunk):
        q = lax.broadcasted_iota(jnp.int32, (kc, chunk), 1) + c0
        sel = jnp.where(r == (q >> key_shift) * key_mul + (q & key_mask), 1.0, 0.0).astype(BF16)
        cg = ((lax.broadcasted_iota(jnp.int32, (1, chunk), 1) + c0) >> col_gshift) & (SLB - 1)
        dense_ref[:, c0:c0 + chunk] = jnp.where(rg == cg, _dot(comp, sel), 0.0).astype(BF16)


def _cmul(ar, ai, sr, si):
    return ar * sr - ai * si, ar * si + ai * sr


def _s5_state_kernel(us_ref, wst_ref, a_ref, s_ref, e_ref, x_ref, w_ref):
    _fold_copy(us_ref.at[0], x_ref, True)
    _expand_block_diag(wst_ref, w_ref, 9, P_S, P_S - 1, 4, 6)
    e_ref[...] = _dot(x_ref[...].astype(BF16), w_ref[...])
    apow = a_ref[0]
    step_a = ((apow[0:1, :], apow[1:2, :]), (apow[2:3, :], apow[3:4, :]))
    nctx, seg = S5_NCTX, S5_SEG

    def advance(rows, d, st, write):
        cr = slice(2 * d * SW, (2 * d + 1) * SW)
        ci = slice((2 * d + 1) * SW, (2 * d + 2) * SW)
        dr = e_ref[rows, cr]
        di = e_ref[rows, ci]
        if write:
            e_ref[rows, cr] = st[0]
            e_ref[rows, ci] = st[1]
        nr, ni = _cmul(step_a[d][0], step_a[d][1], st[0], st[1])
        return nr + dr, ni + di

    def sweep(nsteps, rows_of, init, write):
        def body(kk, sts):
            out = []
            for b in range(NB_):
                out.append(advance(rows_of(b, kk), 0, sts[2 * b], write))
                out.append(advance(rows_of(b, nsteps - 1 - kk), 1, sts[2 * b + 1], write))
            return tuple(out)
        return lax.fori_loop(0, nsteps, body, init)

    z1 = jnp.zeros((1, SW), F32)
    carry = sweep(nctx, lambda b, kk: pl.ds(NL // SL + b * nctx + kk, 1), ((z1, z1),) * (2 * NB_), True)

    seg_rows = lambda b, kk: pl.ds(pl.multiple_of(b * S5_NLAT + kk * NSEG, NSEG), NSEG)
    z8 = jnp.zeros((NSEG, SW), F32)
    ends = sweep(seg, seg_rows, ((z8, z8),) * (2 * NB_), False)
    inits = []
    for ch in range(2 * NB_):
        d = ch % 2
        ar, ai = step_a[d]
        for _ in range(int(math.log2(seg))):
            ar, ai = ar * ar - ai * ai, 2.0 * ar * ai
        cur = carry[ch]
        rows_r, rows_i = [None] * NSEG, [None] * NSEG
        for kseg in (range(NSEG) if d == 0 else range(NSEG - 1, -1, -1)):
            rows_r[kseg], rows_i[kseg] = cur
            nr, ni = _cmul(ar, ai, cur[0], cur[1])
            cur = (nr + ends[ch][0][kseg:kseg + 1, :], ni + ends[ch][1][kseg:kseg + 1, :])
        inits.append((jnp.concatenate(rows_r, axis=0), jnp.concatenate(rows_i, axis=0)))
    sweep(seg, seg_rows, tuple(inits), True)
    s_ref[0] = e_ref[...].astype(BF16)


def _s5_state(us, wst, apow, l):
    return pl.pallas_call(
        _s5_state_kernel,
        grid=(NLB,),
        in_specs=[pl.BlockSpec((1, R, 128), lambda g: (g, 0, 0), pipeline_mode=pl.Buffered(1)),
                  pl.BlockSpec((1, SL * 128, 4 * P_S), lambda g: (l * NLB + g, 0, 0)),
                  pl.BlockSpec((1, 8, SW), lambda g: (l * NLB + g, 0, 0))],
        out_specs=pl.BlockSpec((1, NCH, 4 * SW), lambda g: (g, 0, 0)),
        out_shape=jax.ShapeDtypeStruct((NLB, NCH, 4 * SW), BF16),
        scratch_shapes=[pltpu.VMEM((NCH, 4 * SW), F32),
                        pltpu.VMEM((NCH, SL * 128), F32),
                        pltpu.VMEM((SL * 128, 4 * SW), BF16)],
        compiler_params=_cp(("arbitrary",)),
        name="s5_state",
    )(us, wst, apow)


def _s5_out_kernel(us_ref, toep_ref, s_ref, wo_ref, d_ref, y_ref, x_ref, tw_ref, ow_ref):
    _fold_copy(us_ref.at[0], x_ref, True)
    _expand_block_diag(toep_ref, tw_ref, 7, GROUP, GROUP - 1, 4, 4)
    _expand_block_diag(wo_ref, ow_ref, 7, GROUP, GROUP - 1, 6, 4)
    x = x_ref[...]
    y = _dot(x.astype(BF16), tw_ref[...]) + _dot(s_ref[0], ow_ref[...]) + d_ref[0] * x
    x_ref[...] = jax.nn.gelu(y)
    _fold_copy(y_ref.at[0], x_ref, False)


def _s5_out(us, toep, s, wout, dflat, l):
    return pl.pallas_call(
        _s5_out_kernel,
        grid=(NLB,),
        in_specs=[pl.BlockSpec((1, R, 128), lambda g: (g, 0, 0), pipeline_mode=pl.Buffered(1)),
                  pl.BlockSpec((1, SL * 128, SL * GROUP), lambda g: (l * NLB + g, 0, 0)),
                  pl.BlockSpec((1, NCH, 4 * SW), lambda g: (g, 0, 0), pipeline_mode=pl.Buffered(1)),
                  pl.BlockSpec((1, 4 * SW, SL * GROUP), lambda g: (l * NLB + g, 0, 0)),
                  pl.BlockSpec((1, 1, SL * 128), lambda g: (l * NLB + g, 0, 0))],
        out_specs=pl.BlockSpec((1, R, 128), lambda g: (g, 0, 0)),
        out_shape=jax.ShapeDtypeStruct((NLB, R, 128), F32),
        scratch_shapes=[pltpu.VMEM((NCH, SL * 128), F32),
                        pltpu.VMEM((SL * 128, SL * 128), BF16),
                        pltpu.VMEM((4 * SW, SL * 128), BF16)],
        compiler_params=_cp(("arbitrary",)),
        name="s5_out",
    )(us, toep, s, wout, dflat)


def _glu_kernel(y_ref, w_ref, o_ref):
    y = jnp.concatenate([y_ref[cb] for cb in range(NLB)], axis=1).astype(BF16)
    z = _dot(y, w_ref[0])
    o_ref[...] = (z[:, :D_S] * _sigmoid(z[:, D_S:])).astype(BF16)


def _glu(y, w, l):
    return pl.pallas_call(
        _glu_kernel,
        grid=(NBLK,),
        in_specs=[pl.BlockSpec((NLB, TM, 128), lambda i: (0, i, 0)),
                  pl.BlockSpec((1, D_S, 2 * D_S), lambda i: (l, 0, 0))],
        out_specs=pl.BlockSpec((TM, D_S), lambda i: (i, 0)),
        out_shape=jax.ShapeDtypeStruct((R, D_S), BF16),
        compiler_params=_cp(("arbitrary",)),
        name="s5_glu",
    )(y, w)


def _s5_params(a_re, a_im, log_dt, b_re, b_im, c_re, c_im, dskip):
    dt = jnp.exp(log_dt)[:, :, None]
    lam_r, lam_i = a_re * dt, a_im * dt
    mag = jnp.exp(lam_r)
    ar, ai = mag * jnp.cos(lam_i), mag * jnp.sin(lam_i)
    den = a_re * a_re + a_im * a_im
    nr, ni = ar - 1.0, ai
    cr = (nr * a_re + ni * a_im) / den
    ci = (ni * a_re - nr * a_im) / den
    cpr = c_re[None] * cr[:, :, None, :] - c_im[None] * ci[:, :, None, :]
    cpi = c_re[None] * ci[:, :, None, :] + c_im[None] * cr[:, :, None, :]
    j = jnp.arange(SL + 1, dtype=F32)[:, None, None, None]
    pm = jnp.exp(lam_r[None] * j)
    pr, pi = pm * jnp.cos(lam_i[None] * j), pm * jnp.sin(lam_i[None] * j)
    abr = pr[..., None] * b_re[None, None] - pi[..., None] * b_im[None, None]
    abi = pr[..., None] * b_im[None, None] + pi[..., None] * b_re[None, None]
    kern = (jnp.einsum('dgcp,jdgpe->jdgce', cpr, abr[:SL]) - jnp.einsum('dgcp,jdgpe->jdgce', cpi, abi[:SL]))
    zero = jnp.zeros_like(kern[0, 0])
    rows = []
    for t_in in range(SL):
        blks = []
        for t_out in range(SL):
            blk = zero
            if t_out >= t_in:
                blk = blk + kern[t_out - t_in, 0]
            if t_in >= t_out:
                blk = blk + kern[t_in - t_out, 1]
            blks.append(blk)
        rows.append(jnp.stack(blks, axis=0))
    t5 = jnp.stack(rows, axis=0).reshape(SL, SL, NLB, SLB, GROUP, GROUP)
    toep = t5.transpose(2, 0, 3, 5, 1, 4).reshape(NLB, SL * 128, SL * GROUP)
    rev = lambda a, lo: jnp.stack([a[lo + SL - 1 - t] for t in range(SL)], axis=0)
    ws = jnp.stack([rev(abr[:, 0], 0), rev(abi[:, 0], 0), abr[:SL, 1], abi[:SL, 1]], axis=0)
    ws = ws.reshape(4, SL, NLB, SLB, P_S, GROUP)
    wst = ws.transpose(2, 1, 3, 5, 0, 4).reshape(NLB, SL * 128, 4 * P_S)
    def readout(pw_r, pw_i, d):
        re = cpr[d][None] * pw_r[:, :, None, :] - cpi[d][None] * pw_i[:, :, None, :]
        im = cpr[d][None] * pw_i[:, :, None, :] + cpi[d][None] * pw_r[:, :, None, :]
        return re, -im
    of_re, of_im = readout(pr[1:SL + 1, 0], pi[1:SL + 1, 0], 0)
    ob_re, ob_im = readout(rev(pr[:, 1], 1), rev(pi[:, 1], 1), 1)
    wo = jnp.stack([of_re, of_im, ob_re, ob_im], axis=0).reshape(4, SL, NLB, SLB, GROUP, P_S)
    wout = wo.transpose(2, 0, 3, 5, 1, 4).reshape(NLB, 4 * SW, SL * GROUP)
    blk = lambda a: a.reshape(NLB, 1, SW)
    apow = jnp.concatenate([blk(pr[SL, 0]), blk(pi[SL, 0]), blk(pr[SL, 1]), blk(pi[SL, 1]),
                            jnp.zeros((NLB, 4, SW), F32)], axis=1)
    dflat = jnp.tile(dskip.reshape(NLB, 1, 128), (1, 1, SL))
    return toep.astype(BF16), wst.astype(BF16), wout.astype(BF16), apow, dflat


def _mix_out_kernel(a_ref, hf_ref, hb_ref, om_ref, gh_ref, s_ref, ga_ref, gm_ref, gs_ref,
                    wa_ref, wm_ref, ws_ref, wo_ref, x_ref, gt_ref, gf_ref, sh_ref, sc_ref, x1_ref, h2_ref):
    hsum = hf_ref[0] + hb_ref[0]
    parts = []
    for h in range(H_M):
        xh = hsum[:, h * DH_M:(h + 1) * DH_M]
        parts.append(xh * lax.rsqrt(jnp.mean(xh * xh, axis=-1, keepdims=True) + EPS))
    hn = jnp.concatenate(parts, axis=1) * gh_ref[0]
    m = (_sigmoid(om_ref[...].astype(F32)) * hn).astype(BF16)
    t = (_sigmoid(ga_ref[...].astype(F32)) * _dot(a_ref[...], wa_ref[0])
         + _sigmoid(gm_ref[...].astype(F32)) * _dot(m, wm_ref[0])
         + _sigmoid(gs_ref[...].astype(F32)) * _dot(s_ref[...], ws_ref[0]))
    x1 = x_ref[...] + gt_ref[0] * _dot(t.astype(BF16), wo_ref[0])
    x1_ref[...] = x1
    h2_ref[...] = (_rms(x1, gf_ref[0]) * (1.0 + sc_ref[0]) + sh_ref[0]).astype(BF16)


def _mix_out(a, hdir, p, gh, s, wa, wm, ws, wo, x, modrb, g_ffn, l, nrows):
    tm = 256
    per = TM // tm
    gcol = PM_GATES // D
    row = lambda w, c: pl.BlockSpec((tm, w), lambda i: (i, c))
    once = pl.Buffered(1)
    wspec = pl.BlockSpec((1, 1024, D), lambda i: (l, 0, 0), pipeline_mode=once)
    mod = lambda c: pl.BlockSpec((1, 1, D), lambda i: (i // per, 0, c))
    return pl.pallas_call(
        _mix_out_kernel,
        grid=(nrows // tm,),
        in_specs=[row(1024, 0),
                  pl.BlockSpec((1, tm, D_M), lambda i: (0, i, 0)),
                  pl.BlockSpec((1, tm, D_M), lambda i: (1, i, 0)),
                  row(1024, PM_OM // 1024),
                  pl.BlockSpec((1, 1, D_M), lambda i: (l, 0, 0)),
                  row(1024, 0),
                  row(D, gcol), row(D, gcol + 1), row(D, gcol + 2),
                  wspec, wspec, wspec,
                  pl.BlockSpec((1, D, D), lambda i: (l, 0, 0), pipeline_mode=once),
                  row(D, 0), mod(2),
                  pl.BlockSpec((1, 1, D), lambda i: (l, 0, 0)), mod(3), mod(4)],
        out_specs=[row(D, 0), row(D, 0)],
        out_shape=[jax.ShapeDtypeStruct((nrows, D), F32), jax.ShapeDtypeStruct((nrows, D), BF16)],
        compiler_params=_cp(("arbitrary",)),
        name="mix_out",
    )(a, hdir, hdir, p, gh, s, p, p, p, wa, wm, ws, wo, x, modrb, g_ffn, modrb, modrb)


def _resid_kernel(t_ref, w_ref, x_ref, gt_ref, o_ref, wb_ref):
    @pl.when(pl.program_id(1) == 0)
    def _():
        wb_ref[...] = w_ref[0].astype(BF16)

    o_ref[...] = x_ref[...] + gt_ref[0] * _dot(t_ref[...], wb_ref[...])


def _resid(t, w, x, modrb, gate_chunk, tn, l, nrows, name):
    kdim = t.shape[1]
    nj = D // tn
    return pl.pallas_call(
        _resid_kernel,
        grid=(nj, nrows // TM),
        in_specs=[pl.BlockSpec((TM, kdim), lambda j, i: (i, 0)),
                  pl.BlockSpec((1, kdim, tn), lambda j, i: (l, 0, j)),
                  pl.BlockSpec((TM, tn), lambda j, i: (i, j)),
                  pl.BlockSpec((1, 1, tn), lambda j, i: (i, 0, gate_chunk * nj + j))],
        out_specs=pl.BlockSpec((TM, tn), lambda j, i: (i, j)),
        out_shape=jax.ShapeDtypeStruct((nrows, D), F32),
        scratch_shapes=[pltpu.VMEM((kdim, tn), BF16)],
        compiler_params=_cp(("arbitrary", "arbitrary")),
        name=name,
    )(t, w, x, modrb)


def _ffn_in_kernel(h_ref, wa_ref, wb_ref, o_ref, was_ref, wbs_ref):
    @pl.when(pl.program_id(1) == 0)
    def _():
        was_ref[...] = wa_ref[0].astype(BF16)
        wbs_ref[...] = wb_ref[0].astype(BF16)

    h = h_ref[...]
    a = _dot(h, was_ref[...])
    b = _dot(h, wbs_ref[...])
    o_ref[...] = (a * _sigmoid(a) * b).astype(BF16)


def _ffn_in(h, w, l, nrows):
    tn = 512
    nj = D_FF // tn
    return pl.pallas_call(
        _ffn_in_kernel,
        grid=(nj, nrows // TM),
        in_specs=[pl.BlockSpec((TM, D), lambda j, i: (i, 0)),
                  pl.BlockSpec((1, D, tn), lambda j, i: (l, 0, j)),
                  pl.BlockSpec((1, D, tn), lambda j, i: (l, 0, nj + j))],
        out_specs=pl.BlockSpec((TM, tn), lambda j, i: (i, j)),
        out_shape=jax.ShapeDtypeStruct((nrows, D_FF), BF16),
        scratch_shapes=[pltpu.VMEM((D, tn), BF16), pltpu.VMEM((D, tn), BF16)],
        compiler_params=_cp(("arbitrary", "arbitrary")),
        name="ffn_in",
    )(h, w, w)


def _final_norm_kernel(x_ref, g_ref, o_ref):
    o_ref[...] = _rms(x_ref[...], g_ref[...])


def _final_norm(x, g):
    return pl.pallas_call(
        _final_norm_kernel,
        grid=(NL // TM,),
        in_specs=[pl.BlockSpec((TM, D), lambda i: (i, 0)),
                  pl.BlockSpec((1, D), lambda i: (0, 0))],
        out_specs=pl.BlockSpec((TM, D), lambda i: (i, 0)),
        out_shape=jax.ShapeDtypeStruct((NL, D), F32),
        compiler_params=_cp(("arbitrary",)),
        name="final_norm",
    )(x, g.reshape(1, D))


def _rope_tables():
    rows = T // GRID_W
    rr, cc = jnp.meshgrid(jnp.arange(rows, dtype=F32), jnp.arange(GRID_W, dtype=F32), indexing='ij')
    rr, cc = rr.reshape(-1), cc.reshape(-1)
    half = ROPE // 2
    inv = 1.0 / (ROPE_THETA ** (jnp.arange(0, half, 2, dtype=F32) / half))
    ang = jnp.stack([rr[:, None] * inv, cc[:, None] * inv], axis=1)
    cos = jnp.cos(ang)
    sin = jnp.sin(ang)
    cos_f = jnp.stack([cos, cos], axis=2).reshape(T, ROPE)
    sin_f = jnp.stack([-sin, sin], axis=2).reshape(T, ROPE)
    pad = lambda a: jnp.concatenate([a, jnp.zeros((a.shape[0], 128 - ROPE), F32)], axis=1)
    cos_l, sin_l = pad(cos_f), pad(sin_f)
    cos_c = pad(jnp.ones((NCX, ROPE), F32))
    sin_c = jnp.zeros((NCX, 128), F32)
    return (jnp.concatenate([cos_l] * NB_ + [cos_c], axis=0),
            jnp.concatenate([sin_l] * NB_ + [sin_c], axis=0))


def _rope_partner(w):
    s = w.shape[:-1]
    w4 = w.reshape(s + (2, 2, ROPE // 4))
    return jnp.concatenate([w4[..., 1:2, :], w4[..., 0:1, :]], axis=-2).reshape(s + (ROPE,))


def _cols(w, name):
    o, n = OFF[name]
    return w[..., o:o + n]


def _prep_weights(w_in, w_uq, w_ukv):
    depth = w_in.shape[0]
    wmain = jnp.concatenate([_cols(w_in, n) for n in ('cq', 'ckv', 'xm', 'gates', 'om')], axis=-1).astype(BF16)
    kr = _cols(w_in, 'krope')
    z64 = jnp.zeros((depth, D, 64), F32)
    wsmall = jnp.concatenate([_cols(w_in, 'us'), kr, z64, _rope_partner(kr), z64, _cols(w_in, 'gm'),
                              jnp.zeros((depth, D, 128 - 4 * H_M), F32)], axis=-1).astype(BF16)
    uq = w_uq.reshape(depth, Q_LORA, H_A, NOPE + ROPE)
    wqn = uq[..., :NOPE].reshape(depth, Q_LORA, H_A * NOPE)
    qr = uq[..., NOPE:]
    zq = jnp.zeros_like(qr)
    wqr = jnp.concatenate([qr, zq], axis=-1).reshape(depth, Q_LORA, H_A * 128)
    wqp = jnp.concatenate([_rope_partner(qr), zq], axis=-1).reshape(depth, Q_LORA, H_A * 128)
    ukv = w_ukv.reshape(depth, KV_LORA, H_A, NOPE + DV)
    wkn = ukv[..., :NOPE].reshape(depth, KV_LORA, H_A * NOPE)
    wvt = ukv[..., NOPE:].reshape(depth, KV_LORA, H_A * DV).transpose(0, 2, 1)
    return wmain, wsmall, tuple(a.astype(BF16) for a in (wqn, wqr, wqp, wkn, wvt))


def kernel(x, c, ctx, c_ctx, w_ada, b_ada, g_mix, g_ffn, w_in, g_cq, w_uq, g_ckv, w_ukv, conv_m, w_qkv_m, b_gate_m, g_h_m, s5_a_re, s5_a_im, s5_log_dt, s5_b_re, s5_b_im, s5_c_re, s5_c_im, s5_d, w_glu, w_br_a, w_br_m, w_br_s, w_out, w_ffn_in, w_ffn_out, g_final):
    depth = w_ada.shape[0]
    xs = jnp.concatenate([x.reshape(NL, D), ctx.reshape(NCX, D)], axis=0)
    cvec = jnp.concatenate([c, c_ctx[None], jnp.zeros((8 - NB_ - 1, D), F32)], axis=0)
    mod = _ada(cvec, w_ada, b_ada)
    blocks_per_batch = T // TM
    cosk, sink = _rope_tables()
    wmain, wsmall, mla_w = _prep_weights(w_in, w_uq, w_ukv)
    toep, wst, wout, apow, dflat = (a.reshape((depth * NLB,) + a.shape[2:]) for a in jax.vmap(_s5_params)(
        s5_a_re, s5_a_im, s5_log_dt, s5_b_re, s5_b_im, s5_c_re, s5_c_im, s5_d))
    w_out_b = w_out.astype(BF16)
    w_qkv_b = jnp.concatenate([w_qkv_m[:, :2], jnp.swapaxes(w_qkv_m[:, 2:], -1, -2)], axis=1).astype(BF16)
    w_glu_b = w_glu.astype(BF16)
    w_br_a_b, w_br_m_b, w_br_s_b = w_br_a.astype(BF16), w_br_m.astype(BF16), w_br_s.astype(BF16)
    gain = lambda g: g.reshape(depth, 1, g.shape[-1])
    g_mix, g_ffn, g_cq, g_ckv, g_h_m = gain(g_mix), gain(g_ffn), gain(g_cq), gain(g_ckv), gain(g_h_m)

    for l in range(depth):
        modrb = jnp.concatenate(
            [jnp.broadcast_to(mod[l, b:b + 1], (blocks_per_batch, 6 * D)) for b in range(NB_)]
            + [jnp.broadcast_to(mod[l, NB_:NB_ + 1], (NCX // TM, 6 * D))], axis=0).reshape(NBLK, 1, 6 * D)
        p = _inproj(xs, g_mix, modrb, wmain, l, 1024, BF16, "inproj")
        us, psm = _inproj_side(xs, g_mix, modrb, wsmall, l)

        last = l == depth - 1
        nrows = NL if last else R
        q, k, vt = _mla_proj(p, psm, cosk, sink, g_cq, g_ckv, *mla_w, l)
        a = _attention(q, k, vt, not last)

        qm, km, vm = _mprep(p, conv_m, w_qkv_b, l)
        gm = psm[:, SM_GM:SM_GM + 4 * H_M]
        gd = jnp.stack([gm[:, :2 * H_M], gm[:, 2 * H_M:]], axis=0)
        bd = b_gate_m[l].reshape(2, 1, 2 * H_M)
        hdir = _mlstm(qm, km, vm, gd, gd.transpose(0, 2, 1), bd, bd.transpose(0, 2, 1))

        st = _s5_state(us, wst, apow, l)
        y = _s5_out(us, toep, st, wout, dflat, l)
        s = _glu(y, w_glu_b, l)

        x1, h2 = _mix_out(a, hdir, p, g_h_m, s, w_br_a_b, w_br_m_b, w_br_s_b, w_out_b, xs, modrb, g_ffn, l, nrows)
        u = _ffn_in(h2, w_ffn_in, l, nrows)
        xs = _resid(u, w_ffn_out, x1, modrb, 5, 512, l, nrows, "resid_ffn")

    return _final_norm(xs, g_final).reshape(NB_, T, D)
```

```python
import functools
import math

import numpy as np
import jax
import jax.numpy as jnp
from jax import lax
from jax.experimental import pallas as pl
from jax.experimental.pallas import tpu as pltpu

F32 = jnp.float32
BF16 = jnp.bfloat16

D = 2048
NB_ = 2
T = 4096
TC = 256
GRID_W = 64
EPS = 1e-6
H_A, Q_LORA, KV_LORA, NOPE, ROPE, DV = 8, 512, 512, 128, 64, 128
ROPE_THETA = 10000.0
ATTN_SCALE = (NOPE + ROPE) ** -0.5
H_M, DH_M = 8, 128
D_M = H_M * DH_M
D_S, GROUP, P_S = 1024, 16, 64
G_S = D_S // GROUP
D_FF = ((8 * D // 3 + 255) // 256) * 256
OFF = {}
_o = 0
for _n, _w in (('cq', Q_LORA), ('ckv', KV_LORA), ('krope', ROPE), ('xm', D_M), ('om', D_M),
               ('gm', 4 * H_M), ('us', D_S), ('gates', 3 * D)):
    OFF[_n] = (_o, _w)
    _o += _w

NL = NB_ * T
NCX = NB_ * TC
R = NL + NCX
TM = 512
NBLK = R // TM
LC = 256
TQ = 256
TK = 256
AH = 4
SL = 8
NCH = R // SL
NSEG = 8
SLB = 128 // GROUP
NLB = D_S // 128
SW = SLB * P_S
LOG2E = math.log2(math.e)
PM_CQ, PM_CKV, PM_XM, PM_GATES, PM_OM, PM_W = 0, 512, 1024, 2048, 2048 + 3 * D, 3072 + 3 * D
SM_KR, SM_KP, SM_GM, SM_W = 0, 128, 256, 384
VMEM_LIMIT = 56 * 1024 * 1024
SEQ_STARTS = tuple(b * T for b in range(NB_)) + tuple(NL + b * TC for b in range(NB_))
SEQ_ENDS = tuple(b * T + T - 1 for b in range(NB_)) + tuple(NL + b * TC + TC - 1 for b in range(NB_))


def _cp(sem):
    return pltpu.CompilerParams(dimension_semantics=sem, vmem_limit_bytes=VMEM_LIMIT)


def _dot(a, b):
    return jnp.dot(a, b, preferred_element_type=F32)


def _dot_nt(a, b):
    return lax.dot_general(a, b, (((1,), (1,)), ((), ())), preferred_element_type=F32)


def _dot_tn(a, b):
    return lax.dot_general(a, b, (((0,), (0,)), ((), ())), preferred_element_type=F32)


def _sigmoid(x):
    return 1.0 / (1.0 + jnp.exp(-x))


def _log_sigmoid(x):
    return jnp.minimum(x, 0.0) - jnp.log(1.0 + jnp.exp(-jnp.abs(x)))


def _rms(x, g):
    xf = x.astype(F32)
    return xf * lax.rsqrt(jnp.mean(xf * xf, axis=-1, keepdims=True) + EPS) * g


def _ada_kernel(c_ref, w_ref, b_ref, o_ref):
    c = c_ref[...]
    s = (c * _sigmoid(c)).astype(BF16)
    o_ref[0] = _dot(s, w_ref[0].astype(BF16)) + b_ref[0]


def _ada(cvec, w_ada, b_ada):
    depth = w_ada.shape[0]
    tn = 1024
    return pl.pallas_call(
        _ada_kernel,
        grid=(depth, 6 * D // tn),
        in_specs=[pl.BlockSpec((8, D), lambda l, j: (0, 0)),
                  pl.BlockSpec((1, D, tn), lambda l, j: (l, 0, j)),
                  pl.BlockSpec((1, 1, tn), lambda l, j: (l, 0, j))],
        out_specs=pl.BlockSpec((1, 8, tn), lambda l, j: (l, 0, j)),
        out_shape=jax.ShapeDtypeStruct((depth, 8, 6 * D), F32),
        compiler_params=_cp(("arbitrary", "arbitrary")),
        name="ada",
    )(cvec, w_ada, b_ada.reshape(depth, 1, 6 * D))


def _modulate(x_ref, g_ref, sh_ref, sc_ref):
    return _rms(x_ref[...], g_ref[0]) * (1.0 + sc_ref[0]) + sh_ref[0]


IP_TN = 1024
IP_STARTS = ((OFF['cq'][0], OFF['xm'][0]) + tuple(OFF['gates'][0] + k * IP_TN for k in range(3 * D // IP_TN))
             + (OFF['om'][0],))
IP_SHIFTS = tuple(s % IP_TN for s in IP_STARTS)
assert all(s < 128 for s in IP_SHIFTS) and OFF['ckv'][0] == Q_LORA and PM_W == IP_TN * len(IP_STARTS)


def _ip_window(j):
    idx = IP_STARTS[-1] // IP_TN
    for jj in range(len(IP_STARTS) - 2, -1, -1):
        idx = jnp.where(j == jj, IP_STARTS[jj] // IP_TN, idx)
    return idx


def _inproj_kernel(x_ref, g_ref, sh_ref, sc_ref, wm_ref, we_ref, o_ref, wb_ref):
    j = pl.program_id(0)

    @pl.when(pl.program_id(1) == 0)
    def _():
        for shift in sorted(set(IP_SHIFTS)):
            hit = functools.reduce(jnp.logical_or, [j == jj for jj, s in enumerate(IP_SHIFTS) if s == shift])

            @pl.when(hit)
            def _(shift=shift):
                if shift == 0:
                    wb_ref[...] = wm_ref[0].astype(BF16)
                else:
                    w = jnp.concatenate([wm_ref[0], we_ref[0]], axis=1)
                    wb_ref[...] = w[:, shift:shift + IP_TN].astype(BF16)

    h = _modulate(x_ref, g_ref, sh_ref, sc_ref).astype(BF16)
    o_ref[...] = _dot(h, wb_ref[...]).astype(o_ref.dtype)


def _inproj(x, g, modrb, w_in, l):
    return pl.pallas_call(
        _inproj_kernel,
        grid=(len(IP_STARTS), NBLK),
        in_specs=[pl.BlockSpec((TM, D), lambda j, i: (i, 0)),
                  pl.BlockSpec((1, 1, D), lambda j, i: (l, 0, 0)),
                  pl.BlockSpec((1, 1, D), lambda j, i: (i, 0, 0)),
                  pl.BlockSpec((1, 1, D), lambda j, i: (i, 0, 1)),
                  pl.BlockSpec((1, D, IP_TN), lambda j, i: (l, 0, _ip_window(j))),
                  pl.BlockSpec((1, D, 128), lambda j, i: (l, 0, (_ip_window(j) + 1) * (IP_TN // 128)))],
        out_specs=pl.BlockSpec((TM, IP_TN), lambda j, i: (i, j)),
        out_shape=jax.ShapeDtypeStruct((R, PM_W), BF16),
        scratch_shapes=[pltpu.VMEM((D, IP_TN), BF16)],
        compiler_params=_cp(("arbitrary", "arbitrary")),
        name="inproj",
    )(x, g, modrb, modrb, w_in, w_in)


def _inproj_side_kernel(x_ref, g_ref, sh_ref, sc_ref, w_ref, us_ref, o_ref):
    h = _modulate(x_ref, g_ref, sh_ref, sc_ref).astype(BF16)
    res = _dot(h, w_ref[0])
    for cb in range(NLB):
        us_ref[cb] = res[:, cb * 128:(cb + 1) * 128]
    o_ref[...] = res[:, D_S:]


def _inproj_side(x, g, modrb, w, l):
    n = D_S + SM_W
    return pl.pallas_call(
        _inproj_side_kernel,
        grid=(NBLK,),
        in_specs=[pl.BlockSpec((TM, D), lambda i: (i, 0)),
                  pl.BlockSpec((1, 1, D), lambda i: (l, 0, 0)),
                  pl.BlockSpec((1, 1, D), lambda i: (i, 0, 0)),
                  pl.BlockSpec((1, 1, D), lambda i: (i, 0, 1)),
                  pl.BlockSpec((1, D, n), lambda i: (l, 0, 0))],
        out_specs=[pl.BlockSpec((NLB, TM, 128), lambda i: (0, i, 0)),
                   pl.BlockSpec((TM, SM_W), lambda i: (i, 0))],
        out_shape=[jax.ShapeDtypeStruct((NLB, R, 128), F32),
                   jax.ShapeDtypeStruct((R, SM_W), F32)],
        compiler_params=_cp(("arbitrary",)),
        name="inproj_side",
    )(x, g, modrb, modrb, w)


def _mla_proj_kernel(cq_ref, ckv_ref, kr_ref, kp_ref, cos_ref, sin_ref, gq_ref, gkv_ref,
                     wqn_ref, wqr_ref, wqp_ref, wkn_ref, wvt_ref, q_ref, k_ref, vt_ref):
    hq = _rms(cq_ref[...], gq_ref[0]).astype(BF16)
    hk = _rms(ckv_ref[...], gkv_ref[0]).astype(BF16)
    cos = cos_ref[...]
    sin = sin_ref[...]
    qscale = ATTN_SCALE * LOG2E
    qn = _dot(hq, wqn_ref[0]) * qscale
    qr = _dot(hq, wqr_ref[0])
    qp = _dot(hq, wqp_ref[0])
    kn = _dot(hk, wkn_ref[0])
    vt = _dot_nt(wvt_ref[0], hk)
    kr = (kr_ref[...] * cos + kp_ref[...] * sin).astype(BF16)
    ones = jnp.ones((DV, TM), BF16)
    for h in range(H_A):
        lo = slice(h * 256, h * 256 + 128)
        hi = slice(h * 256 + 128, (h + 1) * 256)
        sl = slice(h * 128, (h + 1) * 128)
        q_ref[:, lo] = qn[:, sl].astype(BF16)
        q_ref[:, hi] = ((qr[:, sl] * cos + qp[:, sl] * sin) * qscale).astype(BF16)
        k_ref[:, lo] = kn[:, sl].astype(BF16)
        k_ref[:, hi] = kr
        vt_ref[lo, :] = vt[sl, :].astype(BF16)
        vt_ref[hi, :] = ones


def _mla_proj(p, psm, cosk, sink, gq, gkv, wqn, wqr, wqp, wkn, wvt, l):
    lw = lambda shp: pl.BlockSpec((1,) + shp, lambda i: (l, 0, 0))
    return pl.pallas_call(
        _mla_proj_kernel,
        grid=(NBLK,),
        in_specs=[pl.BlockSpec((TM, 512), lambda i: (i, PM_CQ // 512)),
                  pl.BlockSpec((TM, 512), lambda i: (i, PM_CKV // 512)),
                  pl.BlockSpec((TM, 128), lambda i: (i, SM_KR // 128)),
                  pl.BlockSpec((TM, 128), lambda i: (i, SM_KP // 128)),
                  pl.BlockSpec((TM, 128), lambda i: (i, 0)),
                  pl.BlockSpec((TM, 128), lambda i: (i, 0)),
                  pl.BlockSpec((1, 1, 512), lambda i: (l, 0, 0)),
                  pl.BlockSpec((1, 1, 512), lambda i: (l, 0, 0)),
                  lw((512, 1024)), lw((512, 1024)), lw((512, 1024)), lw((512, 1024)), lw((1024, 512))],
        out_specs=[pl.BlockSpec((TM, 2048), lambda i: (i, 0)),
                   pl.BlockSpec((TM, 2048), lambda i: (i, 0)),
                   pl.BlockSpec((2048, TM), lambda i: (0, i))],
        out_shape=[jax.ShapeDtypeStruct((R, 2048), BF16),
                   jax.ShapeDtypeStruct((R, 2048), BF16),
                   jax.ShapeDtypeStruct((2048, R), BF16)],
        compiler_params=_cp(("arbitrary",)),
        name="mla_proj",
    )(p, p, psm, psm, cosk, sink, gq, gkv, wqn, wqr, wqp, wkn, wvt)


def _attn_kernel(q_ref, kl_ref, kc_ref, vl_ref, vc_ref, o_ref, m_ref, acc_ref):
    qi = pl.program_id(2)
    heads = [slice(hh * 256, (hh + 1) * 256) for hh in range(AH)]
    qs = [q_ref[:, hs] for hs in heads]

    def chunk(hh, k, vt, m, acc):
        st = _dot_nt(k, qs[hh])
        m_new = jnp.maximum(m, jnp.max(st, axis=0, keepdims=True))
        p = jnp.exp2(st - m_new).astype(BF16)
        return m_new, jnp.exp2(m - m_new) * acc + _dot(vt, p)

    for hh, hs in enumerate(heads):
        m, acc = chunk(hh, kc_ref[:, hs], vc_ref[hs, :], jnp.full((1, TQ), -jnp.inf, F32),
                       jnp.zeros((2 * DV, TQ), F32))
        m_ref[hh] = m
        acc_ref[hh] = acc

    @pl.when(qi < T // TQ)
    def _():
        st = [(m_ref[hh], acc_ref[hh]) for hh in range(AH)]
        for j in range(T // TK):
            ks = slice(j * TK, (j + 1) * TK)
            st = [chunk(hh, kl_ref[ks, hs], vl_ref[hs, ks], *st[hh]) for hh, hs in enumerate(heads)]
        for hh in range(AH):
            acc_ref[hh] = st[hh][1]

    for hh in range(AH):
        acc = acc_ref[hh]
        o_ref[:, hh * DV:(hh + 1) * DV] = jnp.transpose(acc[0:DV, :] / acc[DV:2 * DV, :]).astype(o_ref.dtype)


def _attention(q, k, vt, ctx_queries):
    nq = T // TQ
    qrow = lambda b, h, i: jnp.where(i < nq, b * nq + i, NL // TQ + b)
    return pl.pallas_call(
        _attn_kernel,
        grid=(NB_, H_A // AH, nq + 1 if ctx_queries else nq),
        in_specs=[pl.BlockSpec((TQ, AH * 256), lambda b, h, i: (qrow(b, h, i), h)),
                  pl.BlockSpec((T, AH * 256), lambda b, h, i: (b, h)),
                  pl.BlockSpec((TC, AH * 256), lambda b, h, i: (NL // TC + b, h)),
                  pl.BlockSpec((AH * 256, T), lambda b, h, i: (h, b)),
                  pl.BlockSpec((AH * 256, TC), lambda b, h, i: (h, NL // TC + b))],
        out_specs=pl.BlockSpec((TQ, AH * DV), lambda b, h, i: (qrow(b, h, i), h)),
        out_shape=jax.ShapeDtypeStruct((R if ctx_queries else NL, H_A * DV), BF16),
        scratch_shapes=[pltpu.VMEM((AH, 1, TQ), F32), pltpu.VMEM((AH, 2 * DV, TQ), F32)],
        compiler_params=_cp(("arbitrary", "arbitrary", "arbitrary")),
        name="attention",
    )(q, k, k, vt, vt)


def _row_in(grow, rows):
    hit = grow == rows[0]
    for r in rows[1:]:
        hit = jnp.logical_or(hit, grow == r)
    return hit


def _mprep_kernel(x_ref, xp_ref, xn_ref, cw_ref, wq_ref, q_ref, k_ref, v_ref):
    i = pl.program_id(0)
    xb = x_ref[...]
    x = xb.astype(F32)
    prev_row = xp_ref[15:16, :].astype(F32)
    next_row = xn_ref[0:1, :].astype(F32)
    row = lax.broadcasted_iota(jnp.int32, (TM, 1), 0)
    grow = row + i * TM
    xprev = jnp.where(row == 0, prev_row, pltpu.roll(x, 1, axis=0))
    xprev = jnp.where(_row_in(grow, SEQ_STARTS), 0.0, xprev)
    xnext = jnp.where(row == TM - 1, next_row, pltpu.roll(x, TM - 1, axis=0))
    xnext = jnp.where(_row_in(grow, SEQ_ENDS), 0.0, xnext)
    cw = cw_ref[0]
    xc = xprev * cw[0:1, :] + x * cw[1:2, :] + xnext * cw[2:3, :]
    xcb = (xc * _sigmoid(xc)).astype(BF16)
    ones = jnp.ones((DH_M, TM), BF16)
    for h in range(H_M):
        sl = slice(h * DH_M, (h + 1) * DH_M)
        q_ref[:, sl] = _dot(xcb[:, sl], wq_ref[0, 0, h]).astype(BF16)
        k_ref[:, sl] = (_dot(xcb[:, sl], wq_ref[0, 1, h]) * DH_M ** -0.5).astype(BF16)
        v_ref[2 * h * DH_M:(2 * h + 1) * DH_M, :] = _dot_nt(wq_ref[0, 2, h], xb[:, sl]).astype(BF16)
        v_ref[(2 * h + 1) * DH_M:(2 * h + 2) * DH_M, :] = ones


def _mprep(p, conv_w, w_qkv, l):
    hb = TM // 16
    xcol = PM_XM // D_M
    return pl.pallas_call(
        _mprep_kernel,
        grid=(NBLK,),
        in_specs=[pl.BlockSpec((TM, D_M), lambda i: (i, xcol)),
                  pl.BlockSpec((16, D_M), lambda i: (jnp.maximum(i * hb - 1, 0), xcol)),
                  pl.BlockSpec((16, D_M), lambda i: (jnp.minimum((i + 1) * hb, R // 16 - 1), xcol)),
                  pl.BlockSpec((1, 3, D_M), lambda i: (l, 0, 0)),
                  pl.BlockSpec((1, 3, H_M, DH_M, DH_M), lambda i: (l, 0, 0, 0, 0))],
        out_specs=[pl.BlockSpec((TM, D_M), lambda i: (i, 0)),
                   pl.BlockSpec((TM, D_M), lambda i: (i, 0)),
                   pl.BlockSpec((2 * D_M, TM), lambda i: (0, i))],
        out_shape=[jax.ShapeDtypeStruct((R, D_M), BF16),
                   jax.ShapeDtypeStruct((R, D_M), BF16),
                   jax.ShapeDtypeStruct((2 * D_M, R), BF16)],
        compiler_params=_cp(("arbitrary",)),
        name="mlstm_prep",
    )(p, p, p, conv_w, w_qkv)


def _split3(x):
    h = x.astype(BF16)
    r = x - h.astype(F32)
    m = r.astype(BF16)
    l = (r - m.astype(F32)).astype(BF16)
    return h, m, l


def _mlstm_kernel(q_ref, k_ref, v_ref, g_ref, gt_ref, b_ref, bt_ref, h_ref, c_ref, m_ref):
    d = pl.program_id(1)
    s = pl.program_id(2)

    @pl.when(s == 0)
    def _():
        c_ref[...] = jnp.zeros_like(c_ref)
        m_ref[...] = jnp.zeros_like(m_ref)

    sign = 1 - 2 * d
    r = lax.broadcasted_iota(jnp.int32, (LC, LC), 0)
    c = lax.broadcasted_iota(jnp.int32, (LC, LC), 1)
    before = (c - r) * sign >= 0
    tri = jnp.where((r - c) * sign >= 0, 1.0, 0.0).astype(BF16)
    g = g_ref[0] + b_ref[0]
    gt = gt_ref[0] + bt_ref[0]
    ig_c = g[:, 0:H_M] * LOG2E
    lf_c = _log_sigmoid(g[:, H_M:2 * H_M]) * LOG2E
    lf_r = _log_sigmoid(gt[H_M:2 * H_M, :]) * LOG2E
    b_c = sum(_dot(tri, part) for part in _split3(lf_c))
    b_r = sum(_dot_nt(part, tri) for part in _split3(lf_r))
    tot = jnp.sum(lf_c, axis=0, keepdims=True)
    src = ig_c - b_c

    for h in range(H_M):
        sl = slice(h * DH_M, (h + 1) * DH_M)
        qh = q_ref[:, sl]
        kh = k_ref[:, sl]
        vth = v_ref[2 * h * DH_M:(2 * h + 2) * DH_M, :]
        bt = b_r[h:h + 1, :]
        tt = tot[:, h:h + 1]
        mp = m_ref[h:h + 1, 0:1]
        dm = jnp.where(before, bt + src[:, h:h + 1], -jnp.inf)
        m_inter = bt + mp
        m_t = jnp.maximum(jnp.max(dm, axis=0, keepdims=True), m_inter)
        sc = _dot_nt(kh, qh) * jnp.exp2(dm - m_t)
        inter = jnp.exp2(m_inter - m_t)
        st = c_ref[h]
        both = _dot(vth, sc.astype(BF16)) + inter * _dot_nt(st.astype(BF16), qh)
        den = both[DH_M:DH_M + 1, :]
        ht = both[0:DH_M, :] / jnp.maximum(jnp.abs(den), jnp.exp2(-m_t))
        h_ref[0, :, sl] = jnp.transpose(ht)
        gg = tt + src[:, h:h + 1]
        m_new = jnp.maximum(tt + mp, jnp.max(gg, axis=0, keepdims=True))
        kw = (kh.astype(F32) * jnp.exp2(gg - m_new)).astype(BF16)
        c_ref[h] = jnp.exp2(tt + mp - m_new) * st + _dot(vth, kw)
        m_ref[h:h + 1, :] = jnp.broadcast_to(m_new, (1, DH_M))


def _mlstm(q, k, v, gd, gtd, bd, btd):
    nl = T // LC
    nsteps = nl + TC // LC

    def rb(b, d, s):
        lat = b * nl + jnp.where(d == 0, s - 1, nl - s)
        return jnp.where(s == 0, NL // LC + b, lat)

    qspec = pl.BlockSpec((LC, D_M), lambda b, d, s: (rb(b, d, s), 0))
    return pl.pallas_call(
        _mlstm_kernel,
        grid=(NB_, 2, nsteps),
        in_specs=[qspec, qspec,
                  pl.BlockSpec((2 * D_M, LC), lambda b, d, s: (0, rb(b, d, s))),
                  pl.BlockSpec((1, LC, 16), lambda b, d, s: (d, rb(b, d, s), 0)),
                  pl.BlockSpec((1, 16, LC), lambda b, d, s: (d, 0, rb(b, d, s))),
                  pl.BlockSpec((1, 1, 16), lambda b, d, s: (d, 0, 0)),
                  pl.BlockSpec((1, 16, 1), lambda b, d, s: (d, 0, 0))],
        out_specs=pl.BlockSpec((1, LC, D_M), lambda b, d, s: (d, rb(b, d, s), 0)),
        out_shape=jax.ShapeDtypeStruct((2, R, D_M), F32),
        scratch_shapes=[pltpu.VMEM((H_M, 2 * DH_M, DH_M), F32),
                        pltpu.VMEM((H_M, DH_M), F32)],
        compiler_params=_cp(("arbitrary", "arbitrary", "arbitrary")),
        name="mlstm_scan",
    )(q, k, v, gd, gtd, bd, btd)


S5_NLAT = T // SL
S5_NCTX = TC // SL
S5_SEG = S5_NLAT // NSEG


def _fold_copy(tok_ref, fold_ref, to_fold):
    def move(tok_idx, fold_rows, t):
        lanes = slice(t * 128, (t + 1) * 128)
        if to_fold:
            fold_ref[fold_rows, lanes] = tok_ref[tok_idx, :]
        else:
            tok_ref[tok_idx, :] = fold_ref[fold_rows, lanes]

    def body(j, carry):
        for b in range(NB_):
            rows = pl.ds(pl.multiple_of(b * S5_NLAT + j * NSEG, NSEG), NSEG)
            for t in range(SL):
                move(pl.ds(b * T + j * SL + t, NSEG, stride=S5_SEG * SL), rows, t)
        return carry

    lax.fori_loop(0, S5_SEG, body, 0)
    for t in range(SL):
        move(pl.ds(NL + t, NCX // SL, stride=SL), slice(NL // SL, NCH), t)


def _expand_block_diag(compact_ref, dense_ref, key_shift, key_mul, key_mask, row_gshift, col_gshift):
    comp = compact_ref[0]
    nrows, kc = comp.shape
    chunk = 512
    r = lax.broadcasted_iota(jnp.int32, (kc, chunk), 0)
    rg = (lax.broadcasted_iota(jnp.int32, (nrows, 1), 0) >> row_gshift) & (SLB - 1)
    for c0 in range(0, dense_ref.shape[1], chunk):
        q = lax.broadcasted_iota(jnp.int32, (kc, chunk), 1) + c0
        sel = jnp.where(r == (q >> key_shift) * key_mul + (q & key_mask), 1.0, 0.0).astype(BF16)
        cg = ((lax.broadcasted_iota(jnp.int32, (1, chunk), 1) + c0) >> col_gshift) & (SLB - 1)
        dense_ref[:, c0:c0 + chunk] = jnp.where(rg == cg, _dot(comp, sel), 0.0).astype(BF16)


def _cmul(ar, ai, sr, si):
    return ar * sr - ai * si, ar * si + ai * sr


def _s5_state_kernel(us_ref, wst_ref, a_ref, s_ref, e_ref, x_ref, w_ref):
    _fold_copy(us_ref.at[0], x_ref, True)
    _expand_block_diag(wst_ref, w_ref, 9, P_S, P_S - 1, 4, 6)
    e_ref[...] = _dot(x_ref[...].astype(BF16), w_ref[...])
    apow = a_ref[0]
    step_a = ((apow[0:1, :], apow[1:2, :]), (apow[2:3, :], apow[3:4, :]))
    nctx, seg = S5_NCTX, S5_SEG

    def advance(rows, d, st, write):
        cr = slice(2 * d * SW, (2 * d + 1) * SW)
        ci = slice((2 * d + 1) * SW, (2 * d + 2) * SW)
        dr = e_ref[rows, cr]
        di = e_ref[rows, ci]
        if write:
            e_ref[rows, cr] = st[0]
            e_ref[rows, ci] = st[1]
        nr, ni = _cmul(step_a[d][0], step_a[d][1], st[0], st[1])
        return nr + dr, ni + di

    def sweep(nsteps, rows_of, init, write):
        def body(kk, sts):
            out = []
            for b in range(NB_):
                out.append(advance(rows_of(b, kk), 0, sts[2 * b], write))
                out.append(advance(rows_of(b, nsteps - 1 - kk), 1, sts[2 * b + 1], write))
            return tuple(out)
        return lax.fori_loop(0, nsteps, body, init)

    z1 = jnp.zeros((1, SW), F32)
    carry = sweep(nctx, lambda b, kk: pl.ds(NL // SL + b * nctx + kk, 1), ((z1, z1),) * (2 * NB_), True)

    seg_rows = lambda b, kk: pl.ds(pl.multiple_of(b * S5_NLAT + kk * NSEG, NSEG), NSEG)
    z8 = jnp.zeros((NSEG, SW), F32)
    ends = sweep(seg, seg_rows, ((z8, z8),) * (2 * NB_), False)
    inits = []
    for ch in range(2 * NB_):
        d = ch % 2
        ar, ai = step_a[d]
        for _ in range(int(math.log2(seg))):
            ar, ai = ar * ar - ai * ai, 2.0 * ar * ai
        cur = carry[ch]
        rows_r, rows_i = [None] * NSEG, [None] * NSEG
        for kseg in (range(NSEG) if d == 0 else range(NSEG - 1, -1, -1)):
            rows_r[kseg], rows_i[kseg] = cur
            nr, ni = _cmul(ar, ai, cur[0], cur[1])
            cur = (nr + ends[ch][0][kseg:kseg + 1, :], ni + ends[ch][1][kseg:kseg + 1, :])
        inits.append((jnp.concatenate(rows_r, axis=0), jnp.concatenate(rows_i, axis=0)))
    sweep(seg, seg_rows, tuple(inits), True)
    s_ref[0] = e_ref[...].astype(BF16)


def _s5_state(us, wst, apow, l):
    return pl.pallas_call(
        _s5_state_kernel,
        grid=(NLB,),
        in_specs=[pl.BlockSpec((1, R, 128), lambda g: (g, 0, 0), pipeline_mode=pl.Buffered(1)),
                  pl.BlockSpec((1, SL * 128, 4 * P_S), lambda g: (l * NLB + g, 0, 0)),
                  pl.BlockSpec((1, 8, SW), lambda g: (l * NLB + g, 0, 0))],
        out_specs=pl.BlockSpec((1, NCH, 4 * SW), lambda g: (g, 0, 0)),
        out_shape=jax.ShapeDtypeStruct((NLB, NCH, 4 * SW), BF16),
        scratch_shapes=[pltpu.VMEM((NCH, 4 * SW), F32),
                        pltpu.VMEM((NCH, SL * 128), F32),
                        pltpu.VMEM((SL * 128, 4 * SW), BF16)],
        compiler_params=_cp(("arbitrary",)),
        name="s5_state",
    )(us, wst, apow)


def _s5_out_kernel(us_ref, toep_ref, s_ref, wo_ref, d_ref, y_ref, x_ref, tw_ref, ow_ref):
    _fold_copy(us_ref.at[0], x_ref, True)
    _expand_block_diag(toep_ref, tw_ref, 7, GROUP, GROUP - 1, 4, 4)
    _expand_block_diag(wo_ref, ow_ref, 7, GROUP, GROUP - 1, 6, 4)
    x = x_ref[...]
    y = _dot(x.astype(BF16), tw_ref[...]) + _dot(s_ref[0], ow_ref[...]) + d_ref[0] * x
    x_ref[...] = jax.nn.gelu(y)
    _fold_copy(y_ref.at[0], x_ref, False)


def _s5_out(us, toep, s, wout, dflat, l):
    return pl.pallas_call(
        _s5_out_kernel,
        grid=(NLB,),
        in_specs=[pl.BlockSpec((1, R, 128), lambda g: (g, 0, 0), pipeline_mode=pl.Buffered(1)),
                  pl.BlockSpec((1, SL * 128, SL * GROUP), lambda g: (l * NLB + g, 0, 0)),
                  pl.BlockSpec((1, NCH, 4 * SW), lambda g: (g, 0, 0), pipeline_mode=pl.Buffered(1)),
                  pl.BlockSpec((1, 4 * SW, SL * GROUP), lambda g: (l * NLB + g, 0, 0)),
                  pl.BlockSpec((1, 1, SL * 128), lambda g: (l * NLB + g, 0, 0))],
        out_specs=pl.BlockSpec((1, R, 128), lambda g: (g, 0, 0)),
        out_shape=jax.ShapeDtypeStruct((NLB, R, 128), F32),
        scratch_shapes=[pltpu.VMEM((NCH, SL * 128), F32),
                        pltpu.VMEM((SL * 128, SL * 128), BF16),
                        pltpu.VMEM((4 * SW, SL * 128), BF16)],
        compiler_params=_cp(("arbitrary",)),
        name="s5_out",
    )(us, toep, s, wout, dflat)


def _glu_kernel(y_ref, w_ref, o_ref):
    y = jnp.concatenate([y_ref[cb] for cb in range(NLB)], axis=1).astype(BF16)
    z = _dot(y, w_ref[0])
    o_ref[...] = (z[:, :D_S] * _sigmoid(z[:, D_S:])).astype(BF16)


def _glu(y, w, l):
    return pl.pallas_call(
        _glu_kernel,
        grid=(NBLK,),
        in_specs=[pl.BlockSpec((NLB, TM, 128), lambda i: (0, i, 0)),
                  pl.BlockSpec((1, D_S, 2 * D_S), lambda i: (l, 0, 0))],
        out_specs=pl.BlockSpec((TM, D_S), lambda i: (i, 0)),
        out_shape=jax.ShapeDtypeStruct((R, D_S), BF16),
        compiler_params=_cp(("arbitrary",)),
        name="s5_glu",
    )(y, w)


def _s5_params(a_re, a_im, log_dt, b_re, b_im, c_re, c_im, dskip):
    dt = jnp.exp(log_dt)[:, :, None]
    lam_r, lam_i = a_re * dt, a_im * dt
    mag = jnp.exp(lam_r)
    ar, ai = mag * jnp.cos(lam_i), mag * jnp.sin(lam_i)
    den = a_re * a_re + a_im * a_im
    nr, ni = ar - 1.0, ai
    cr = (nr * a_re + ni * a_im) / den
    ci = (ni * a_re - nr * a_im) / den
    cpr = c_re[None] * cr[:, :, None, :] - c_im[None] * ci[:, :, None, :]
    cpi = c_re[None] * ci[:, :, None, :] + c_im[None] * cr[:, :, None, :]
    j = jnp.arange(SL + 1, dtype=F32)[:, None, None, None]
    pm = jnp.exp(lam_r[None] * j)
    pr, pi = pm * jnp.cos(lam_i[None] * j), pm * jnp.sin(lam_i[None] * j)
    abr = pr[..., None] * b_re[None, None] - pi[..., None] * b_im[None, None]
    abi = pr[..., None] * b_im[None, None] + pi[..., None] * b_re[None, None]
    kern = (jnp.einsum('dgcp,jdgpe->jdgce', cpr, abr[:SL]) - jnp.einsum('dgcp,jdgpe->jdgce', cpi, abi[:SL]))
    zero = jnp.zeros_like(kern[0, 0])
    rows = []
    for t_in in range(SL):
        blks = []
        for t_out in range(SL):
            blk = zero
            if t_out >= t_in:
                blk = blk + kern[t_out - t_in, 0]
            if t_in >= t_out:
                blk = blk + kern[t_in - t_out, 1]
            blks.append(blk)
        rows.append(jnp.stack(blks, axis=0))
    t5 = jnp.stack(rows, axis=0).reshape(SL, SL, NLB, SLB, GROUP, GROUP)
    toep = t5.transpose(2, 0, 3, 5, 1, 4).reshape(NLB, SL * 128, SL * GROUP)
    rev = lambda a, lo: jnp.stack([a[lo + SL - 1 - t] for t in range(SL)], axis=0)
    ws = jnp.stack([rev(abr[:, 0], 0), rev(abi[:, 0], 0), abr[:SL, 1], abi[:SL, 1]], axis=0)
    ws = ws.reshape(4, SL, NLB, SLB, P_S, GROUP)
    wst = ws.transpose(2, 1, 3, 5, 0, 4).reshape(NLB, SL * 128, 4 * P_S)
    def readout(pw_r, pw_i, d):
        re = cpr[d][None] * pw_r[:, :, None, :] - cpi[d][None] * pw_i[:, :, None, :]
        im = cpr[d][None] * pw_i[:, :, None, :] + cpi[d][None] * pw_r[:, :, None, :]
        return re, -im
    of_re, of_im = readout(pr[1:SL + 1, 0], pi[1:SL + 1, 0], 0)
    ob_re, ob_im = readout(rev(pr[:, 1], 1), rev(pi[:, 1], 1), 1)
    wo = jnp.stack([of_re, of_im, ob_re, ob_im], axis=0).reshape(4, SL, NLB, SLB, GROUP, P_S)
    wout = wo.transpose(2, 0, 3, 5, 1, 4).reshape(NLB, 4 * SW, SL * GROUP)
    blk = lambda a: a.reshape(NLB, 1, SW)
    apow = jnp.concatenate([blk(pr[SL, 0]), blk(pi[SL, 0]), blk(pr[SL, 1]), blk(pi[SL, 1]),
                            jnp.zeros((NLB, 4, SW), F32)], axis=1)
    dflat = jnp.tile(dskip.reshape(NLB, 1, 128), (1, 1, SL))
    return toep.astype(BF16), wst.astype(BF16), wout.astype(BF16), apow, dflat


def _mix_out_kernel(a_ref, hf_ref, hb_ref, om_ref, gh_ref, s_ref, ga_ref, gm_ref, gs_ref,
                    wa_ref, wm_ref, ws_ref, wo_ref, x_ref, gt_ref, gf_ref, sh_ref, sc_ref, x1_ref, h2_ref):
    hsum = hf_ref[0] + hb_ref[0]
    parts = []
    for h in range(H_M):
        xh = hsum[:, h * DH_M:(h + 1) * DH_M]
        parts.append(xh * lax.rsqrt(jnp.mean(xh * xh, axis=-1, keepdims=True) + EPS))
    hn = jnp.concatenate(parts, axis=1) * gh_ref[0]
    m = (_sigmoid(om_ref[...].astype(F32)) * hn).astype(BF16)
    t = (_sigmoid(ga_ref[...].astype(F32)) * _dot(a_ref[...], wa_ref[0])
         + _sigmoid(gm_ref[...].astype(F32)) * _dot(m, wm_ref[0])
         + _sigmoid(gs_ref[...].astype(F32)) * _dot(s_ref[...], ws_ref[0]))
    x1 = x_ref[...] + gt_ref[0] * _dot(t.astype(BF16), wo_ref[0])
    x1_ref[...] = x1
    h2_ref[...] = (_rms(x1, gf_ref[0]) * (1.0 + sc_ref[0]) + sh_ref[0]).astype(BF16)


def _mix_out(a, hdir, p, gh, s, wa, wm, ws, wo, x, modrb, g_ffn, l, nrows):
    tm = 256
    per = TM // tm
    gcol = PM_GATES // D
    row = lambda w, c: pl.BlockSpec((tm, w), lambda i: (i, c))
    once = pl.Buffered(1)
    wspec = pl.BlockSpec((1, 1024, D), lambda i: (l, 0, 0), pipeline_mode=once)
    mod = lambda c: pl.BlockSpec((1, 1, D), lambda i: (i // per, 0, c))
    return pl.pallas_call(
        _mix_out_kernel,
        grid=(nrows // tm,),
        in_specs=[row(1024, 0),
                  pl.BlockSpec((1, tm, D_M), lambda i: (0, i, 0)),
                  pl.BlockSpec((1, tm, D_M), lambda i: (1, i, 0)),
                  row(1024, PM_OM // 1024),
                  pl.BlockSpec((1, 1, D_M), lambda i: (l, 0, 0)),
                  row(1024, 0),
                  row(D, gcol), row(D, gcol + 1), row(D, gcol + 2),
                  wspec, wspec, wspec,
                  pl.BlockSpec((1, D, D), lambda i: (l, 0, 0), pipeline_mode=once),
                  row(D, 0), mod(2),
                  pl.BlockSpec((1, 1, D), lambda i: (l, 0, 0)), mod(3), mod(4)],
        out_specs=[row(D, 0), row(D, 0)],
        out_shape=[jax.ShapeDtypeStruct((nrows, D), F32), jax.ShapeDtypeStruct((nrows, D), BF16)],
        compiler_params=_cp(("arbitrary",)),
        name="mix_out",
    )(a, hdir, hdir, p, gh, s, p, p, p, wa, wm, ws, wo, x, modrb, g_ffn, modrb, modrb)


def _resid_kernel(t_ref, w_ref, x_ref, gt_ref, o_ref, wb_ref):
    @pl.when(pl.program_id(1) == 0)
    def _():
        wb_ref[...] = w_ref[0].astype(BF16)

    o_ref[...] = x_ref[...] + gt_ref[0] * _dot(t_ref[...], wb_ref[...])


def _resid(t, w, x, modrb, gate_chunk, tn, l, nrows, name):
    kdim = t.shape[1]
    nj = D // tn
    return pl.pallas_call(
        _resid_kernel,
        grid=(nj, nrows // TM),
        in_specs=[pl.BlockSpec((TM, kdim), lambda j, i: (i, 0)),
                  pl.BlockSpec((1, kdim, tn), lambda j, i: (l, 0, j)),
                  pl.BlockSpec((TM, tn), lambda j, i: (i, j)),
                  pl.BlockSpec((1, 1, tn), lambda j, i: (i, 0, gate_chunk * nj + j))],
        out_specs=pl.BlockSpec((TM, tn), lambda j, i: (i, j)),
        out_shape=jax.ShapeDtypeStruct((nrows, D), F32),
        scratch_shapes=[pltpu.VMEM((kdim, tn), BF16)],
        compiler_params=_cp(("arbitrary", "arbitrary")),
        name=name,
    )(t, w, x, modrb)


def _ffn_in_kernel(h_ref, wa_ref, wb_ref, o_ref, was_ref, wbs_ref):
    @pl.when(pl.program_id(1) == 0)
    def _():
        was_ref[...] = wa_ref[0].astype(BF16)
        wbs_ref[...] = wb_ref[0].astype(BF16)

    h = h_ref[...]
    a = _dot(h, was_ref[...])
    b = _dot(h, wbs_ref[...])
    o_ref[...] = (a * _sigmoid(a) * b).astype(BF16)


def _ffn_in(h, w, l, nrows):
    tn = 512
    nj = D_FF // tn
    return pl.pallas_call(
        _ffn_in_kernel,
        grid=(nj, nrows // TM),
        in_specs=[pl.BlockSpec((TM, D), lambda j, i: (i, 0)),
                  pl.BlockSpec((1, D, tn), lambda j, i: (l, 0, j)),
                  pl.BlockSpec((1, D, tn), lambda j, i: (l, 0, nj + j))],
        out_specs=pl.BlockSpec((TM, tn), lambda j, i: (i, j)),
        out_shape=jax.ShapeDtypeStruct((nrows, D_FF), BF16),
        scratch_shapes=[pltpu.VMEM((D, tn), BF16), pltpu.VMEM((D, tn), BF16)],
        compiler_params=_cp(("arbitrary", "arbitrary")),
        name="ffn_in",
    )(h, w, w)


def _final_norm_kernel(x_ref, g_ref, o_ref):
    o_ref[...] = _rms(x_ref[...], g_ref[...])


def _final_norm(x, g):
    return pl.pallas_call(
        _final_norm_kernel,
        grid=(NL // TM,),
        in_specs=[pl.BlockSpec((TM, D), lambda i: (i, 0)),
                  pl.BlockSpec((1, D), lambda i: (0, 0))],
        out_specs=pl.BlockSpec((TM, D), lambda i: (i, 0)),
        out_shape=jax.ShapeDtypeStruct((NL, D), F32),
        compiler_params=_cp(("arbitrary",)),
        name="final_norm",
    )(x, g.reshape(1, D))


def _rope_tables():
    f32 = np.float32
    rows = T // GRID_W
    rr, cc = np.meshgrid(np.arange(rows, dtype=f32), np.arange(GRID_W, dtype=f32), indexing='ij')
    rr, cc = rr.reshape(-1), cc.reshape(-1)
    half = ROPE // 2
    inv = (f32(1.0) / (f32(ROPE_THETA) ** (np.arange(0, half, 2, dtype=f32) / f32(half)))).astype(f32)
    ang = np.stack([rr[:, None] * inv, cc[:, None] * inv], axis=1).astype(f32)
    cos = np.cos(ang).astype(f32)
    sin = np.sin(ang).astype(f32)
    cos_f = np.stack([cos, cos], axis=2).reshape(T, ROPE)
    sin_f = np.stack([-sin, sin], axis=2).reshape(T, ROPE)
    pad = lambda a: np.concatenate([a, np.zeros((a.shape[0], 128 - ROPE), f32)], axis=1)
    cos_l, sin_l = pad(cos_f), pad(sin_f)
    cos_c = pad(np.ones((NCX, ROPE), f32))
    sin_c = np.zeros((NCX, 128), f32)
    return (jnp.asarray(np.concatenate([cos_l] * NB_ + [cos_c], axis=0)),
            jnp.asarray(np.concatenate([sin_l] * NB_ + [sin_c], axis=0)))


def _rope_partner(w):
    s = w.shape[:-1]
    w4 = w.reshape(s + (2, 2, ROPE // 4))
    return jnp.concatenate([w4[..., 1:2, :], w4[..., 0:1, :]], axis=-2).reshape(s + (ROPE,))


def _cols(w, name):
    o, n = OFF[name]
    return w[..., o:o + n]


def _prep_weights(w_in, w_uq, w_ukv):
    depth = w_in.shape[0]
    kr = _cols(w_in, 'krope')
    z64 = jnp.zeros((depth, D, 64), F32)
    wsmall = jnp.concatenate([_cols(w_in, 'us'), kr, z64, _rope_partner(kr), z64, _cols(w_in, 'gm'),
                              jnp.zeros((depth, D, 128 - 4 * H_M), F32)], axis=-1).astype(BF16)
    uq = w_uq.reshape(depth, Q_LORA, H_A, NOPE + ROPE)
    wqn = uq[..., :NOPE].reshape(depth, Q_LORA, H_A * NOPE)
    qr = uq[..., NOPE:]
    zq = jnp.zeros_like(qr)
    wqr = jnp.concatenate([qr, zq], axis=-1).reshape(depth, Q_LORA, H_A * 128)
    wqp = jnp.concatenate([_rope_partner(qr), zq], axis=-1).reshape(depth, Q_LORA, H_A * 128)
    ukv = w_ukv.reshape(depth, KV_LORA, H_A, NOPE + DV)
    wkn = ukv[..., :NOPE].reshape(depth, KV_LORA, H_A * NOPE)
    wvt = ukv[..., NOPE:].reshape(depth, KV_LORA, H_A * DV).transpose(0, 2, 1)
    return wsmall, tuple(a.astype(BF16) for a in (wqn, wqr, wqp, wkn, wvt))


def kernel(x, c, ctx, c_ctx, w_ada, b_ada, g_mix, g_ffn, w_in, g_cq, w_uq, g_ckv, w_ukv, conv_m, w_qkv_m, b_gate_m, g_h_m, s5_a_re, s5_a_im, s5_log_dt, s5_b_re, s5_b_im, s5_c_re, s5_c_im, s5_d, w_glu, w_br_a, w_br_m, w_br_s, w_out, w_ffn_in, w_ffn_out, g_final):
    depth = w_ada.shape[0]
    xs = jnp.concatenate([x.reshape(NL, D), ctx.reshape(NCX, D)], axis=0)
    cvec = jnp.concatenate([c, c_ctx[None], jnp.zeros((8 - NB_ - 1, D), F32)], axis=0)
    mod = _ada(cvec, w_ada, b_ada)
    blocks_per_batch = T // TM
    cosk, sink = _rope_tables()
    wsmall, mla_w = _prep_weights(w_in, w_uq, w_ukv)
    toep, wst, wout, apow, dflat = (a.reshape((depth * NLB,) + a.shape[2:]) for a in jax.vmap(_s5_params)(
        s5_a_re, s5_a_im, s5_log_dt, s5_b_re, s5_b_im, s5_c_re, s5_c_im, s5_d))
    w_out_b = w_out.astype(BF16)
    w_qkv_b = jnp.concatenate([w_qkv_m[:, :2], jnp.swapaxes(w_qkv_m[:, 2:], -1, -2)], axis=1).astype(BF16)
    w_glu_b = w_glu.astype(BF16)
    w_br_a_b, w_br_m_b, w_br_s_b = w_br_a.astype(BF16), w_br_m.astype(BF16), w_br_s.astype(BF16)
    gain = lambda g: g.reshape(depth, 1, g.shape[-1])
    g_mix, g_ffn, g_cq, g_ckv, g_h_m = gain(g_mix), gain(g_ffn), gain(g_cq), gain(g_ckv), gain(g_h_m)

    for l in range(depth):
        modrb = jnp.concatenate(
            [jnp.broadcast_to(mod[l, b:b + 1], (blocks_per_batch, 6 * D)) for b in range(NB_)]
            + [jnp.broadcast_to(mod[l, NB_:NB_ + 1], (NCX // TM, 6 * D))], axis=0).reshape(NBLK, 1, 6 * D)
        p = _inproj(xs, g_mix, modrb, w_in, l)
        us, psm = _inproj_side(xs, g_mix, modrb, wsmall, l)

        last = l == depth - 1
        nrows = NL if last else R
        q, k, vt = _mla_proj(p, psm, cosk, sink, g_cq, g_ckv, *mla_w, l)
        a = _attention(q, k, vt, not last)

        qm, km, vm = _mprep(p, conv_m, w_qkv_b, l)
        gm = psm[:, SM_GM:SM_GM + 4 * H_M]
        gd = jnp.stack([gm[:, :2 * H_M], gm[:, 2 * H_M:]], axis=0)
        bd = b_gate_m[l].reshape(2, 1, 2 * H_M)
        hdir = _mlstm(qm, km, vm, gd, gd.transpose(0, 2, 1), bd, bd.transpose(0, 2, 1))

        st = _s5_state(us, wst, apow, l)
        y = _s5_out(us, toep, st, wout, dflat, l)
        s = _glu(y, w_glu_b, l)

        x1, h2 = _mix_out(a, hdir, p, g_h_m, s, w_br_a_b, w_br_m_b, w_br_s_b, w_out_b, xs, modrb, g_ffn, l, nrows)
        u = _ffn_in(h2, w_ffn_in, l, nrows)
        xs = _resid(u, w_ffn_out, x1, modrb, 5, 512, l, nrows, "resid_ffn")

    return _final_norm(xs, g_final).reshape(NB_, T, D)
```

```python
import functools
import math

import numpy as np
import jax
import jax.numpy as jnp
from jax import lax
from jax.experimental import pallas as pl
from jax.experimental.pallas import tpu as pltpu

F32 = jnp.float32
BF16 = jnp.bfloat16

D = 2048
NB_ = 2
T = 4096
TC = 256
GRID_W = 64
EPS = 1e-6
H_A, Q_LORA, KV_LORA, NOPE, ROPE, DV = 8, 512, 512, 128, 64, 128
ROPE_THETA = 10000.0
ATTN_SCALE = (NOPE + ROPE) ** -0.5
H_M, DH_M = 8, 128
D_M = H_M * DH_M
D_S, GROUP, P_S = 1024, 16, 64
G_S = D_S // GROUP
D_FF = ((8 * D // 3 + 255) // 256) * 256
OFF = {}
_o = 0
for _n, _w in (('cq', Q_LORA), ('ckv', KV_LORA), ('krope', ROPE), ('xm', D_M), ('om', D_M),
               ('gm', 4 * H_M), ('us', D_S), ('gates', 3 * D)):
    OFF[_n] = (_o, _w)
    _o += _w

NL = NB_ * T
NCX = NB_ * TC
R = NL + NCX
TM = 512
NBLK = R // TM
LC = 256
TQ = 256
TK = 256
AH = 4
SL = 8
NCH = R // SL
NSEG = 8
SLB = 128 // GROUP
NLB = D_S // 128
SW = SLB * P_S
LOG2E = math.log2(math.e)
PM_CQ, PM_CKV, PM_XM, PM_GATES, PM_OM, PM_W = 0, 512, 1024, 2048, 2048 + 3 * D, 3072 + 3 * D
SM_KR, SM_KP, SM_GM, SM_W = 0, 128, 256, 384
VMEM_LIMIT = 56 * 1024 * 1024
SEQ_STARTS = tuple(b * T for b in range(NB_)) + tuple(NL + b * TC for b in range(NB_))
SEQ_ENDS = tuple(b * T + T - 1 for b in range(NB_)) + tuple(NL + b * TC + TC - 1 for b in range(NB_))


def _cp(sem):
    return pltpu.CompilerParams(dimension_semantics=sem, vmem_limit_bytes=VMEM_LIMIT)


def _dot(a, b):
    return jnp.dot(a, b, preferred_element_type=F32)


def _dot_nt(a, b):
    return lax.dot_general(a, b, (((1,), (1,)), ((), ())), preferred_element_type=F32)


def _dot_tn(a, b):
    return lax.dot_general(a, b, (((0,), (0,)), ((), ())), preferred_element_type=F32)


def _sigmoid(x):
    return 1.0 / (1.0 + jnp.exp(-x))


def _log_sigmoid(x):
    return jnp.minimum(x, 0.0) - jnp.log(1.0 + jnp.exp(-jnp.abs(x)))


def _rms(x, g):
    xf = x.astype(F32)
    return xf * lax.rsqrt(jnp.mean(xf * xf, axis=-1, keepdims=True) + EPS) * g


def _ada_kernel(c_ref, w_ref, b_ref, o_ref):
    c = c_ref[...]
    s = (c * _sigmoid(c)).astype(BF16)
    o_ref[0] = _dot(s, w_ref[0].astype(BF16)) + b_ref[0]


def _ada(cvec, w_ada, b_ada):
    depth = w_ada.shape[0]
    tn = 1024
    return pl.pallas_call(
        _ada_kernel,
        grid=(depth, 6 * D // tn),
        in_specs=[pl.BlockSpec((8, D), lambda l, j: (0, 0)),
                  pl.BlockSpec((1, D, tn), lambda l, j: (l, 0, j)),
                  pl.BlockSpec((1, 1, tn), lambda l, j: (l, 0, j))],
        out_specs=pl.BlockSpec((1, 8, tn), lambda l, j: (l, 0, j)),
        out_shape=jax.ShapeDtypeStruct((depth, 8, 6 * D), F32),
        compiler_params=_cp(("arbitrary", "arbitrary")),
        name="ada",
    )(cvec, w_ada, b_ada.reshape(depth, 1, 6 * D))


def _modulate(x_ref, g_ref, sh_ref, sc_ref):
    return _rms(x_ref[...], g_ref[0]) * (1.0 + sc_ref[0]) + sh_ref[0]


IP_TN = 1024
IP_STARTS = ((OFF['cq'][0], OFF['xm'][0]) + tuple(OFF['gates'][0] + k * IP_TN for k in range(3 * D // IP_TN))
             + (OFF['om'][0],))
IP_SHIFTS = tuple(s % IP_TN for s in IP_STARTS)
assert all(s < 128 and s % 16 == 0 for s in IP_SHIFTS) and OFF['ckv'][0] == Q_LORA and PM_W == IP_TN * len(IP_STARTS)


def _ip_window(j):
    idx = IP_STARTS[-1] // IP_TN
    for jj in range(len(IP_STARTS) - 2, -1, -1):
        idx = jnp.where(j == jj, IP_STARTS[jj] // IP_TN, idx)
    return idx


def _inproj_kernel(x_ref, g_ref, sh_ref, sc_ref, wm_ref, we_ref, o_ref, wb_ref):
    j = pl.program_id(0)

    @pl.when(pl.program_id(1) == 0)
    def _():
        for shift in sorted(set(IP_SHIFTS)):
            hit = functools.reduce(jnp.logical_or, [j == jj for jj, s in enumerate(IP_SHIFTS) if s == shift])

            @pl.when(hit)
            def _(shift=shift):
                wb_ref[0:IP_TN - shift, :] = wm_ref[0, shift:IP_TN, :].astype(BF16)
                if shift:
                    wb_ref[IP_TN - shift:IP_TN, :] = we_ref[0, 0:shift, :].astype(BF16)

    h = _modulate(x_ref, g_ref, sh_ref, sc_ref).astype(BF16)
    o_ref[...] = _dot_nt(h, wb_ref[...]).astype(o_ref.dtype)


def _inproj(x, g, modrb, w_in_t, l):
    return pl.pallas_call(
        _inproj_kernel,
        grid=(len(IP_STARTS), NBLK),
        in_specs=[pl.BlockSpec((TM, D), lambda j, i: (i, 0)),
                  pl.BlockSpec((1, 1, D), lambda j, i: (l, 0, 0)),
                  pl.BlockSpec((1, 1, D), lambda j, i: (i, 0, 0)),
                  pl.BlockSpec((1, 1, D), lambda j, i: (i, 0, 1)),
                  pl.BlockSpec((1, IP_TN, D), lambda j, i: (l, _ip_window(j), 0)),
                  pl.BlockSpec((1, 128, D), lambda j, i: (l, (_ip_window(j) + 1) * (IP_TN // 128), 0))],
        out_specs=pl.BlockSpec((TM, IP_TN), lambda j, i: (i, j)),
        out_shape=jax.ShapeDtypeStruct((R, PM_W), BF16),
        scratch_shapes=[pltpu.VMEM((IP_TN, D), BF16)],
        compiler_params=_cp(("arbitrary", "arbitrary")),
        name="inproj",
    )(x, g, modrb, modrb, w_in_t, w_in_t)


US0, KR0, GM0 = OFF['us'][0], OFF['krope'][0], OFF['gm'][0]
assert KR0 % ROPE == 0 and GM0 % (4 * H_M) == 0 and US0 % IP_TN < 128 and US0 % 16 == 0 and OFF['us'][1] == IP_TN


def _inproj_side_kernel(x_ref, g_ref, sh_ref, sc_ref, wu_ref, wue_ref, wkr_ref, wgm_ref, us_ref, o_ref, wb_ref):
    @pl.when(pl.program_id(0) == 0)
    def _():
        wb_ref[...] = jnp.zeros_like(wb_ref)
        shift = US0 % IP_TN
        wb_ref[0:D_S - shift, :] = wu_ref[0, shift:IP_TN, :].astype(BF16)
        wb_ref[D_S - shift:D_S, :] = wue_ref[0, 0:shift, :].astype(BF16)
        kr = wkr_ref[0].astype(BF16)
        wb_ref[D_S + SM_KR:D_S + SM_KR + ROPE, :] = kr
        half = ROPE // 4
        for blk in range(ROPE // half):
            src = (blk ^ 1) * half
            wb_ref[D_S + SM_KP + blk * half:D_S + SM_KP + (blk + 1) * half, :] = kr[src:src + half, :]
        wb_ref[D_S + SM_GM:D_S + SM_GM + 4 * H_M, :] = wgm_ref[0].astype(BF16)

    h = _modulate(x_ref, g_ref, sh_ref, sc_ref).astype(BF16)
    res = _dot_nt(h, wb_ref[...])
    for cb in range(NLB):
        us_ref[cb] = res[:, cb * 128:(cb + 1) * 128]
    o_ref[...] = res[:, D_S:]


def _inproj_side(x, g, modrb, w_in_t, l):
    once = pl.Buffered(1)
    wblk = lambda rows, row: pl.BlockSpec((1, rows, D), lambda i: (l, row // rows, 0), pipeline_mode=once)
    return pl.pallas_call(
        _inproj_side_kernel,
        grid=(NBLK,),
        in_specs=[pl.BlockSpec((TM, D), lambda i: (i, 0)),
                  pl.BlockSpec((1, 1, D), lambda i: (l, 0, 0)),
                  pl.BlockSpec((1, 1, D), lambda i: (i, 0, 0)),
                  pl.BlockSpec((1, 1, D), lambda i: (i, 0, 1)),
                  wblk(IP_TN, US0 - US0 % IP_TN), wblk(128, US0 - US0 % IP_TN + IP_TN),
                  wblk(ROPE, KR0), wblk(4 * H_M, GM0)],
        out_specs=[pl.BlockSpec((NLB, TM, 128), lambda i: (0, i, 0)),
                   pl.BlockSpec((TM, SM_W), lambda i: (i, 0))],
        out_shape=[jax.ShapeDtypeStruct((NLB, R, 128), F32),
                   jax.ShapeDtypeStruct((R, SM_W), F32)],
        scratch_shapes=[pltpu.VMEM((D_S + SM_W, D), BF16)],
        compiler_params=_cp(("arbitrary",)),
        name="inproj_side",
    )(x, g, modrb, modrb, w_in_t, w_in_t, w_in_t, w_in_t)


def _mla_proj_kernel(cq_ref, ckv_ref, kr_ref, kp_ref, cos_ref, sin_ref, gq_ref, gkv_ref,
                     wqn_ref, wqr_ref, wqp_ref, wkn_ref, wvt_ref, q_ref, k_ref, vt_ref):
    hq = _rms(cq_ref[...], gq_ref[0]).astype(BF16)
    hk = _rms(ckv_ref[...], gkv_ref[0]).astype(BF16)
    cos = cos_ref[...]
    sin = sin_ref[...]
    qscale = ATTN_SCALE * LOG2E
    qn = _dot(hq, wqn_ref[0]) * qscale
    qr = _dot(hq, wqr_ref[0])
    qp = _dot(hq, wqp_ref[0])
    kn = _dot(hk, wkn_ref[0])
    vt = _dot_nt(wvt_ref[0], hk)
    kr = (kr_ref[...] * cos + kp_ref[...] * sin).astype(BF16)
    ones = jnp.ones((DV, TM), BF16)
    for h in range(H_A):
        lo = slice(h * 256, h * 256 + 128)
        hi = slice(h * 256 + 128, (h + 1) * 256)
        sl = slice(h * 128, (h + 1) * 128)
        q_ref[:, lo] = qn[:, sl].astype(BF16)
        q_ref[:, hi] = ((qr[:, sl] * cos + qp[:, sl] * sin) * qscale).astype(BF16)
        k_ref[:, lo] = kn[:, sl].astype(BF16)
        k_ref[:, hi] = kr
        vt_ref[lo, :] = vt[sl, :].astype(BF16)
        vt_ref[hi, :] = ones


def _mla_proj(p, psm, cosk, sink, gq, gkv, wqn, wqr, wqp, wkn, wvt, l):
    lw = lambda shp: pl.BlockSpec((1,) + shp, lambda i: (l, 0, 0))
    return pl.pallas_call(
        _mla_proj_kernel,
        grid=(NBLK,),
        in_specs=[pl.BlockSpec((TM, 512), lambda i: (i, PM_CQ // 512)),
                  pl.BlockSpec((TM, 512), lambda i: (i, PM_CKV // 512)),
                  pl.BlockSpec((TM, 128), lambda i: (i, SM_KR // 128)),
                  pl.BlockSpec((TM, 128), lambda i: (i, SM_KP // 128)),
                  pl.BlockSpec((TM, 128), lambda i: (i, 0)),
                  pl.BlockSpec((TM, 128), lambda i: (i, 0)),
                  pl.BlockSpec((1, 1, 512), lambda i: (l, 0, 0)),
                  pl.BlockSpec((1, 1, 512), lambda i: (l, 0, 0)),
                  lw((512, 1024)), lw((512, 1024)), lw((512, 1024)), lw((512, 1024)), lw((1024, 512))],
        out_specs=[pl.BlockSpec((TM, 2048), lambda i: (i, 0)),
                   pl.BlockSpec((TM, 2048), lambda i: (i, 0)),
                   pl.BlockSpec((2048, TM), lambda i: (0, i))],
        out_shape=[jax.ShapeDtypeStruct((R, 2048), BF16),
                   jax.ShapeDtypeStruct((R, 2048), BF16),
                   jax.ShapeDtypeStruct((2048, R), BF16)],
        compiler_params=_cp(("arbitrary",)),
        name="mla_proj",
    )(p, p, psm, psm, cosk, sink, gq, gkv, wqn, wqr, wqp, wkn, wvt)


def _attn_kernel(q_ref, kl_ref, kc_ref, vl_ref, vc_ref, o_ref, m_ref, acc_ref):
    qi = pl.program_id(2)
    heads = [slice(hh * 256, (hh + 1) * 256) for hh in range(AH)]
    qs = [q_ref[:, hs] for hs in heads]

    def chunk(hh, k, vt, m, acc):
        st = _dot_nt(k, qs[hh])
        m_new = jnp.maximum(m, jnp.max(st, axis=0, keepdims=True))
        p = jnp.exp2(st - m_new).astype(BF16)
        return m_new, jnp.exp2(m - m_new) * acc + _dot(vt, p)

    for hh, hs in enumerate(heads):
        m, acc = chunk(hh, kc_ref[:, hs], vc_ref[hs, :], jnp.full((1, TQ), -jnp.inf, F32),
                       jnp.zeros((2 * DV, TQ), F32))
        m_ref[hh] = m
        acc_ref[hh] = acc

    @pl.when(qi < T // TQ)
    def _():
        st = [(m_ref[hh], acc_ref[hh]) for hh in range(AH)]
        for j in range(T // TK):
            ks = slice(j * TK, (j + 1) * TK)
            st = [chunk(hh, kl_ref[ks, hs], vl_ref[hs, ks], *st[hh]) for hh, hs in enumerate(heads)]
        for hh in range(AH):
            acc_ref[hh] = st[hh][1]

    for hh in range(AH):
        acc = acc_ref[hh]
        o_ref[:, hh * DV:(hh + 1) * DV] = jnp.transpose(acc[0:DV, :] / acc[DV:2 * DV, :]).astype(o_ref.dtype)


def _attention(q, k, vt, ctx_queries):
    nq = T // TQ
    qrow = lambda b, h, i: jnp.where(i < nq, b * nq + i, NL // TQ + b)
    return pl.pallas_call(
        _attn_kernel,
        grid=(NB_, H_A // AH, nq + 1 if ctx_queries else nq),
        in_specs=[pl.BlockSpec((TQ, AH * 256), lambda b, h, i: (qrow(b, h, i), h)),
                  pl.BlockSpec((T, AH * 256), lambda b, h, i: (b, h)),
                  pl.BlockSpec((TC, AH * 256), lambda b, h, i: (NL // TC + b, h)),
                  pl.BlockSpec((AH * 256, T), lambda b, h, i: (h, b)),
                  pl.BlockSpec((AH * 256, TC), lambda b, h, i: (h, NL // TC + b))],
        out_specs=pl.BlockSpec((TQ, AH * DV), lambda b, h, i: (qrow(b, h, i), h)),
        out_shape=jax.ShapeDtypeStruct((R if ctx_queries else NL, H_A * DV), BF16),
        scratch_shapes=[pltpu.VMEM((AH, 1, TQ), F32), pltpu.VMEM((AH, 2 * DV, TQ), F32)],
        compiler_params=_cp(("arbitrary", "arbitrary", "arbitrary")),
        name="attention",
    )(q, k, k, vt, vt)


def _row_in(grow, rows):
    hit = grow == rows[0]
    for r in rows[1:]:
        hit = jnp.logical_or(hit, grow == r)
    return hit


def _mprep_kernel(x_ref, xp_ref, xn_ref, cw_ref, wq_ref, q_ref, k_ref, v_ref):
    i = pl.program_id(0)
    xb = x_ref[...]
    x = xb.astype(F32)
    prev_row = xp_ref[15:16, :].astype(F32)
    next_row = xn_ref[0:1, :].astype(F32)
    row = lax.broadcasted_iota(jnp.int32, (TM, 1), 0)
    grow = row + i * TM
    xprev = jnp.where(row == 0, prev_row, pltpu.roll(x, 1, axis=0))
    xprev = jnp.where(_row_in(grow, SEQ_STARTS), 0.0, xprev)
    xnext = jnp.where(row == TM - 1, next_row, pltpu.roll(x, TM - 1, axis=0))
    xnext = jnp.where(_row_in(grow, SEQ_ENDS), 0.0, xnext)
    cw = cw_ref[0]
    xc = xprev * cw[0:1, :] + x * cw[1:2, :] + xnext * cw[2:3, :]
    xcb = (xc * _sigmoid(xc)).astype(BF16)
    ones = jnp.ones((DH_M, TM), BF16)
    for h in range(H_M):
        sl = slice(h * DH_M, (h + 1) * DH_M)
        q_ref[:, sl] = _dot(xcb[:, sl], wq_ref[0, 0, h]).astype(BF16)
        k_ref[:, sl] = (_dot(xcb[:, sl], wq_ref[0, 1, h]) * DH_M ** -0.5).astype(BF16)
        v_ref[2 * h * DH_M:(2 * h + 1) * DH_M, :] = _dot_nt(wq_ref[0, 2, h], xb[:, sl]).astype(BF16)
        v_ref[(2 * h + 1) * DH_M:(2 * h + 2) * DH_M, :] = ones


def _mprep(p, conv_w, w_qkv, l):
    hb = TM // 16
    xcol = PM_XM // D_M
    return pl.pallas_call(
        _mprep_kernel,
        grid=(NBLK,),
        in_specs=[pl.BlockSpec((TM, D_M), lambda i: (i, xcol)),
                  pl.BlockSpec((16, D_M), lambda i: (jnp.maximum(i * hb - 1, 0), xcol)),
                  pl.BlockSpec((16, D_M), lambda i: (jnp.minimum((i + 1) * hb, R // 16 - 1), xcol)),
                  pl.BlockSpec((1, 3, D_M), lambda i: (l, 0, 0)),
                  pl.BlockSpec((1, 3, H_M, DH_M, DH_M), lambda i: (l, 0, 0, 0, 0))],
        out_specs=[pl.BlockSpec((TM, D_M), lambda i: (i, 0)),
                   pl.BlockSpec((TM, D_M), lambda i: (i, 0)),
                   pl.BlockSpec((2 * D_M, TM), lambda i: (0, i))],
        out_shape=[jax.ShapeDtypeStruct((R, D_M), BF16),
                   jax.ShapeDtypeStruct((R, D_M), BF16),
                   jax.ShapeDtypeStruct((2 * D_M, R), BF16)],
        compiler_params=_cp(("arbitrary",)),
        name="mlstm_prep",
    )(p, p, p, conv_w, w_qkv)


def _split3(x):
    h = x.astype(BF16)
    r = x - h.astype(F32)
    m = r.astype(BF16)
    l = (r - m.astype(F32)).astype(BF16)
    return h, m, l


def _mlstm_kernel(q_ref, k_ref, v_ref, g_ref, gt_ref, b_ref, bt_ref, h_ref, c_ref, m_ref):
    d = pl.program_id(1)
    s = pl.program_id(2)

    @pl.when(s == 0)
    def _():
        c_ref[...] = jnp.zeros_like(c_ref)
        m_ref[...] = jnp.zeros_like(m_ref)

    sign = 1 - 2 * d
    r = lax.broadcasted_iota(jnp.int32, (LC, LC), 0)
    c = lax.broadcasted_iota(jnp.int32, (LC, LC), 1)
    before = (c - r) * sign >= 0
    tri = jnp.where((r - c) * sign >= 0, 1.0, 0.0).astype(BF16)
    g = g_ref[0] + b_ref[0]
    gt = gt_ref[0] + bt_ref[0]
    ig_c = g[:, 0:H_M] * LOG2E
    lf_c = _log_sigmoid(g[:, H_M:2 * H_M]) * LOG2E
    lf_r = _log_sigmoid(gt[H_M:2 * H_M, :]) * LOG2E
    b_c = sum(_dot(tri, part) for part in _split3(lf_c))
    b_r = sum(_dot_nt(part, tri) for part in _split3(lf_r))
    tot = jnp.sum(lf_c, axis=0, keepdims=True)
    src = ig_c - b_c

    for h in range(H_M):
        sl = slice(h * DH_M, (h + 1) * DH_M)
        qh = q_ref[:, sl]
        kh = k_ref[:, sl]
        vth = v_ref[2 * h * DH_M:(2 * h + 2) * DH_M, :]
        bt = b_r[h:h + 1, :]
        tt = tot[:, h:h + 1]
        mp = m_ref[h:h + 1, 0:1]
        dm = jnp.where(before, bt + src[:, h:h + 1], -jnp.inf)
        m_inter = bt + mp
        m_t = jnp.maximum(jnp.max(dm, axis=0, keepdims=True), m_inter)
        sc = _dot_nt(kh, qh) * jnp.exp2(dm - m_t)
        inter = jnp.exp2(m_inter - m_t)
        st = c_ref[h]
        both = _dot(vth, sc.astype(BF16)) + inter * _dot_nt(st.astype(BF16), qh)
        den = both[DH_M:DH_M + 1, :]
        ht = both[0:DH_M, :] / jnp.maximum(jnp.abs(den), jnp.exp2(-m_t))
        h_ref[0, :, sl] = jnp.transpose(ht)
        gg = tt + src[:, h:h + 1]
        m_new = jnp.maximum(tt + mp, jnp.max(gg, axis=0, keepdims=True))
        kw = (kh.astype(F32) * jnp.exp2(gg - m_new)).astype(BF16)
        c_ref[h] = jnp.exp2(tt + mp - m_new) * st + _dot(vth, kw)
        m_ref[h:h + 1, :] = jnp.broadcast_to(m_new, (1, DH_M))


def _mlstm(q, k, v, gd, gtd, bd, btd):
    nl = T // LC
    nsteps = nl + TC // LC

    def rb(b, d, s):
        lat = b * nl + jnp.where(d == 0, s - 1, nl - s)
        return jnp.where(s == 0, NL // LC + b, lat)

    qspec = pl.BlockSpec((LC, D_M), lambda b, d, s: (rb(b, d, s), 0))
    return pl.pallas_call(
        _mlstm_kernel,
        grid=(NB_, 2, nsteps),
        in_specs=[qspec, qspec,
                  pl.BlockSpec((2 * D_M, LC), lambda b, d, s: (0, rb(b, d, s))),
                  pl.BlockSpec((1, LC, 16), lambda b, d, s: (d, rb(b, d, s), 0)),
                  pl.BlockSpec((1, 16, LC), lambda b, d, s: (d, 0, rb(b, d, s))),
                  pl.BlockSpec((1, 1, 16), lambda b, d, s: (d, 0, 0)),
                  pl.BlockSpec((1, 16, 1), lambda b, d, s: (d, 0, 0))],
        out_specs=pl.BlockSpec((1, LC, D_M), lambda b, d, s: (d, rb(b, d, s), 0)),
        out_shape=jax.ShapeDtypeStruct((2, R, D_M), F32),
        scratch_shapes=[pltpu.VMEM((H_M, 2 * DH_M, DH_M), F32),
                        pltpu.VMEM((H_M, DH_M), F32)],
        compiler_params=_cp(("arbitrary", "arbitrary", "arbitrary")),
        name="mlstm_scan",
    )(q, k, v, gd, gtd, bd, btd)


S5_NLAT = T // SL
S5_NCTX = TC // SL
S5_SEG = S5_NLAT // NSEG


def _fold_copy(tok_ref, fold_ref, to_fold):
    def move(tok_idx, fold_rows, t):
        lanes = slice(t * 128, (t + 1) * 128)
        if to_fold:
            fold_ref[fold_rows, lanes] = tok_ref[tok_idx, :]
        else:
            tok_ref[tok_idx, :] = fold_ref[fold_rows, lanes]

    def body(j, carry):
        for b in range(NB_):
            rows = pl.ds(pl.multiple_of(b * S5_NLAT + j * NSEG, NSEG), NSEG)
            for t in range(SL):
                move(pl.ds(b * T + j * SL + t, NSEG, stride=S5_SEG * SL), rows, t)
        return carry

    lax.fori_loop(0, S5_SEG, body, 0)
    for t in range(SL):
        move(pl.ds(NL + t, NCX // SL, stride=SL), slice(NL // SL, NCH), t)


def _expand_block_diag(compact_ref, dense_ref, key_shift, key_mul, key_mask, row_gshift, col_gshift):
    comp = compact_ref[0]
    nrows, kc = comp.shape
    chunk = 512
    r = lax.broadcasted_iota(jnp.int32, (kc, chunk), 0)
    rg = (lax.broadcasted_iota(jnp.int32, (nrows, 1), 0) >> row_gshift) & (SLB - 1)
    for c0 in range(0, dense_ref.shape[1], chunk):
        q = lax.broadcasted_iota(jnp.int32, (kc, chunk), 1) + c0
        sel = jnp.where(r == (q >> key_shift) * key_mul + (q & key_mask), 1.0, 0.0).astype(BF16)
        cg = ((lax.broadcasted_iota(jnp.int32, (1, chunk), 1) + c0) >> col_gshift) & (SLB - 1)
        dense_ref[:, c0:c0 + chunk] = jnp.where(rg == cg, _dot(comp, sel), 0.0).astype(BF16)


def _cmul(ar, ai, sr, si):
    return ar * sr - ai * si, ar * si + ai * sr


def _s5_state_kernel(us_ref, wst_ref, a_ref, s_ref, e_ref, x_ref, w_ref):
    _fold_copy(us_ref.at[0], x_ref, True)
    _expand_block_diag(wst_ref, w_ref, 9, P_S, P_S - 1, 4, 6)
    e_ref[...] = _dot(x_ref[...].astype(BF16), w_ref[...])
    apow = a_ref[0]
    step_a = ((apow[0:1, :], apow[1:2, :]), (apow[2:3, :], apow[3:4, :]))
    nctx, seg = S5_NCTX, S5_SEG

    def advance(rows, d, st, write):
        cr = slice(2 * d * SW, (2 * d + 1) * SW)
        ci = slice((2 * d + 1) * SW, (2 * d + 2) * SW)
        dr = e_ref[rows, cr]
        di = e_ref[rows, ci]
        if write:
            e_ref[rows, cr] = st[0]
            e_ref[rows, ci] = st[1]
        nr, ni = _cmul(step_a[d][0], step_a[d][1], st[0], st[1])
        return nr + dr, ni + di

    def sweep(nsteps, rows_of, init, write):
        def body(kk, sts):
            out = []
            for b in range(NB_):
                out.append(advance(rows_of(b, kk), 0, sts[2 * b], write))
                out.append(advance(rows_of(b, nsteps - 1 - kk), 1, sts[2 * b + 1], write))
            return tuple(out)
        return lax.fori_loop(0, nsteps, body, init)

    z1 = jnp.zeros((1, SW), F32)
    carry = sweep(nctx, lambda b, kk: pl.ds(NL // SL + b * nctx + kk, 1), ((z1, z1),) * (2 * NB_), True)

    seg_rows = lambda b, kk: pl.ds(pl.multiple_of(b * S5_NLAT + kk * NSEG, NSEG), NSEG)
    z8 = jnp.zeros((NSEG, SW), F32)
    ends = sweep(seg, seg_rows, ((z8, z8),) * (2 * NB_), False)
    inits = []
    for ch in range(2 * NB_):
        d = ch % 2
        ar, ai = step_a[d]
        for _ in range(int(math.log2(seg))):
            ar, ai = ar * ar - ai * ai, 2.0 * ar * ai
        cur = carry[ch]
        rows_r, rows_i = [None] * NSEG, [None] * NSEG
        for kseg in (range(NSEG) if d == 0 else range(NSEG - 1, -1, -1)):
            rows_r[kseg], rows_i[kseg] = cur
            nr, ni = _cmul(ar, ai, cur[0], cur[1])
            cur = (nr + ends[ch][0][kseg:kseg + 1, :], ni + ends[ch][1][kseg:kseg + 1, :])
        inits.append((jnp.concatenate(rows_r, axis=0), jnp.concatenate(rows_i, axis=0)))
    sweep(seg, seg_rows, tuple(inits), True)
    s_ref[0] = e_ref[...].astype(BF16)


def _s5_state(us, wst, apow, l):
    return pl.pallas_call(
        _s5_state_kernel,
        grid=(NLB,),
        in_specs=[pl.BlockSpec((1, R, 128), lambda g: (g, 0, 0), pipeline_mode=pl.Buffered(1)),
                  pl.BlockSpec((1, SL * 128, 4 * P_S), lambda g: (l * NLB + g, 0, 0)),
                  pl.BlockSpec((1, 8, SW), lambda g: (l * NLB + g, 0, 0))],
        out_specs=pl.BlockSpec((1, NCH, 4 * SW), lambda g: (g, 0, 0)),
        out_shape=jax.ShapeDtypeStruct((NLB, NCH, 4 * SW), BF16),
        scratch_shapes=[pltpu.VMEM((NCH, 4 * SW), F32),
                        pltpu.VMEM((NCH, SL * 128), F32),
                        pltpu.VMEM((SL * 128, 4 * SW), BF16)],
        compiler_params=_cp(("arbitrary",)),
        name="s5_state",
    )(us, wst, apow)


def _s5_out_kernel(us_ref, toep_ref, s_ref, wo_ref, d_ref, y_ref, x_ref, tw_ref, ow_ref):
    _fold_copy(us_ref.at[0], x_ref, True)
    _expand_block_diag(toep_ref, tw_ref, 7, GROUP, GROUP - 1, 4, 4)
    _expand_block_diag(wo_ref, ow_ref, 7, GROUP, GROUP - 1, 6, 4)
    x = x_ref[...]
    y = _dot(x.astype(BF16), tw_ref[...]) + _dot(s_ref[0], ow_ref[...]) + d_ref[0] * x
    x_ref[...] = jax.nn.gelu(y)
    _fold_copy(y_ref.at[0], x_ref, False)


def _s5_out(us, toep, s, wout, dflat, l):
    return pl.pallas_call(
        _s5_out_kernel,
        grid=(NLB,),
        in_specs=[pl.BlockSpec((1, R, 128), lambda g: (g, 0, 0), pipeline_mode=pl.Buffered(1)),
                  pl.BlockSpec((1, SL * 128, SL * GROUP), lambda g: (l * NLB + g, 0, 0)),
                  pl.BlockSpec((1, NCH, 4 * SW), lambda g: (g, 0, 0), pipeline_mode=pl.Buffered(1)),
                  pl.BlockSpec((1, 4 * SW, SL * GROUP), lambda g: (l * NLB + g, 0, 0)),
                  pl.BlockSpec((1, 1, SL * 128), lambda g: (l * NLB + g, 0, 0))],
        out_specs=pl.BlockSpec((1, R, 128), lambda g: (g, 0, 0)),
        out_shape=jax.ShapeDtypeStruct((NLB, R, 128), F32),
        scratch_shapes=[pltpu.VMEM((NCH, SL * 128), F32),
                        pltpu.VMEM((SL * 128, SL * 128), BF16),
                        pltpu.VMEM((4 * SW, SL * 128), BF16)],
        compiler_params=_cp(("arbitrary",)),
        name="s5_out",
    )(us, toep, s, wout, dflat)


def _glu_kernel(y_ref, w_ref, o_ref):
    y = jnp.concatenate([y_ref[cb] for cb in range(NLB)], axis=1).astype(BF16)
    z = _dot(y, w_ref[0])
    o_ref[...] = (z[:, :D_S] * _sigmoid(z[:, D_S:])).astype(BF16)


def _glu(y, w, l):
    return pl.pallas_call(
        _glu_kernel,
        grid=(NBLK,),
        in_specs=[pl.BlockSpec((NLB, TM, 128), lambda i: (0, i, 0)),
                  pl.BlockSpec((1, D_S, 2 * D_S), lambda i: (l, 0, 0))],
        out_specs=pl.BlockSpec((TM, D_S), lambda i: (i, 0)),
        out_shape=jax.ShapeDtypeStruct((R, D_S), BF16),
        compiler_params=_cp(("arbitrary",)),
        name="s5_glu",
    )(y, w)


def _s5_params(a_re, a_im, log_dt, b_re, b_im, c_re, c_im, dskip):
    dt = jnp.exp(log_dt)[:, :, None]
    lam_r, lam_i = a_re * dt, a_im * dt
    mag = jnp.exp(lam_r)
    ar, ai = mag * jnp.cos(lam_i), mag * jnp.sin(lam_i)
    den = a_re * a_re + a_im * a_im
    nr, ni = ar - 1.0, ai
    cr = (nr * a_re + ni * a_im) / den
    ci = (ni * a_re - nr * a_im) / den
    cpr = c_re[None] * cr[:, :, None, :] - c_im[None] * ci[:, :, None, :]
    cpi = c_re[None] * ci[:, :, None, :] + c_im[None] * cr[:, :, None, :]
    j = jnp.arange(SL + 1, dtype=F32)[:, None, None, None]
    pm = jnp.exp(lam_r[None] * j)
    pr, pi = pm * jnp.cos(lam_i[None] * j), pm * jnp.sin(lam_i[None] * j)
    abr = pr[..., None] * b_re[None, None] - pi[..., None] * b_im[None, None]
    abi = pr[..., None] * b_im[None, None] + pi[..., None] * b_re[None, None]
    kern = (jnp.einsum('dgcp,jdgpe->jdgce', cpr, abr[:SL]) - jnp.einsum('dgcp,jdgpe->jdgce', cpi, abi[:SL]))
    zero = jnp.zeros_like(kern[0, 0])
    rows = []
    for t_in in range(SL):
        blks = []
        for t_out in range(SL):
            blk = zero
            if t_out >= t_in:
                blk = blk + kern[t_out - t_in, 0]
            if t_in >= t_out:
                blk = blk + kern[t_in - t_out, 1]
            blks.append(blk)
        rows.append(jnp.stack(blks, axis=0))
    t5 = jnp.stack(rows, axis=0).reshape(SL, SL, NLB, SLB, GROUP, GROUP)
    toep = t5.transpose(2, 0, 3, 5, 1, 4).reshape(NLB, SL * 128, SL * GROUP)
    rev = lambda a, lo: jnp.stack([a[lo + SL - 1 - t] for t in range(SL)], axis=0)
    ws = jnp.stack([rev(abr[:, 0], 0), rev(abi[:, 0], 0), abr[:SL, 1], abi[:SL, 1]], axis=0)
    ws = ws.reshape(4, SL, NLB, SLB, P_S, GROUP)
    wst = ws.transpose(2, 1, 3, 5, 0, 4).reshape(NLB, SL * 128, 4 * P_S)
    def readout(pw_r, pw_i, d):
        re = cpr[d][None] * pw_r[:, :, None, :] - cpi[d][None] * pw_i[:, :, None, :]
        im = cpr[d][None] * pw_i[:, :, None, :] + cpi[d][None] * pw_r[:, :, None, :]
        return re, -im
    of_re, of_im = readout(pr[1:SL + 1, 0], pi[1:SL + 1, 0], 0)
    ob_re, ob_im = readout(rev(pr[:, 1], 1), rev(pi[:, 1], 1), 1)
    wo = jnp.stack([of_re, of_im, ob_re, ob_im], axis=0).reshape(4, SL, NLB, SLB, GROUP, P_S)
    wout = wo.transpose(2, 0, 3, 5, 1, 4).reshape(NLB, 4 * SW, SL * GROUP)
    blk = lambda a: a.reshape(NLB, 1, SW)
    apow = jnp.concatenate([blk(pr[SL, 0]), blk(pi[SL, 0]), blk(pr[SL, 1]), blk(pi[SL, 1]),
                            jnp.zeros((NLB, 4, SW), F32)], axis=1)
    dflat = jnp.tile(dskip.reshape(NLB, 1, 128), (1, 1, SL))
    return toep.astype(BF16), wst.astype(BF16), wout.astype(BF16), apow, dflat


def _mix_out_kernel(a_ref, hf_ref, hb_ref, om_ref, gh_ref, s_ref, ga_ref, gm_ref, gs_ref,
                    wa_ref, wm_ref, ws_ref, wo_ref, x_ref, gt_ref, gf_ref, sh_ref, sc_ref, x1_ref, h2_ref):
    hsum = hf_ref[0] + hb_ref[0]
    parts = []
    for h in range(H_M):
        xh = hsum[:, h * DH_M:(h + 1) * DH_M]
        parts.append(xh * lax.rsqrt(jnp.mean(xh * xh, axis=-1, keepdims=True) + EPS))
    hn = jnp.concatenate(parts, axis=1) * gh_ref[0]
    m = (_sigmoid(om_ref[...].astype(F32)) * hn).astype(BF16)
    t = (_sigmoid(ga_ref[...].astype(F32)) * _dot(a_ref[...], wa_ref[0])
         + _sigmoid(gm_ref[...].astype(F32)) * _dot(m, wm_ref[0])
         + _sigmoid(gs_ref[...].astype(F32)) * _dot(s_ref[...], ws_ref[0]))
    x1 = x_ref[...] + gt_ref[0] * _dot(t.astype(BF16), wo_ref[0])
    x1_ref[...] = x1
    h2_ref[...] = (_rms(x1, gf_ref[0]) * (1.0 + sc_ref[0]) + sh_ref[0]).astype(BF16)


def _mix_out(a, hdir, p, gh, s, wa, wm, ws, wo, x, modrb, g_ffn, l, nrows):
    tm = 256
    per = TM // tm
    gcol = PM_GATES // D
    row = lambda w, c: pl.BlockSpec((tm, w), lambda i: (i, c))
    once = pl.Buffered(1)
    wspec = pl.BlockSpec((1, 1024, D), lambda i: (l, 0, 0), pipeline_mode=once)
    mod = lambda c: pl.BlockSpec((1, 1, D), lambda i: (i // per, 0, c))
    return pl.pallas_call(
        _mix_out_kernel,
        grid=(nrows // tm,),
        in_specs=[row(1024, 0),
                  pl.BlockSpec((1, tm, D_M), lambda i: (0, i, 0)),
                  pl.BlockSpec((1, tm, D_M), lambda i: (1, i, 0)),
                  row(1024, PM_OM // 1024),
                  pl.BlockSpec((1, 1, D_M), lambda i: (l, 0, 0)),
                  row(1024, 0),
                  row(D, gcol), row(D, gcol + 1), row(D, gcol + 2),
                  wspec, wspec, wspec,
                  pl.BlockSpec((1, D, D), lambda i: (l, 0, 0), pipeline_mode=once),
                  row(D, 0), mod(2),
                  pl.BlockSpec((1, 1, D), lambda i: (l, 0, 0)), mod(3), mod(4)],
        out_specs=[row(D, 0), row(D, 0)],
        out_shape=[jax.ShapeDtypeStruct((nrows, D), F32), jax.ShapeDtypeStruct((nrows, D), BF16)],
        compiler_params=_cp(("arbitrary",)),
        name="mix_out",
    )(a, hdir, hdir, p, gh, s, p, p, p, wa, wm, ws, wo, x, modrb, g_ffn, modrb, modrb)


def _resid_kernel(t_ref, w_ref, x_ref, gt_ref, o_ref, wb_ref):
    @pl.when(pl.program_id(1) == 0)
    def _():
        wb_ref[...] = w_ref[0].astype(BF16)

    o_ref[...] = x_ref[...] + gt_ref[0] * _dot(t_ref[...], wb_ref[...])


def _resid(t, w, x, modrb, gate_chunk, tn, l, nrows, name):
    kdim = t.shape[1]
    nj = D // tn
    return pl.pallas_call(
        _resid_kernel,
        grid=(nj, nrows // TM),
        in_specs=[pl.BlockSpec((TM, kdim), lambda j, i: (i, 0)),
                  pl.BlockSpec((1, kdim, tn), lambda j, i: (l, 0, j)),
                  pl.BlockSpec((TM, tn), lambda j, i: (i, j)),
                  pl.BlockSpec((1, 1, tn), lambda j, i: (i, 0, gate_chunk * nj + j))],
        out_specs=pl.BlockSpec((TM, tn), lambda j, i: (i, j)),
        out_shape=jax.ShapeDtypeStruct((nrows, D), F32),
        scratch_shapes=[pltpu.VMEM((kdim, tn), BF16)],
        compiler_params=_cp(("arbitrary", "arbitrary")),
        name=name,
    )(t, w, x, modrb)


def _ffn_in_kernel(h_ref, wa_ref, wb_ref, o_ref, was_ref, wbs_ref):
    @pl.when(pl.program_id(1) == 0)
    def _():
        was_ref[...] = wa_ref[0].astype(BF16)
        wbs_ref[...] = wb_ref[0].astype(BF16)

    h = h_ref[...]
    a = _dot(h, was_ref[...])
    b = _dot(h, wbs_ref[...])
    o_ref[...] = (a * _sigmoid(a) * b).astype(BF16)


def _ffn_in(h, w, l, nrows):
    tn = 512
    nj = D_FF // tn
    return pl.pallas_call(
        _ffn_in_kernel,
        grid=(nj, nrows // TM),
        in_specs=[pl.BlockSpec((TM, D), lambda j, i: (i, 0)),
                  pl.BlockSpec((1, D, tn), lambda j, i: (l, 0, j)),
                  pl.BlockSpec((1, D, tn), lambda j, i: (l, 0, nj + j))],
        out_specs=pl.BlockSpec((TM, tn), lambda j, i: (i, j)),
        out_shape=jax.ShapeDtypeStruct((nrows, D_FF), BF16),
        scratch_shapes=[pltpu.VMEM((D, tn), BF16), pltpu.VMEM((D, tn), BF16)],
        compiler_params=_cp(("arbitrary", "arbitrary")),
        name="ffn_in",
    )(h, w, w)


def _final_norm_kernel(x_ref, g_ref, o_ref):
    o_ref[...] = _rms(x_ref[...], g_ref[...])


def _final_norm(x, g):
    return pl.pallas_call(
        _final_norm_kernel,
        grid=(NL // TM,),
        in_specs=[pl.BlockSpec((TM, D), lambda i: (i, 0)),
                  pl.BlockSpec((1, D), lambda i: (0, 0))],
        out_specs=pl.BlockSpec((TM, D), lambda i: (i, 0)),
        out_shape=jax.ShapeDtypeStruct((NL, D), F32),
        compiler_params=_cp(("arbitrary",)),
        name="final_norm",
    )(x, g.reshape(1, D))


def _rope_tables():
    f32 = np.float32
    rows = T // GRID_W
    rr, cc = np.meshgrid(np.arange(rows, dtype=f32), np.arange(GRID_W, dtype=f32), indexing='ij')
    rr, cc = rr.reshape(-1), cc.reshape(-1)
    half = ROPE // 2
    inv = (f32(1.0) / (f32(ROPE_THETA) ** (np.arange(0, half, 2, dtype=f32) / f32(half)))).astype(f32)
    ang = np.stack([rr[:, None] * inv, cc[:, None] * inv], axis=1).astype(f32)
    cos = np.cos(ang).astype(f32)
    sin = np.sin(ang).astype(f32)
    cos_f = np.stack([cos, cos], axis=2).reshape(T, ROPE)
    sin_f = np.stack([-sin, sin], axis=2).reshape(T, ROPE)
    pad = lambda a: np.concatenate([a, np.zeros((a.shape[0], 128 - ROPE), f32)], axis=1)
    cos_l, sin_l = pad(cos_f), pad(sin_f)
    cos_c = pad(np.ones((NCX, ROPE), f32))
    sin_c = np.zeros((NCX, 128), f32)
    return (jnp.asarray(np.concatenate([cos_l] * NB_ + [cos_c], axis=0)),
            jnp.asarray(np.concatenate([sin_l] * NB_ + [sin_c], axis=0)))


def _rope_partner(w):
    s = w.shape[:-1]
    w4 = w.reshape(s + (2, 2, ROPE // 4))
    return jnp.concatenate([w4[..., 1:2, :], w4[..., 0:1, :]], axis=-2).reshape(s + (ROPE,))


def _prep_weights(w_uq, w_ukv):
    depth = w_uq.shape[0]
    uq = w_uq.reshape(depth, Q_LORA, H_A, NOPE + ROPE)
    wqn = uq[..., :NOPE].reshape(depth, Q_LORA, H_A * NOPE)
    qr = uq[..., NOPE:]
    zq = jnp.zeros_like(qr)
    wqr = jnp.concatenate([qr, zq], axis=-1).reshape(depth, Q_LORA, H_A * 128)
    wqp = jnp.concatenate([_rope_partner(qr), zq], axis=-1).reshape(depth, Q_LORA, H_A * 128)
    ukv = w_ukv.reshape(depth, KV_LORA, H_A, NOPE + DV)
    wkn = ukv[..., :NOPE].reshape(depth, KV_LORA, H_A * NOPE)
    wvt = ukv[..., NOPE:].reshape(depth, KV_LORA, H_A * DV).transpose(0, 2, 1)
    return tuple(a.astype(BF16) for a in (wqn, wqr, wqp, wkn, wvt))


def kernel(x, c, ctx, c_ctx, w_ada, b_ada, g_mix, g_ffn, w_in, g_cq, w_uq, g_ckv, w_ukv, conv_m, w_qkv_m, b_gate_m, g_h_m, s5_a_re, s5_a_im, s5_log_dt, s5_b_re, s5_b_im, s5_c_re, s5_c_im, s5_d, w_glu, w_br_a, w_br_m, w_br_s, w_out, w_ffn_in, w_ffn_out, g_final):
    depth = w_ada.shape[0]
    xs = jnp.concatenate([x.reshape(NL, D), ctx.reshape(NCX, D)], axis=0)
    cvec = jnp.concatenate([c, c_ctx[None], jnp.zeros((8 - NB_ - 1, D), F32)], axis=0)
    mod = _ada(cvec, w_ada, b_ada)
    blocks_per_batch = T // TM
    cosk, sink = _rope_tables()
    mla_w = _prep_weights(w_uq, w_ukv)
    w_in_t = jnp.swapaxes(w_in, 1, 2)
    toep, wst, wout, apow, dflat = (a.reshape((depth * NLB,) + a.shape[2:]) for a in jax.vmap(_s5_params)(
        s5_a_re, s5_a_im, s5_log_dt, s5_b_re, s5_b_im, s5_c_re, s5_c_im, s5_d))
    w_out_b = w_out.astype(BF16)
    w_qkv_b = jnp.concatenate([w_qkv_m[:, :2], jnp.swapaxes(w_qkv_m[:, 2:], -1, -2)], axis=1).astype(BF16)
    w_glu_b = w_glu.astype(BF16)
    w_br_a_b, w_br_m_b, w_br_s_b = w_br_a.astype(BF16), w_br_m.astype(BF16), w_br_s.astype(BF16)
    gain = lambda g: g.reshape(depth, 1, g.shape[-1])
    g_mix, g_ffn, g_cq, g_ckv, g_h_m = gain(g_mix), gain(g_ffn), gain(g_cq), gain(g_ckv), gain(g_h_m)

    for l in range(depth):
        modrb = jnp.concatenate(
            [jnp.broadcast_to(mod[l, b:b + 1], (blocks_per_batch, 6 * D)) for b in range(NB_)]
            + [jnp.broadcast_to(mod[l, NB_:NB_ + 1], (NCX // TM, 6 * D))], axis=0).reshape(NBLK, 1, 6 * D)
        p = _inproj(xs, g_mix, modrb, w_in_t, l)
        us, psm = _inproj_side(xs, g_mix, modrb, w_in_t, l)

        last = l == depth - 1
        nrows = NL if last else R
        q, k, vt = _mla_proj(p, psm, cosk, sink, g_cq, g_ckv, *mla_w, l)
        a = _attention(q, k, vt, not last)

        qm, km, vm = _mprep(p, conv_m, w_qkv_b, l)
        gm = psm[:, SM_GM:SM_GM + 4 * H_M]
        gd = jnp.stack([gm[:, :2 * H_M], gm[:, 2 * H_M:]], axis=0)
        bd = b_gate_m[l].reshape(2, 1, 2 * H_M)
        hdir = _mlstm(qm, km, vm, gd, gd.transpose(0, 2, 1), bd, bd.transpose(0, 2, 1))

        st = _s5_state(us, wst, apow, l)
        y = _s5_out(us, toep, st, wout, dflat, l)
        s = _glu(y, w_glu_b, l)

        x1, h2 = _mix_out(a, hdir, p, g_h_m, s, w_br_a_b, w_br_m_b, w_br_s_b, w_out_b, xs, modrb, g_ffn, l, nrows)
        u = _ffn_in(h2, w_ffn_in, l, nrows)
        xs = _resid(u, w_ffn_out, x1, modrb, 5, 512, l, nrows, "resid_ffn")

    return _final_norm(xs, g_final).reshape(NB_, T, D)
```

```python
import functools
import math

import numpy as np
import jax
import jax.numpy as jnp
from jax import lax
from jax.experimental import pallas as pl
from jax.experimental.pallas import tpu as pltpu

F32 = jnp.float32
BF16 = jnp.bfloat16

D = 2048
NB_ = 2
T = 4096
TC = 256
GRID_W = 64
EPS = 1e-6
H_A, Q_LORA, KV_LORA, NOPE, ROPE, DV = 8, 512, 512, 128, 64, 128
ROPE_THETA = 10000.0
ATTN_SCALE = (NOPE + ROPE) ** -0.5
H_M, DH_M = 8, 128
D_M = H_M * DH_M
D_S, GROUP, P_S = 1024, 16, 64
G_S = D_S // GROUP
D_FF = ((8 * D // 3 + 255) // 256) * 256
OFF = {}
_o = 0
for _n, _w in (('cq', Q_LORA), ('ckv', KV_LORA), ('krope', ROPE), ('xm', D_M), ('om', D_M),
               ('gm', 4 * H_M), ('us', D_S), ('gates', 3 * D)):
    OFF[_n] = (_o, _w)
    _o += _w

NL = NB_ * T
NCX = NB_ * TC
R = NL + NCX
TM = 512
NBLK = R // TM
LC = 256
TQ = 256
TK = 512
AH = 4
AG = 2
SL = 8
NCH = R // SL
NSEG = 8
SLB = 128 // GROUP
NLB = D_S // 128
SW = SLB * P_S
LOG2E = math.log2(math.e)
PM_CQ, PM_CKV, PM_XM, PM_GATES, PM_OM, PM_W = 0, 512, 1024, 2048, 2048 + 3 * D, 3072 + 3 * D
SM_KR, SM_KP, SM_GM, SM_W = 0, 128, 256, 384
VMEM_LIMIT = 56 * 1024 * 1024
SEQ_STARTS = tuple(b * T for b in range(NB_)) + tuple(NL + b * TC for b in range(NB_))
SEQ_ENDS = tuple(b * T + T - 1 for b in range(NB_)) + tuple(NL + b * TC + TC - 1 for b in range(NB_))


def _cp(sem):
    return pltpu.CompilerParams(dimension_semantics=sem, vmem_limit_bytes=VMEM_LIMIT)


def _dot(a, b):
    return jnp.dot(a, b, preferred_element_type=F32)


def _dot_nt(a, b):
    return lax.dot_general(a, b, (((1,), (1,)), ((), ())), preferred_element_type=F32)


def _dot_tn(a, b):
    return lax.dot_general(a, b, (((0,), (0,)), ((), ())), preferred_element_type=F32)


def _sigmoid(x):
    return 1.0 / (1.0 + jnp.exp(-x))


def _log_sigmoid(x):
    return jnp.minimum(x, 0.0) - jnp.log(1.0 + jnp.exp(-jnp.abs(x)))


def _rms(x, g):
    xf = x.astype(F32)
    return xf * lax.rsqrt(jnp.mean(xf * xf, axis=-1, keepdims=True) + EPS) * g


def _ada_kernel(c_ref, w_ref, b_ref, o_ref):
    c = c_ref[...]
    s = (c * _sigmoid(c)).astype(BF16)
    o_ref[0] = _dot(s, w_ref[0].astype(BF16)) + b_ref[0]


def _ada(cvec, w_ada, b_ada):
    depth = w_ada.shape[0]
    tn = 1024
    return pl.pallas_call(
        _ada_kernel,
        grid=(depth, 6 * D // tn),
        in_specs=[pl.BlockSpec((8, D), lambda l, j: (0, 0)),
                  pl.BlockSpec((1, D, tn), lambda l, j: (l, 0, j)),
                  pl.BlockSpec((1, 1, tn), lambda l, j: (l, 0, j))],
        out_specs=pl.BlockSpec((1, 8, tn), lambda l, j: (l, 0, j)),
        out_shape=jax.ShapeDtypeStruct((depth, 8, 6 * D), F32),
        compiler_params=_cp(("arbitrary", "arbitrary")),
        name="ada",
    )(cvec, w_ada, b_ada.reshape(depth, 1, 6 * D))


def _modulate(x_ref, g_ref, sh_ref, sc_ref):
    return _rms(x_ref[...], g_ref[0]) * (1.0 + sc_ref[0]) + sh_ref[0]


IP_TN = 1024
IP_STARTS = ((OFF['cq'][0], OFF['xm'][0]) + tuple(OFF['gates'][0] + k * IP_TN for k in range(3 * D // IP_TN))
             + (OFF['om'][0],))
IP_SHIFTS = tuple(s % IP_TN for s in IP_STARTS)
assert all(s < 128 and s % 16 == 0 for s in IP_SHIFTS) and OFF['ckv'][0] == Q_LORA and PM_W == IP_TN * len(IP_STARTS)


def _ip_window(j):
    idx = IP_STARTS[-1] // IP_TN
    for jj in range(len(IP_STARTS) - 2, -1, -1):
        idx = jnp.where(j == jj, IP_STARTS[jj] // IP_TN, idx)
    return idx


def _inproj_kernel(x_ref, g_ref, sh_ref, sc_ref, wm_ref, we_ref, o_ref, wb_ref):
    j = pl.program_id(0)

    @pl.when(pl.program_id(1) == 0)
    def _():
        for shift in sorted(set(IP_SHIFTS)):
            hit = functools.reduce(jnp.logical_or, [j == jj for jj, s in enumerate(IP_SHIFTS) if s == shift])

            @pl.when(hit)
            def _(shift=shift):
                wb_ref[0:IP_TN - shift, :] = wm_ref[0, shift:IP_TN, :].astype(BF16)
                if shift:
                    wb_ref[IP_TN - shift:IP_TN, :] = we_ref[0, 0:shift, :].astype(BF16)

    h = _modulate(x_ref, g_ref, sh_ref, sc_ref).astype(BF16)
    o_ref[...] = _dot_nt(h, wb_ref[...]).astype(o_ref.dtype)


def _inproj(x, g, modrb, w_in_t, l):
    return pl.pallas_call(
        _inproj_kernel,
        grid=(len(IP_STARTS), NBLK),
        in_specs=[pl.BlockSpec((TM, D), lambda j, i: (i, 0)),
                  pl.BlockSpec((1, 1, D), lambda j, i: (l, 0, 0)),
                  pl.BlockSpec((1, 1, D), lambda j, i: (i, 0, 0)),
                  pl.BlockSpec((1, 1, D), lambda j, i: (i, 0, 1)),
                  pl.BlockSpec((1, IP_TN, D), lambda j, i: (l, _ip_window(j), 0)),
                  pl.BlockSpec((1, 128, D), lambda j, i: (l, (_ip_window(j) + 1) * (IP_TN // 128), 0))],
        out_specs=pl.BlockSpec((TM, IP_TN), lambda j, i: (i, j)),
        out_shape=jax.ShapeDtypeStruct((R, PM_W), BF16),
        scratch_shapes=[pltpu.VMEM((IP_TN, D), BF16)],
        compiler_params=_cp(("arbitrary", "arbitrary")),
        name="inproj",
    )(x, g, modrb, modrb, w_in_t, w_in_t)


US0, KR0, GM0 = OFF['us'][0], OFF['krope'][0], OFF['gm'][0]
assert KR0 % ROPE == 0 and GM0 % (4 * H_M) == 0 and US0 % IP_TN < 128 and US0 % 16 == 0 and OFF['us'][1] == IP_TN


def _inproj_side_kernel(x_ref, g_ref, sh_ref, sc_ref, wu_ref, wue_ref, wkr_ref, wgm_ref, us_ref, o_ref, wb_ref):
    @pl.when(pl.program_id(0) == 0)
    def _():
        wb_ref[...] = jnp.zeros_like(wb_ref)
        shift = US0 % IP_TN
        wb_ref[0:D_S - shift, :] = wu_ref[0, shift:IP_TN, :].astype(BF16)
        wb_ref[D_S - shift:D_S, :] = wue_ref[0, 0:shift, :].astype(BF16)
        kr = wkr_ref[0].astype(BF16)
        wb_ref[D_S + SM_KR:D_S + SM_KR + ROPE, :] = kr
        half = ROPE // 4
        for blk in range(ROPE // half):
            src = (blk ^ 1) * half
            wb_ref[D_S + SM_KP + blk * half:D_S + SM_KP + (blk + 1) * half, :] = kr[src:src + half, :]
        wb_ref[D_S + SM_GM:D_S + SM_GM + 4 * H_M, :] = wgm_ref[0].astype(BF16)

    h = _modulate(x_ref, g_ref, sh_ref, sc_ref).astype(BF16)
    res = _dot_nt(h, wb_ref[...])
    for cb in range(NLB):
        us_ref[cb] = res[:, cb * 128:(cb + 1) * 128]
    o_ref[...] = res[:, D_S:]


def _inproj_side(x, g, modrb, w_in_t, l):
    once = pl.Buffered(1)
    wblk = lambda rows, row: pl.BlockSpec((1, rows, D), lambda i: (l, row // rows, 0), pipeline_mode=once)
    return pl.pallas_call(
        _inproj_side_kernel,
        grid=(NBLK,),
        in_specs=[pl.BlockSpec((TM, D), lambda i: (i, 0)),
                  pl.BlockSpec((1, 1, D), lambda i: (l, 0, 0)),
                  pl.BlockSpec((1, 1, D), lambda i: (i, 0, 0)),
                  pl.BlockSpec((1, 1, D), lambda i: (i, 0, 1)),
                  wblk(IP_TN, US0 - US0 % IP_TN), wblk(128, US0 - US0 % IP_TN + IP_TN),
                  wblk(ROPE, KR0), wblk(4 * H_M, GM0)],
        out_specs=[pl.BlockSpec((NLB, TM, 128), lambda i: (0, i, 0)),
                   pl.BlockSpec((TM, SM_W), lambda i: (i, 0))],
        out_shape=[jax.ShapeDtypeStruct((NLB, R, 128), F32),
                   jax.ShapeDtypeStruct((R, SM_W), F32)],
        scratch_shapes=[pltpu.VMEM((D_S + SM_W, D), BF16)],
        compiler_params=_cp(("arbitrary",)),
        name="inproj_side",
    )(x, g, modrb, modrb, w_in_t, w_in_t, w_in_t, w_in_t)


def _mla_proj_kernel(cq_ref, ckv_ref, kr_ref, kp_ref, cos_ref, sin_ref, gq_ref, gkv_ref,
                     wqn_ref, wqr_ref, wqp_ref, wkn_ref, wvt_ref, q_ref, k_ref, vt_ref):
    hq = _rms(cq_ref[...], gq_ref[0]).astype(BF16)
    hk = _rms(ckv_ref[...], gkv_ref[0]).astype(BF16)
    cos = cos_ref[...]
    sin = sin_ref[...]
    qscale = ATTN_SCALE * LOG2E
    qn = _dot(hq, wqn_ref[0]) * qscale
    qr = _dot(hq, wqr_ref[0])
    qp = _dot(hq, wqp_ref[0])
    kn = _dot(hk, wkn_ref[0])
    vt = _dot_nt(wvt_ref[0], hk)
    kr = (kr_ref[...] * cos + kp_ref[...] * sin).astype(BF16)
    ones = jnp.ones((DV, TM), BF16)
    for h in range(H_A):
        lo = slice(h * 256, h * 256 + 128)
        hi = slice(h * 256 + 128, (h + 1) * 256)
        sl = slice(h * 128, (h + 1) * 128)
        q_ref[:, lo] = qn[:, sl].astype(BF16)
        q_ref[:, hi] = ((qr[:, sl] * cos + qp[:, sl] * sin) * qscale).astype(BF16)
        k_ref[:, lo] = kn[:, sl].astype(BF16)
        k_ref[:, hi] = kr
        vt_ref[lo, :] = vt[sl, :].astype(BF16)
        vt_ref[hi, :] = ones


def _mla_proj(p, psm, cosk, sink, gq, gkv, wqn, wqr, wqp, wkn, wvt, l):
    lw = lambda shp: pl.BlockSpec((1,) + shp, lambda i: (l, 0, 0))
    return pl.pallas_call(
        _mla_proj_kernel,
        grid=(NBLK,),
        in_specs=[pl.BlockSpec((TM, 512), lambda i: (i, PM_CQ // 512)),
                  pl.BlockSpec((TM, 512), lambda i: (i, PM_CKV // 512)),
                  pl.BlockSpec((TM, 128), lambda i: (i, SM_KR // 128)),
                  pl.BlockSpec((TM, 128), lambda i: (i, SM_KP // 128)),
                  pl.BlockSpec((TM, 128), lambda i: (i, 0)),
                  pl.BlockSpec((TM, 128), lambda i: (i, 0)),
                  pl.BlockSpec((1, 1, 512), lambda i: (l, 0, 0)),
                  pl.BlockSpec((1, 1, 512), lambda i: (l, 0, 0)),
                  lw((512, 1024)), lw((512, 1024)), lw((512, 1024)), lw((512, 1024)), lw((1024, 512))],
        out_specs=[pl.BlockSpec((TM, 2048), lambda i: (i, 0)),
                   pl.BlockSpec((TM, 2048), lambda i: (i, 0)),
                   pl.BlockSpec((2048, TM), lambda i: (0, i))],
        out_shape=[jax.ShapeDtypeStruct((R, 2048), BF16),
                   jax.ShapeDtypeStruct((R, 2048), BF16),
                   jax.ShapeDtypeStruct((2048, R), BF16)],
        compiler_params=_cp(("arbitrary",)),
        name="mla_proj",
    )(p, p, psm, psm, cosk, sink, gq, gkv, wqn, wqr, wqp, wkn, wvt)


def _attn_kernel(q_ref, kl_ref, kc_ref, vl_ref, vc_ref, o_ref, m_ref, acc_ref, st_ref):
    qi = pl.program_id(2)
    heads = [slice(hh * 256, (hh + 1) * 256) for hh in range(AH)]
    qs = [q_ref[:, hs] for hs in heads]

    def group(hh, k_chunks, vt, m, acc):
        mx = None
        nk = k_chunks[0].shape[0]
        for c, k in enumerate(k_chunks):
            st = _dot_nt(k, qs[hh])
            st_ref[hh, c, 0:nk, :] = st
            mx = st if mx is None else jnp.maximum(mx, st)
        m_new = jnp.maximum(m, jnp.max(mx, axis=0, keepdims=True))
        p = jnp.concatenate([jnp.exp2(st_ref[hh, c, 0:nk, :] - m_new).astype(BF16)
                             for c in range(len(k_chunks))], axis=0)
        return m_new, jnp.exp2(m - m_new) * acc + _dot(vt, p)

    for hh, hs in enumerate(heads):
        m, acc = group(hh, [kc_ref[:, hs]], vc_ref[hs, :], jnp.full((1, TQ), -jnp.inf, F32),
                       jnp.zeros((2 * DV, TQ), F32))
        m_ref[hh] = m
        acc_ref[hh] = acc

    @pl.when(qi < T // TQ)
    def _():
        st = [(m_ref[hh], acc_ref[hh]) for hh in range(AH)]
        for j in range(T // (TK * AG)):
            chunks = [slice((j * AG + c) * TK, (j * AG + c + 1) * TK) for c in range(AG)]
            span = slice(j * AG * TK, (j + 1) * AG * TK)
            st = [group(hh, [kl_ref[ks, hs] for ks in chunks], vl_ref[hs, span], *st[hh])
                  for hh, hs in enumerate(heads)]
        for hh in range(AH):
            acc_ref[hh] = st[hh][1]

    for hh in range(AH):
        acc = acc_ref[hh]
        o_ref[:, hh * DV:(hh + 1) * DV] = jnp.transpose(acc[0:DV, :] / acc[DV:2 * DV, :]).astype(o_ref.dtype)


def _attention(q, k, vt, ctx_queries):
    nq = T // TQ
    qrow = lambda b, h, i: jnp.where(i < nq, b * nq + i, NL // TQ + b)
    return pl.pallas_call(
        _attn_kernel,
        grid=(NB_, H_A // AH, nq + 1 if ctx_queries else nq),
        in_specs=[pl.BlockSpec((TQ, AH * 256), lambda b, h, i: (qrow(b, h, i), h)),
                  pl.BlockSpec((T, AH * 256), lambda b, h, i: (b, h)),
                  pl.BlockSpec((TC, AH * 256), lambda b, h, i: (NL // TC + b, h)),
                  pl.BlockSpec((AH * 256, T), lambda b, h, i: (h, b)),
                  pl.BlockSpec((AH * 256, TC), lambda b, h, i: (h, NL // TC + b))],
        out_specs=pl.BlockSpec((TQ, AH * DV), lambda b, h, i: (qrow(b, h, i), h)),
        out_shape=jax.ShapeDtypeStruct((R if ctx_queries else NL, H_A * DV), BF16),
        scratch_shapes=[pltpu.VMEM((AH, 1, TQ), F32), pltpu.VMEM((AH, 2 * DV, TQ), F32),
                        pltpu.VMEM((AH, AG, TK, TQ), F32)],
        compiler_params=_cp(("arbitrary", "arbitrary", "arbitrary")),
        name="attention",
    )(q, k, k, vt, vt)


def _row_in(grow, rows):
    hit = grow == rows[0]
    for r in rows[1:]:
        hit = jnp.logical_or(hit, grow == r)
    return hit


def _mprep_kernel(x_ref, xp_ref, xn_ref, cw_ref, wq_ref, q_ref, k_ref, v_ref):
    i = pl.program_id(0)
    xb = x_ref[...]
    x = xb.astype(F32)
    prev_row = xp_ref[15:16, :].astype(F32)
    next_row = xn_ref[0:1, :].astype(F32)
    row = lax.broadcasted_iota(jnp.int32, (TM, 1), 0)
    grow = row + i * TM
    xprev = jnp.where(row == 0, prev_row, pltpu.roll(x, 1, axis=0))
    xprev = jnp.where(_row_in(grow, SEQ_STARTS), 0.0, xprev)
    xnext = jnp.where(row == TM - 1, next_row, pltpu.roll(x, TM - 1, axis=0))
    xnext = jnp.where(_row_in(grow, SEQ_ENDS), 0.0, xnext)
    cw = cw_ref[0]
    xc = xprev * cw[0:1, :] + x * cw[1:2, :] + xnext * cw[2:3, :]
    xcb = (xc * _sigmoid(xc)).astype(BF16)
    ones = jnp.ones((DH_M, TM), BF16)
    for h in range(H_M):
        sl = slice(h * DH_M, (h + 1) * DH_M)
        q_ref[:, sl] = _dot(xcb[:, sl], wq_ref[0, 0, h]).astype(BF16)
        k_ref[:, sl] = (_dot(xcb[:, sl], wq_ref[0, 1, h]) * DH_M ** -0.5).astype(BF16)
        v_ref[2 * h * DH_M:(2 * h + 1) * DH_M, :] = _dot_nt(wq_ref[0, 2, h], xb[:, sl]).astype(BF16)
        v_ref[(2 * h + 1) * DH_M:(2 * h + 2) * DH_M, :] = ones


def _mprep(p, conv_w, w_qkv, l):
    hb = TM // 16
    xcol = PM_XM // D_M
    return pl.pallas_call(
        _mprep_kernel,
        grid=(NBLK,),
        in_specs=[pl.BlockSpec((TM, D_M), lambda i: (i, xcol)),
                  pl.BlockSpec((16, D_M), lambda i: (jnp.maximum(i * hb - 1, 0), xcol)),
                  pl.BlockSpec((16, D_M), lambda i: (jnp.minimum((i + 1) * hb, R // 16 - 1), xcol)),
                  pl.BlockSpec((1, 3, D_M), lambda i: (l, 0, 0)),
                  pl.BlockSpec((1, 3, H_M, DH_M, DH_M), lambda i: (l, 0, 0, 0, 0))],
        out_specs=[pl.BlockSpec((TM, D_M), lambda i: (i, 0)),
                   pl.BlockSpec((TM, D_M), lambda i: (i, 0)),
                   pl.BlockSpec((2 * D_M, TM), lambda i: (0, i))],
        out_shape=[jax.ShapeDtypeStruct((R, D_M), BF16),
                   jax.ShapeDtypeStruct((R, D_M), BF16),
                   jax.ShapeDtypeStruct((2 * D_M, R), BF16)],
        compiler_params=_cp(("arbitrary",)),
        name="mlstm_prep",
    )(p, p, p, conv_w, w_qkv)


def _split3(x):
    h = x.astype(BF16)
    r = x - h.astype(F32)
    m = r.astype(BF16)
    l = (r - m.astype(F32)).astype(BF16)
    return h, m, l


def _mlstm_kernel(q_ref, k_ref, v_ref, g_ref, gt_ref, b_ref, bt_ref, h_ref, c_ref, m_ref):
    d = pl.program_id(1)
    s = pl.program_id(2)

    @pl.when(s == 0)
    def _():
        c_ref[...] = jnp.zeros_like(c_ref)
        m_ref[...] = jnp.zeros_like(m_ref)

    sign = 1 - 2 * d
    r = lax.broadcasted_iota(jnp.int32, (LC, LC), 0)
    c = lax.broadcasted_iota(jnp.int32, (LC, LC), 1)
    before = (c - r) * sign >= 0
    tri = jnp.where((r - c) * sign >= 0, 1.0, 0.0).astype(BF16)
    g = g_ref[0] + b_ref[0]
    gt = gt_ref[0] + bt_ref[0]
    ig_c = g[:, 0:H_M] * LOG2E
    lf_c = _log_sigmoid(g[:, H_M:2 * H_M]) * LOG2E
    lf_r = _log_sigmoid(gt[H_M:2 * H_M, :]) * LOG2E
    b_c = sum(_dot(tri, part) for part in _split3(lf_c))
    b_r = sum(_dot_nt(part, tri) for part in _split3(lf_r))
    tot = jnp.sum(lf_c, axis=0, keepdims=True)
    src = ig_c - b_c

    for h in range(H_M):
        sl = slice(h * DH_M, (h + 1) * DH_M)
        qh = q_ref[:, sl]
        kh = k_ref[:, sl]
        vth = v_ref[2 * h * DH_M:(2 * h + 2) * DH_M, :]
        bt = b_r[h:h + 1, :]
        tt = tot[:, h:h + 1]
        mp = m_ref[h:h + 1, 0:1]
        dm = jnp.where(before, bt + src[:, h:h + 1], -jnp.inf)
        m_inter = bt + mp
        m_t = jnp.maximum(jnp.max(dm, axis=0, keepdims=True), m_inter)
        sc = _dot_nt(kh, qh) * jnp.exp2(dm - m_t)
        inter = jnp.exp2(m_inter - m_t)
        st = c_ref[h]
        both = _dot(vth, sc.astype(BF16)) + inter * _dot_nt(st.astype(BF16), qh)
        den = both[DH_M:DH_M + 1, :]
        ht = both[0:DH_M, :] / jnp.maximum(jnp.abs(den), jnp.exp2(-m_t))
        h_ref[0, :, sl] = jnp.transpose(ht)
        gg = tt + src[:, h:h + 1]
        m_new = jnp.maximum(tt + mp, jnp.max(gg, axis=0, keepdims=True))
        kw = (kh.astype(F32) * jnp.exp2(gg - m_new)).astype(BF16)
        c_ref[h] = jnp.exp2(tt + mp - m_new) * st + _dot(vth, kw)
        m_ref[h:h + 1, :] = jnp.broadcast_to(m_new, (1, DH_M))


def _mlstm(q, k, v, gd, gtd, bd, btd):
    nl = T // LC
    nsteps = nl + TC // LC

    def rb(b, d, s):
        lat = b * nl + jnp.where(d == 0, s - 1, nl - s)
        return jnp.where(s == 0, NL // LC + b, lat)

    qspec = pl.BlockSpec((LC, D_M), lambda b, d, s: (rb(b, d, s), 0))
    return pl.pallas_call(
        _mlstm_kernel,
        grid=(NB_, 2, nsteps),
        in_specs=[qspec, qspec,
                  pl.BlockSpec((2 * D_M, LC), lambda b, d, s: (0, rb(b, d, s))),
                  pl.BlockSpec((1, LC, 16), lambda b, d, s: (d, rb(b, d, s), 0)),
                  pl.BlockSpec((1, 16, LC), lambda b, d, s: (d, 0, rb(b, d, s))),
                  pl.BlockSpec((1, 1, 16), lambda b, d, s: (d, 0, 0)),
                  pl.BlockSpec((1, 16, 1), lambda b, d, s: (d, 0, 0))],
        out_specs=pl.BlockSpec((1, LC, D_M), lambda b, d, s: (d, rb(b, d, s), 0)),
        out_shape=jax.ShapeDtypeStruct((2, R, D_M), F32),
        scratch_shapes=[pltpu.VMEM((H_M, 2 * DH_M, DH_M), F32),
                        pltpu.VMEM((H_M, DH_M), F32)],
        compiler_params=_cp(("arbitrary", "arbitrary", "arbitrary")),
        name="mlstm_scan",
    )(q, k, v, gd, gtd, bd, btd)


S5_NLAT = T // SL
S5_NCTX = TC // SL
S5_SEG = S5_NLAT // NSEG


def _fold_copy(tok_ref, fold_ref, to_fold):
    def move(tok_idx, fold_rows, t):
        lanes = slice(t * 128, (t + 1) * 128)
        if to_fold:
            fold_ref[fold_rows, lanes] = tok_ref[tok_idx, :]
        else:
            tok_ref[tok_idx, :] = fold_ref[fold_rows, lanes]

    def body(j, carry):
        for b in range(NB_):
            rows = pl.ds(pl.multiple_of(b * S5_NLAT + j * NSEG, NSEG), NSEG)
            for t in range(SL):
                move(pl.ds(b * T + j * SL + t, NSEG, stride=S5_SEG * SL), rows, t)
        return carry

    lax.fori_loop(0, S5_SEG, body, 0)
    for t in range(SL):
        move(pl.ds(NL + t, NCX // SL, stride=SL), slice(NL // SL, NCH), t)


def _expand_block_diag(compact_ref, dense_ref, key_shift, key_mul, key_mask, row_gshift, col_gshift):
    comp = compact_ref[0]
    nrows, kc = comp.shape
    chunk = 512
    r = lax.broadcasted_iota(jnp.int32, (kc, chunk), 0)
    rg = (lax.broadcasted_iota(jnp.int32, (nrows, 1), 0) >> row_gshift) & (SLB - 1)
    for c0 in range(0, dense_ref.shape[1], chunk):
        q = lax.broadcasted_iota(jnp.int32, (kc, chunk), 1) + c0
        sel = jnp.where(r == (q >> key_shift) * key_mul + (q & key_mask), 1.0, 0.0).astype(BF16)
        cg = ((lax.broadcasted_iota(jnp.int32, (1, chunk), 1) + c0) >> col_gshift) & (SLB - 1)
        dense_ref[:, c0:c0 + chunk] = jnp.where(rg == cg, _dot(comp, sel), 0.0).astype(BF16)


def _cmul(ar, ai, sr, si):
    return ar * sr - ai * si, ar * si + ai * sr


def _s5_state_kernel(us_ref, wst_ref, a_ref, s_ref, e_ref, x_ref, w_ref):
    _fold_copy(us_ref.at[0], x_ref, True)
    _expand_block_diag(wst_ref, w_ref, 9, P_S, P_S - 1, 4, 6)
    e_ref[...] = _dot(x_ref[...].astype(BF16), w_ref[...])
    apow = a_ref[0]
    step_a = ((apow[0:1, :], apow[1:2, :]), (apow[2:3, :], apow[3:4, :]))
    nctx, seg = S5_NCTX, S5_SEG

    def advance(rows, d, st, write):
        cr = slice(2 * d * SW, (2 * d + 1) * SW)
        ci = slice((2 * d + 1) * SW, (2 * d + 2) * SW)
        dr = e_ref[rows, cr]
        di = e_ref[rows, ci]
        if write:
            e_ref[rows, cr] = st[0]
            e_ref[rows, ci] = st[1]
        nr, ni = _cmul(step_a[d][0], step_a[d][1], st[0], st[1])
        return nr + dr, ni + di

    def sweep(nsteps, rows_of, init, write):
        def body(kk, sts):
            out = []
            for b in range(NB_):
                out.append(advance(rows_of(b, kk), 0, sts[2 * b], write))
                out.append(advance(rows_of(b, nsteps - 1 - kk), 1, sts[2 * b + 1], write))
            return tuple(out)
        return lax.fori_loop(0, nsteps, body, init)

    z1 = jnp.zeros((1, SW), F32)
    carry = sweep(nctx, lambda b, kk: pl.ds(NL // SL + b * nctx + kk, 1), ((z1, z1),) * (2 * NB_), True)

    seg_rows = lambda b, kk: pl.ds(pl.multiple_of(b * S5_NLAT + kk * NSEG, NSEG), NSEG)
    z8 = jnp.zeros((NSEG, SW), F32)
    ends = sweep(seg, seg_rows, ((z8, z8),) * (2 * NB_), False)
    inits = []
    for ch in range(2 * NB_):
        d = ch % 2
        ar, ai = step_a[d]
        for _ in range(int(math.log2(seg))):
            ar, ai = ar * ar - ai * ai, 2.0 * ar * ai
        cur = carry[ch]
        rows_r, rows_i = [None] * NSEG, [None] * NSEG
        for kseg in (range(NSEG) if d == 0 else range(NSEG - 1, -1, -1)):
            rows_r[kseg], rows_i[kseg] = cur
            nr, ni = _cmul(ar, ai, cur[0], cur[1])
            cur = (nr + ends[ch][0][kseg:kseg + 1, :], ni + ends[ch][1][kseg:kseg + 1, :])
        inits.append((jnp.concatenate(rows_r, axis=0), jnp.concatenate(rows_i, axis=0)))
    sweep(seg, seg_rows, tuple(inits), True)
    s_ref[0] = e_ref[...].astype(BF16)


def _s5_state(us, wst, apow, l):
    return pl.pallas_call(
        _s5_state_kernel,
        grid=(NLB,),
        in_specs=[pl.BlockSpec((1, R, 128), lambda g: (g, 0, 0), pipeline_mode=pl.Buffered(1)),
                  pl.BlockSpec((1, SL * 128, 4 * P_S), lambda g: (l * NLB + g, 0, 0)),
                  pl.BlockSpec((1, 8, SW), lambda g: (l * NLB + g, 0, 0))],
        out_specs=pl.BlockSpec((1, NCH, 4 * SW), lambda g: (g, 0, 0)),
        out_shape=jax.ShapeDtypeStruct((NLB, NCH, 4 * SW), BF16),
        scratch_shapes=[pltpu.VMEM((NCH, 4 * SW), F32),
                        pltpu.VMEM((NCH, SL * 128), F32),
                        pltpu.VMEM((SL * 128, 4 * SW), BF16)],
        compiler_params=_cp(("arbitrary",)),
        name="s5_state",
    )(us, wst, apow)


def _s5_out_kernel(us_ref, toep_ref, s_ref, wo_ref, d_ref, y_ref, x_ref, tw_ref, ow_ref):
    _fold_copy(us_ref.at[0], x_ref, True)
    _expand_block_diag(toep_ref, tw_ref, 7, GROUP, GROUP - 1, 4, 4)
    _expand_block_diag(wo_ref, ow_ref, 7, GROUP, GROUP - 1, 6, 4)
    x = x_ref[...]
    y = _dot(x.astype(BF16), tw_ref[...]) + _dot(s_ref[0], ow_ref[...]) + d_ref[0] * x
    x_ref[...] = jax.nn.gelu(y)
    _fold_copy(y_ref.at[0], x_ref, False)


def _s5_out(us, toep, s, wout, dflat, l):
    return pl.pallas_call(
        _s5_out_kernel,
        grid=(NLB,),
        in_specs=[pl.BlockSpec((1, R, 128), lambda g: (g, 0, 0), pipeline_mode=pl.Buffered(1)),
                  pl.BlockSpec((1, SL * 128, SL * GROUP), lambda g: (l * NLB + g, 0, 0)),
                  pl.BlockSpec((1, NCH, 4 * SW), lambda g: (g, 0, 0), pipeline_mode=pl.Buffered(1)),
                  pl.BlockSpec((1, 4 * SW, SL * GROUP), lambda g: (l * NLB + g, 0, 0)),
                  pl.BlockSpec((1, 1, SL * 128), lambda g: (l * NLB + g, 0, 0))],
        out_specs=pl.BlockSpec((1, R, 128), lambda g: (g, 0, 0)),
        out_shape=jax.ShapeDtypeStruct((NLB, R, 128), F32),
        scratch_shapes=[pltpu.VMEM((NCH, SL * 128), F32),
                        pltpu.VMEM((SL * 128, SL * 128), BF16),
                        pltpu.VMEM((4 * SW, SL * 128), BF16)],
        compiler_params=_cp(("arbitrary",)),
        name="s5_out",
    )(us, toep, s, wout, dflat)


def _glu_kernel(y_ref, w_ref, o_ref):
    y = jnp.concatenate([y_ref[cb] for cb in range(NLB)], axis=1).astype(BF16)
    z = _dot(y, w_ref[0])
    o_ref[...] = (z[:, :D_S] * _sigmoid(z[:, D_S:])).astype(BF16)


def _glu(y, w, l):
    return pl.pallas_call(
        _glu_kernel,
        grid=(NBLK,),
        in_specs=[pl.BlockSpec((NLB, TM, 128), lambda i: (0, i, 0)),
                  pl.BlockSpec((1, D_S, 2 * D_S), lambda i: (l, 0, 0))],
        out_specs=pl.BlockSpec((TM, D_S), lambda i: (i, 0)),
        out_shape=jax.ShapeDtypeStruct((R, D_S), BF16),
        compiler_params=_cp(("arbitrary",)),
        name="s5_glu",
    )(y, w)


def _s5_params(a_re, a_im, log_dt, b_re, b_im, c_re, c_im, dskip):
    dt = jnp.exp(log_dt)[:, :, None]
    lam_r, lam_i = a_re * dt, a_im * dt
    mag = jnp.exp(lam_r)
    ar, ai = mag * jnp.cos(lam_i), mag * jnp.sin(lam_i)
    den = a_re * a_re + a_im * a_im
    nr, ni = ar - 1.0, ai
    cr = (nr * a_re + ni * a_im) / den
    ci = (ni * a_re - nr * a_im) / den
    cpr = c_re[None] * cr[:, :, None, :] - c_im[None] * ci[:, :, None, :]
    cpi = c_re[None] * ci[:, :, None, :] + c_im[None] * cr[:, :, None, :]
    j = jnp.arange(SL + 1, dtype=F32)[:, None, None, None]
    pm = jnp.exp(lam_r[None] * j)
    pr, pi = pm * jnp.cos(lam_i[None] * j), pm * jnp.sin(lam_i[None] * j)
    bt_re, bt_im = b_re.transpose(0, 2, 1), b_im.transpose(0, 2, 1)
    abr = pr[:, :, :, None, :] * bt_re[None, None] - pi[:, :, :, None, :] * bt_im[None, None]
    abi = pr[:, :, :, None, :] * bt_im[None, None] + pi[:, :, :, None, :] * bt_re[None, None]
    kern = (jnp.einsum('dgcp,jdgep->djgec', cpr, abr[:SL]) - jnp.einsum('dgcp,jdgep->djgec', cpi, abi[:SL]))
    zero = jnp.zeros_like(kern[0, 0])
    rows = []
    for t_in in range(SL):
        blks = []
        for t_out in range(SL):
            blk = zero
            if t_out >= t_in:
                blk = blk + kern[0, t_out - t_in]
            if t_in >= t_out:
                blk = blk + kern[1, t_in - t_out]
            blks.append(blk)
        rows.append(jnp.stack(blks, axis=2))
    by_block = lambda a, lead, minor: a.reshape(lead + (NLB, SLB) + minor)
    t5 = by_block(jnp.stack(rows, axis=0), (SL,), (GROUP, SL * GROUP))
    toep = t5.transpose(1, 0, 2, 3, 4).reshape(NLB, SL * 128, SL * GROUP)
    rev = lambda a, lo: jnp.stack([a[lo + SL - 1 - t] for t in range(SL)], axis=0)
    ws = jnp.stack([rev(abr[:, 0], 0), rev(abi[:, 0], 0), abr[:SL, 1], abi[:SL, 1]], axis=3)
    ws = by_block(ws, (SL,), (GROUP, 4 * P_S))
    wst = ws.transpose(1, 0, 2, 3, 4).reshape(NLB, SL * 128, 4 * P_S)
    def readout(pw_r, pw_i, d):
        cr_t, ci_t = cpr[d].transpose(0, 2, 1)[:, :, None, :], cpi[d].transpose(0, 2, 1)[:, :, None, :]
        wr, wi = pw_r.transpose(1, 2, 0)[..., None], pw_i.transpose(1, 2, 0)[..., None]
        return cr_t * wr - ci_t * wi, -(cr_t * wi + ci_t * wr)
    of_re, of_im = readout(pr[1:SL + 1, 0], pi[1:SL + 1, 0], 0)
    ob_re, ob_im = readout(rev(pr[:, 1], 1), rev(pi[:, 1], 1), 1)
    wo = by_block(jnp.stack([of_re, of_im, ob_re, ob_im], axis=0), (4,), (P_S, SL * GROUP))
    wout = wo.transpose(1, 0, 2, 3, 4).reshape(NLB, 4 * SW, SL * GROUP)
    blk = lambda a: a.reshape(NLB, 1, SW)
    apow = jnp.concatenate([blk(pr[SL, 0]), blk(pi[SL, 0]), blk(pr[SL, 1]), blk(pi[SL, 1]),
                            jnp.zeros((NLB, 4, SW), F32)], axis=1)
    dflat = jnp.tile(dskip.reshape(NLB, 1, 128), (1, 1, SL))
    return toep.astype(BF16), wst.astype(BF16), wout.astype(BF16), apow, dflat


def _mix_out_kernel(a_ref, hf_ref, hb_ref, om_ref, gh_ref, s_ref, ga_ref, gm_ref, gs_ref,
                    wa_ref, wm_ref, ws_ref, wo_ref, x_ref, gt_ref, gf_ref, sh_ref, sc_ref, x1_ref, h2_ref):
    hsum = hf_ref[0] + hb_ref[0]
    parts = []
    for h in range(H_M):
        xh = hsum[:, h * DH_M:(h + 1) * DH_M]
        parts.append(xh * lax.rsqrt(jnp.mean(xh * xh, axis=-1, keepdims=True) + EPS))
    hn = jnp.concatenate(parts, axis=1) * gh_ref[0]
    m = (_sigmoid(om_ref[...].astype(F32)) * hn).astype(BF16)
    t = (_sigmoid(ga_ref[...].astype(F32)) * _dot(a_ref[...], wa_ref[0])
         + _sigmoid(gm_ref[...].astype(F32)) * _dot(m, wm_ref[0])
         + _sigmoid(gs_ref[...].astype(F32)) * _dot(s_ref[...], ws_ref[0]))
    x1 = x_ref[...] + gt_ref[0] * _dot(t.astype(BF16), wo_ref[0])
    x1_ref[...] = x1
    h2_ref[...] = (_rms(x1, gf_ref[0]) * (1.0 + sc_ref[0]) + sh_ref[0]).astype(BF16)


def _mix_out(a, hdir, p, gh, s, wa, wm, ws, wo, x, modrb, g_ffn, l, nrows):
    tm = 256
    per = TM // tm
    gcol = PM_GATES // D
    row = lambda w, c: pl.BlockSpec((tm, w), lambda i: (i, c))
    once = pl.Buffered(1)
    wspec = pl.BlockSpec((1, 1024, D), lambda i: (l, 0, 0), pipeline_mode=once)
    mod = lambda c: pl.BlockSpec((1, 1, D), lambda i: (i // per, 0, c))
    return pl.pallas_call(
        _mix_out_kernel,
        grid=(nrows // tm,),
        in_specs=[row(1024, 0),
                  pl.BlockSpec((1, tm, D_M), lambda i: (0, i, 0)),
                  pl.BlockSpec((1, tm, D_M), lambda i: (1, i, 0)),
                  row(1024, PM_OM // 1024),
                  pl.BlockSpec((1, 1, D_M), lambda i: (l, 0, 0)),
                  row(1024, 0),
                  row(D, gcol), row(D, gcol + 1), row(D, gcol + 2),
                  wspec, wspec, wspec,
                  pl.BlockSpec((1, D, D), lambda i: (l, 0, 0), pipeline_mode=once),
                  row(D, 0), mod(2),
                  pl.BlockSpec((1, 1, D), lambda i: (l, 0, 0)), mod(3), mod(4)],
        out_specs=[row(D, 0), row(D, 0)],
        out_shape=[jax.ShapeDtypeStruct((nrows, D), F32), jax.ShapeDtypeStruct((nrows, D), BF16)],
        compiler_params=_cp(("arbitrary",)),
        name="mix_out",
    )(a, hdir, hdir, p, gh, s, p, p, p, wa, wm, ws, wo, x, modrb, g_ffn, modrb, modrb)


def _resid_kernel(t_ref, w_ref, x_ref, gt_ref, o_ref, wb_ref):
    @pl.when(pl.program_id(1) == 0)
    def _():
        wb_ref[...] = w_ref[0].astype(BF16)

    o_ref[...] = x_ref[...] + gt_ref[0] * _dot(t_ref[...], wb_ref[...])


def _resid(t, w, x, modrb, gate_chunk, tn, l, nrows, name):
    kdim = t.shape[1]
    nj = D // tn
    return pl.pallas_call(
        _resid_kernel,
        grid=(nj, nrows // TM),
        in_specs=[pl.BlockSpec((TM, kdim), lambda j, i: (i, 0)),
                  pl.BlockSpec((1, kdim, tn), lambda j, i: (l, 0, j)),
                  pl.BlockSpec((TM, tn), lambda j, i: (i, j)),
                  pl.BlockSpec((1, 1, tn), lambda j, i: (i, 0, gate_chunk * nj + j))],
        out_specs=pl.BlockSpec((TM, tn), lambda j, i: (i, j)),
        out_shape=jax.ShapeDtypeStruct((nrows, D), F32),
        scratch_shapes=[pltpu.VMEM((kdim, tn), BF16)],
        compiler_params=_cp(("arbitrary", "arbitrary")),
        name=name,
    )(t, w, x, modrb)


def _ffn_in_kernel(h_ref, wa_ref, wb_ref, o_ref, was_ref, wbs_ref):
    @pl.when(pl.program_id(1) == 0)
    def _():
        was_ref[...] = wa_ref[0].astype(BF16)
        wbs_ref[...] = wb_ref[0].astype(BF16)

    h = h_ref[...]
    a = _dot(h, was_ref[...])
    b = _dot(h, wbs_ref[...])
    o_ref[...] = (a * _sigmoid(a) * b).astype(BF16)


def _ffn_in(h, w, l, nrows):
    tn = 512
    nj = D_FF // tn
    return pl.pallas_call(
        _ffn_in_kernel,
        grid=(nj, nrows // TM),
        in_specs=[pl.BlockSpec((TM, D), lambda j, i: (i, 0)),
                  pl.BlockSpec((1, D, tn), lambda j, i: (l, 0, j)),
                  pl.BlockSpec((1, D, tn), lambda j, i: (l, 0, nj + j))],
        out_specs=pl.BlockSpec((TM, tn), lambda j, i: (i, j)),
        out_shape=jax.ShapeDtypeStruct((nrows, D_FF), BF16),
        scratch_shapes=[pltpu.VMEM((D, tn), BF16), pltpu.VMEM((D, tn), BF16)],
        compiler_params=_cp(("arbitrary", "arbitrary")),
        name="ffn_in",
    )(h, w, w)


def _final_norm_kernel(x_ref, g_ref, o_ref):
    o_ref[...] = _rms(x_ref[...], g_ref[...])


def _final_norm(x, g):
    return pl.pallas_call(
        _final_norm_kernel,
        grid=(NL // TM,),
        in_specs=[pl.BlockSpec((TM, D), lambda i: (i, 0)),
                  pl.BlockSpec((1, D), lambda i: (0, 0))],
        out_specs=pl.BlockSpec((TM, D), lambda i: (i, 0)),
        out_shape=jax.ShapeDtypeStruct((NL, D), F32),
        compiler_params=_cp(("arbitrary",)),
        name="final_norm",
    )(x, g.reshape(1, D))


def _rope_tables():
    f32 = np.float32
    rows = T // GRID_W
    rr, cc = np.meshgrid(np.arange(rows, dtype=f32), np.arange(GRID_W, dtype=f32), indexing='ij')
    rr, cc = rr.reshape(-1), cc.reshape(-1)
    half = ROPE // 2
    inv = (f32(1.0) / (f32(ROPE_THETA) ** (np.arange(0, half, 2, dtype=f32) / f32(half)))).astype(f32)
    ang = np.stack([rr[:, None] * inv, cc[:, None] * inv], axis=1).astype(f32)
    cos = np.cos(ang).astype(f32)
    sin = np.sin(ang).astype(f32)
    cos_f = np.stack([cos, cos], axis=2).reshape(T, ROPE)
    sin_f = np.stack([-sin, sin], axis=2).reshape(T, ROPE)
    pad = lambda a: np.concatenate([a, np.zeros((a.shape[0], 128 - ROPE), f32)], axis=1)
    cos_l, sin_l = pad(cos_f), pad(sin_f)
    cos_c = pad(np.ones((NCX, ROPE), f32))
    sin_c = np.zeros((NCX, 128), f32)
    return (jnp.asarray(np.concatenate([cos_l] * NB_ + [cos_c], axis=0)),
            jnp.asarray(np.concatenate([sin_l] * NB_ + [sin_c], axis=0)))


def _rope_partner(w):
    s = w.shape[:-1]
    w4 = w.reshape(s + (2, 2, ROPE // 4))
    return jnp.concatenate([w4[..., 1:2, :], w4[..., 0:1, :]], axis=-2).reshape(s + (ROPE,))


def _prep_weights(w_uq, w_ukv):
    depth = w_uq.shape[0]
    uq = w_uq.reshape(depth, Q_LORA, H_A, NOPE + ROPE)
    wqn = uq[..., :NOPE].reshape(depth, Q_LORA, H_A * NOPE)
    qr = uq[..., NOPE:]
    zq = jnp.zeros_like(qr)
    wqr = jnp.concatenate([qr, zq], axis=-1).reshape(depth, Q_LORA, H_A * 128)
    wqp = jnp.concatenate([_rope_partner(qr), zq], axis=-1).reshape(depth, Q_LORA, H_A * 128)
    ukv = w_ukv.reshape(depth, KV_LORA, H_A, NOPE + DV)
    wkn = ukv[..., :NOPE].reshape(depth, KV_LORA, H_A * NOPE)
    wvt = ukv[..., NOPE:].reshape(depth, KV_LORA, H_A * DV).transpose(0, 2, 1)
    return tuple(a.astype(BF16) for a in (wqn, wqr, wqp, wkn, wvt))


def kernel(x, c, ctx, c_ctx, w_ada, b_ada, g_mix, g_ffn, w_in, g_cq, w_uq, g_ckv, w_ukv, conv_m, w_qkv_m, b_gate_m, g_h_m, s5_a_re, s5_a_im, s5_log_dt, s5_b_re, s5_b_im, s5_c_re, s5_c_im, s5_d, w_glu, w_br_a, w_br_m, w_br_s, w_out, w_ffn_in, w_ffn_out, g_final):
    depth = w_ada.shape[0]
    xs = jnp.concatenate([x.reshape(NL, D), ctx.reshape(NCX, D)], axis=0)
    cvec = jnp.concatenate([c, c_ctx[None], jnp.zeros((8 - NB_ - 1, D), F32)], axis=0)
    mod = _ada(cvec, w_ada, b_ada)
    blocks_per_batch = T // TM
    cosk, sink = _rope_tables()
    mla_w = _prep_weights(w_uq, w_ukv)
    w_in_t = jnp.swapaxes(w_in, 1, 2)
    toep, wst, wout, apow, dflat = (a.reshape((depth * NLB,) + a.shape[2:]) for a in jax.vmap(_s5_params)(
        s5_a_re, s5_a_im, s5_log_dt, s5_b_re, s5_b_im, s5_c_re, s5_c_im, s5_d))
    w_out_b = w_out.astype(BF16)
    w_qkv_b = jnp.concatenate([w_qkv_m[:, :2], jnp.swapaxes(w_qkv_m[:, 2:], -1, -2)], axis=1).astype(BF16)
    w_glu_b = w_glu.astype(BF16)
    w_br_a_b, w_br_m_b, w_br_s_b = w_br_a.astype(BF16), w_br_m.astype(BF16), w_br_s.astype(BF16)
    gain = lambda g: g.reshape(depth, 1, g.shape[-1])
    g_mix, g_ffn, g_cq, g_ckv, g_h_m = gain(g_mix), gain(g_ffn), gain(g_cq), gain(g_ckv), gain(g_h_m)

    for l in range(depth):
        modrb = jnp.concatenate(
            [jnp.broadcast_to(mod[l, b:b + 1], (blocks_per_batch, 6 * D)) for b in range(NB_)]
            + [jnp.broadcast_to(mod[l, NB_:NB_ + 1], (NCX // TM, 6 * D))], axis=0).reshape(NBLK, 1, 6 * D)
        p = _inproj(xs, g_mix, modrb, w_in_t, l)
        us, psm = _inproj_side(xs, g_mix, modrb, w_in_t, l)

        last = l == depth - 1
        nrows = NL if last else R
        q, k, vt = _mla_proj(p, psm, cosk, sink, g_cq, g_ckv, *mla_w, l)
        a = _attention(q, k, vt, not last)

        qm, km, vm = _mprep(p, conv_m, w_qkv_b, l)
        gm = psm[:, SM_GM:SM_GM + 4 * H_M]
        gd = jnp.stack([gm[:, :2 * H_M], gm[:, 2 * H_M:]], axis=0)
        bd = b_gate_m[l].reshape(2, 1, 2 * H_M)
        hdir = _mlstm(qm, km, vm, gd, gd.transpose(0, 2, 1), bd, bd.transpose(0, 2, 1))

        st = _s5_state(us, wst, apow, l)
        y = _s5_out(us, toep, st, wout, dflat, l)
        s = _glu(y, w_glu_b, l)

        x1, h2 = _mix_out(a, hdir, p, g_h_m, s, w_br_a_b, w_br_m_b, w_br_s_b, w_out_b, xs, modrb, g_ffn, l, nrows)
        u = _ffn_in(h2, w_ffn_in, l, nrows)
        xs = _resid(u, w_ffn_out, x1, modrb, 5, 512, l, nrows, "resid_ffn")

    return _final_norm(xs, g_final).reshape(NB_, T, D)
```

```python
import functools
import math

import numpy as np
import jax
import jax.numpy as jnp
from jax import lax
from jax.experimental import pallas as pl
from jax.experimental.pallas import tpu as pltpu

F32 = jnp.float32
BF16 = jnp.bfloat16

D = 2048
NB_ = 2
T = 4096
TC = 256
GRID_W = 64
EPS = 1e-6
H_A, Q_LORA, KV_LORA, NOPE, ROPE, DV = 8, 512, 512, 128, 64, 128
ROPE_THETA = 10000.0
ATTN_SCALE = (NOPE + ROPE) ** -0.5
H_M, DH_M = 8, 128
D_M = H_M * DH_M
D_S, GROUP, P_S = 1024, 16, 64
G_S = D_S // GROUP
D_FF = ((8 * D // 3 + 255) // 256) * 256
OFF = {}
_o = 0
for _n, _w in (('cq', Q_LORA), ('ckv', KV_LORA), ('krope', ROPE), ('xm', D_M), ('om', D_M),
               ('gm', 4 * H_M), ('us', D_S), ('gates', 3 * D)):
    OFF[_n] = (_o, _w)
    _o += _w

NL = NB_ * T
NCX = NB_ * TC
R = NL + NCX
TM = 512
NBLK = R // TM
LC = 256
TQ = 256
TK = 512
AH = 4
AG = 2
SL = 8
NCH = R // SL
NSEG = 8
SLB = 128 // GROUP
NLB = D_S // 128
SW = SLB * P_S
LOG2E = math.log2(math.e)
PM_CQ, PM_CKV, PM_XM, PM_GATES, PM_OM, PM_W = 0, 512, 1024, 2048, 2048 + 3 * D, 3072 + 3 * D
SM_KR, SM_KP, SM_GM, SM_W = 0, 128, 256, 384
VMEM_LIMIT = 56 * 1024 * 1024
SEQ_STARTS = tuple(b * T for b in range(NB_)) + tuple(NL + b * TC for b in range(NB_))
SEQ_ENDS = tuple(b * T + T - 1 for b in range(NB_)) + tuple(NL + b * TC + TC - 1 for b in range(NB_))


def _cp(sem):
    return pltpu.CompilerParams(dimension_semantics=sem, vmem_limit_bytes=VMEM_LIMIT)


def _dot(a, b):
    return jnp.dot(a, b, preferred_element_type=F32)


def _dot_nt(a, b):
    return lax.dot_general(a, b, (((1,), (1,)), ((), ())), preferred_element_type=F32)


def _dot_tn(a, b):
    return lax.dot_general(a, b, (((0,), (0,)), ((), ())), preferred_element_type=F32)


def _sigmoid(x):
    return 1.0 / (1.0 + jnp.exp(-x))


def _log_sigmoid(x):
    return jnp.minimum(x, 0.0) - jnp.log(1.0 + jnp.exp(-jnp.abs(x)))


def _rms(x, g):
    xf = x.astype(F32)
    return xf * lax.rsqrt(jnp.mean(xf * xf, axis=-1, keepdims=True) + EPS) * g


def _ada_kernel(c_ref, w_ref, b_ref, o_ref):
    c = c_ref[...]
    s = (c * _sigmoid(c)).astype(BF16)
    o_ref[0] = _dot(s, w_ref[0].astype(BF16)) + b_ref[0]


def _ada(cvec, w_ada, b_ada):
    depth = w_ada.shape[0]
    tn = 1024
    return pl.pallas_call(
        _ada_kernel,
        grid=(depth, 6 * D // tn),
        in_specs=[pl.BlockSpec((8, D), lambda l, j: (0, 0)),
                  pl.BlockSpec((1, D, tn), lambda l, j: (l, 0, j)),
                  pl.BlockSpec((1, 1, tn), lambda l, j: (l, 0, j))],
        out_specs=pl.BlockSpec((1, 8, tn), lambda l, j: (l, 0, j)),
        out_shape=jax.ShapeDtypeStruct((depth, 8, 6 * D), F32),
        compiler_params=_cp(("arbitrary", "arbitrary")),
        name="ada",
    )(cvec, w_ada, b_ada.reshape(depth, 1, 6 * D))


def _stream_specs(srcs, tm):
    if len(srcs) == 1:
        return [pl.BlockSpec((tm, D), lambda i: (i, 0))]
    nlat = NL // tm
    return [pl.BlockSpec((tm, D), lambda i: (jnp.minimum(i, nlat - 1), 0)),
            pl.BlockSpec((tm, D), lambda i: (jnp.maximum(i - nlat, 0), 0))]


def _stream_rows(i, tm, refs):
    if len(refs) == 1:
        return refs[0][...]
    return jnp.where(i < NL // tm, refs[0][...], refs[1][...])


def _prenorm_kernel(nsrc, *refs):
    g_ref, sh_ref, sc_ref, o_ref = refs[nsrc:]
    x = _stream_rows(pl.program_id(0), TM, refs[:nsrc])
    o_ref[...] = (_rms(x, g_ref[0]) * (1.0 + sc_ref[0]) + sh_ref[0]).astype(BF16)


def _prenorm(srcs, g, modrb, l):
    return pl.pallas_call(
        functools.partial(_prenorm_kernel, len(srcs)),
        grid=(NBLK,),
        in_specs=[*_stream_specs(srcs, TM),
                  pl.BlockSpec((1, 1, D), lambda i: (l, 0, 0)),
                  pl.BlockSpec((1, 1, D), lambda i: (i, 0, 0)),
                  pl.BlockSpec((1, 1, D), lambda i: (i, 0, 1))],
        out_specs=pl.BlockSpec((TM, D), lambda i: (i, 0)),
        out_shape=jax.ShapeDtypeStruct((R, D), BF16),
        compiler_params=_cp(("arbitrary",)),
        name="prenorm",
    )(*srcs, g, modrb, modrb)


IP_TN = 1024
IP_STARTS = ((OFF['cq'][0], OFF['xm'][0]) + tuple(OFF['gates'][0] + k * IP_TN for k in range(3 * D // IP_TN))
             + (OFF['om'][0],))
IP_SHIFTS = tuple(s % IP_TN for s in IP_STARTS)
assert all(s < 128 and s % 16 == 0 for s in IP_SHIFTS) and OFF['ckv'][0] == Q_LORA and PM_W == IP_TN * len(IP_STARTS)


def _ip_window(j):
    idx = IP_STARTS[-1] // IP_TN
    for jj in range(len(IP_STARTS) - 2, -1, -1):
        idx = jnp.where(j == jj, IP_STARTS[jj] // IP_TN, idx)
    return idx


def _inproj_kernel(h_ref, wm_ref, we_ref, o_ref, wb_ref):
    j = pl.program_id(0)

    @pl.when(pl.program_id(1) == 0)
    def _():
        for shift in sorted(set(IP_SHIFTS)):
            hit = functools.reduce(jnp.logical_or, [j == jj for jj, s in enumerate(IP_SHIFTS) if s == shift])

            @pl.when(hit)
            def _(shift=shift):
                wb_ref[0:IP_TN - shift, :] = wm_ref[0, shift:IP_TN, :].astype(BF16)
                if shift:
                    wb_ref[IP_TN - shift:IP_TN, :] = we_ref[0, 0:shift, :].astype(BF16)

    o_ref[...] = _dot_nt(h_ref[...], wb_ref[...]).astype(o_ref.dtype)


def _inproj(h, w_in_t, l):
    return pl.pallas_call(
        _inproj_kernel,
        grid=(len(IP_STARTS), NBLK),
        in_specs=[pl.BlockSpec((TM, D), lambda j, i: (i, 0)),
                  pl.BlockSpec((1, IP_TN, D), lambda j, i: (l, _ip_window(j), 0)),
                  pl.BlockSpec((1, 128, D), lambda j, i: (l, (_ip_window(j) + 1) * (IP_TN // 128), 0))],
        out_specs=pl.BlockSpec((TM, IP_TN), lambda j, i: (i, j)),
        out_shape=jax.ShapeDtypeStruct((R, PM_W), BF16),
        scratch_shapes=[pltpu.VMEM((IP_TN, D), BF16)],
        compiler_params=_cp(("arbitrary", "arbitrary")),
        name="inproj",
    )(h, w_in_t, w_in_t)


US0, KR0, GM0 = OFF['us'][0], OFF['krope'][0], OFF['gm'][0]
assert KR0 % ROPE == 0 and GM0 % (4 * H_M) == 0 and US0 % IP_TN < 128 and US0 % 16 == 0 and OFF['us'][1] == IP_TN


def _inproj_side_kernel(h_ref, wu_ref, wue_ref, wkr_ref, wgm_ref, us_ref, o_ref, wb_ref):
    @pl.when(pl.program_id(0) == 0)
    def _():
        wb_ref[...] = jnp.zeros_like(wb_ref)
        shift = US0 % IP_TN
        wb_ref[0:D_S - shift, :] = wu_ref[0, shift:IP_TN, :].astype(BF16)
        wb_ref[D_S - shift:D_S, :] = wue_ref[0, 0:shift, :].astype(BF16)
        kr = wkr_ref[0].astype(BF16)
        wb_ref[D_S + SM_KR:D_S + SM_KR + ROPE, :] = kr
        half = ROPE // 4
        for blk in range(ROPE // half):
            src = (blk ^ 1) * half
            wb_ref[D_S + SM_KP + blk * half:D_S + SM_KP + (blk + 1) * half, :] = kr[src:src + half, :]
        wb_ref[D_S + SM_GM:D_S + SM_GM + 4 * H_M, :] = wgm_ref[0].astype(BF16)

    res = _dot_nt(h_ref[...], wb_ref[...])
    for cb in range(NLB):
        us_ref[cb] = res[:, cb * 128:(cb + 1) * 128]
    o_ref[...] = res[:, D_S:]


def _inproj_side(h, w_in_t, l):
    once = pl.Buffered(1)
    wblk = lambda rows, row: pl.BlockSpec((1, rows, D), lambda i: (l, row // rows, 0), pipeline_mode=once)
    return pl.pallas_call(
        _inproj_side_kernel,
        grid=(NBLK,),
        in_specs=[pl.BlockSpec((TM, D), lambda i: (i, 0)),
                  wblk(IP_TN, US0 - US0 % IP_TN), wblk(128, US0 - US0 % IP_TN + IP_TN),
                  wblk(ROPE, KR0), wblk(4 * H_M, GM0)],
        out_specs=[pl.BlockSpec((NLB, TM, 128), lambda i: (0, i, 0)),
                   pl.BlockSpec((TM, SM_W), lambda i: (i, 0))],
        out_shape=[jax.ShapeDtypeStruct((NLB, R, 128), F32),
                   jax.ShapeDtypeStruct((R, SM_W), F32)],
        scratch_shapes=[pltpu.VMEM((D_S + SM_W, D), BF16)],
        compiler_params=_cp(("arbitrary",)),
        name="inproj_side",
    )(h, w_in_t, w_in_t, w_in_t, w_in_t)


def _mla_proj_kernel(cq_ref, ckv_ref, kr_ref, kp_ref, cos_ref, sin_ref, gq_ref, gkv_ref,
                     wqn_ref, wqr_ref, wqp_ref, wkn_ref, wvt_ref, q_ref, k_ref, vt_ref):
    hq = _rms(cq_ref[...], gq_ref[0]).astype(BF16)
    hk = _rms(ckv_ref[...], gkv_ref[0]).astype(BF16)
    cos = cos_ref[...]
    sin = sin_ref[...]
    qscale = ATTN_SCALE * LOG2E
    qn = _dot(hq, wqn_ref[0]) * qscale
    qr = _dot(hq, wqr_ref[0])
    qp = _dot(hq, wqp_ref[0])
    kn = _dot(hk, wkn_ref[0])
    vt = _dot_nt(wvt_ref[0], hk)
    kr = (kr_ref[...] * cos + kp_ref[...] * sin).astype(BF16)
    ones = jnp.ones((DV, TM), BF16)
    for h in range(H_A):
        lo = slice(h * 256, h * 256 + 128)
        hi = slice(h * 256 + 128, (h + 1) * 256)
        sl = slice(h * 128, (h + 1) * 128)
        q_ref[:, lo] = qn[:, sl].astype(BF16)
        q_ref[:, hi] = ((qr[:, sl] * cos + qp[:, sl] * sin) * qscale).astype(BF16)
        k_ref[:, lo] = kn[:, sl].astype(BF16)
        k_ref[:, hi] = kr
        vt_ref[lo, :] = vt[sl, :].astype(BF16)
        vt_ref[hi, :] = ones


def _mla_proj(p, psm, cosk, sink, gq, gkv, wqn, wqr, wqp, wkn, wvt, l):
    lw = lambda shp: pl.BlockSpec((1,) + shp, lambda i: (l, 0, 0))
    return pl.pallas_call(
        _mla_proj_kernel,
        grid=(NBLK,),
        in_specs=[pl.BlockSpec((TM, 512), lambda i: (i, PM_CQ // 512)),
                  pl.BlockSpec((TM, 512), lambda i: (i, PM_CKV // 512)),
                  pl.BlockSpec((TM, 128), lambda i: (i, SM_KR // 128)),
                  pl.BlockSpec((TM, 128), lambda i: (i, SM_KP // 128)),
                  pl.BlockSpec((TM, 128), lambda i: (i, 0)),
                  pl.BlockSpec((TM, 128), lambda i: (i, 0)),
                  pl.BlockSpec((1, 1, 512), lambda i: (l, 0, 0)),
                  pl.BlockSpec((1, 1, 512), lambda i: (l, 0, 0)),
                  lw((512, 1024)), lw((512, 1024)), lw((512, 1024)), lw((512, 1024)), lw((1024, 512))],
        out_specs=[pl.BlockSpec((TM, 2048), lambda i: (i, 0)),
                   pl.BlockSpec((TM, 2048), lambda i: (i, 0)),
                   pl.BlockSpec((2048, TM), lambda i: (0, i))],
        out_shape=[jax.ShapeDtypeStruct((R, 2048), BF16),
                   jax.ShapeDtypeStruct((R, 2048), BF16),
                   jax.ShapeDtypeStruct((2048, R), BF16)],
        compiler_params=_cp(("arbitrary",)),
        name="mla_proj",
    )(p, p, psm, psm, cosk, sink, gq, gkv, wqn, wqr, wqp, wkn, wvt)


def _attn_kernel(q_ref, kl_ref, kc_ref, vl_ref, vc_ref, o_ref, m_ref, acc_ref, st_ref):
    qi = pl.program_id(2)
    heads = [slice(hh * 256, (hh + 1) * 256) for hh in range(AH)]
    qs = [q_ref[:, hs] for hs in heads]

    def group(hh, k_chunks, vt, m, acc):
        mx = None
        nk = k_chunks[0].shape[0]
        for c, k in enumerate(k_chunks):
            st = _dot_nt(k, qs[hh])
            st_ref[hh, c, 0:nk, :] = st
            mx = st if mx is None else jnp.maximum(mx, st)
        m_new = jnp.maximum(m, jnp.max(mx, axis=0, keepdims=True))
        p = jnp.concatenate([jnp.exp2(st_ref[hh, c, 0:nk, :] - m_new).astype(BF16)
                             for c in range(len(k_chunks))], axis=0)
        return m_new, jnp.exp2(m - m_new) * acc + _dot(vt, p)

    for hh, hs in enumerate(heads):
        m, acc = group(hh, [kc_ref[:, hs]], vc_ref[hs, :], jnp.full((1, TQ), -jnp.inf, F32),
                       jnp.zeros((2 * DV, TQ), F32))
        m_ref[hh] = m
        acc_ref[hh] = acc

    @pl.when(qi < T // TQ)
    def _():
        st = [(m_ref[hh], acc_ref[hh]) for hh in range(AH)]
        for j in range(T // (TK * AG)):
            chunks = [slice((j * AG + c) * TK, (j * AG + c + 1) * TK) for c in range(AG)]
            span = slice(j * AG * TK, (j + 1) * AG * TK)
            st = [group(hh, [kl_ref[ks, hs] for ks in chunks], vl_ref[hs, span], *st[hh])
                  for hh, hs in enumerate(heads)]
        for hh in range(AH):
            acc_ref[hh] = st[hh][1]

    for hh in range(AH):
        acc = acc_ref[hh]
        o_ref[:, hh * DV:(hh + 1) * DV] = jnp.transpose(acc[0:DV, :] / acc[DV:2 * DV, :]).astype(o_ref.dtype)


def _attention(q, k, vt, ctx_queries):
    nq = T // TQ
    qrow = lambda b, h, i: jnp.where(i < nq, b * nq + i, NL // TQ + b)
    return pl.pallas_call(
        _attn_kernel,
        grid=(NB_, H_A // AH, nq + 1 if ctx_queries else nq),
        in_specs=[pl.BlockSpec((TQ, AH * 256), lambda b, h, i: (qrow(b, h, i), h)),
                  pl.BlockSpec((T, AH * 256), lambda b, h, i: (b, h)),
                  pl.BlockSpec((TC, AH * 256), lambda b, h, i: (NL // TC + b, h)),
                  pl.BlockSpec((AH * 256, T), lambda b, h, i: (h, b)),
                  pl.BlockSpec((AH * 256, TC), lambda b, h, i: (h, NL // TC + b))],
        out_specs=pl.BlockSpec((TQ, AH * DV), lambda b, h, i: (qrow(b, h, i), h)),
        out_shape=jax.ShapeDtypeStruct((R if ctx_queries else NL, H_A * DV), BF16),
        scratch_shapes=[pltpu.VMEM((AH, 1, TQ), F32), pltpu.VMEM((AH, 2 * DV, TQ), F32),
                        pltpu.VMEM((AH, AG, TK, TQ), F32)],
        compiler_params=_cp(("arbitrary", "arbitrary", "arbitrary")),
        name="attention",
    )(q, k, k, vt, vt)


def _row_in(grow, rows):
    hit = grow == rows[0]
    for r in rows[1:]:
        hit = jnp.logical_or(hit, grow == r)
    return hit


def _mprep_kernel(x_ref, xp_ref, xn_ref, cw_ref, wq_ref, q_ref, k_ref, v_ref):
    i = pl.program_id(0)
    xb = x_ref[...]
    x = xb.astype(F32)
    prev_row = xp_ref[15:16, :].astype(F32)
    next_row = xn_ref[0:1, :].astype(F32)
    row = lax.broadcasted_iota(jnp.int32, (TM, 1), 0)
    grow = row + i * TM
    xprev = jnp.where(row == 0, prev_row, pltpu.roll(x, 1, axis=0))
    xprev = jnp.where(_row_in(grow, SEQ_STARTS), 0.0, xprev)
    xnext = jnp.where(row == TM - 1, next_row, pltpu.roll(x, TM - 1, axis=0))
    xnext = jnp.where(_row_in(grow, SEQ_ENDS), 0.0, xnext)
    cw = cw_ref[0]
    xc = xprev * cw[0:1, :] + x * cw[1:2, :] + xnext * cw[2:3, :]
    xcb = (xc * _sigmoid(xc)).astype(BF16)
    ones = jnp.ones((DH_M, TM), BF16)
    for h in range(H_M):
        sl = slice(h * DH_M, (h + 1) * DH_M)
        q_ref[:, sl] = _dot(xcb[:, sl], wq_ref[0, 0, h]).astype(BF16)
        k_ref[:, sl] = (_dot(xcb[:, sl], wq_ref[0, 1, h]) * DH_M ** -0.5).astype(BF16)
        v_ref[2 * h * DH_M:(2 * h + 1) * DH_M, :] = _dot_nt(wq_ref[0, 2, h], xb[:, sl]).astype(BF16)
        v_ref[(2 * h + 1) * DH_M:(2 * h + 2) * DH_M, :] = ones


def _mprep(p, conv_w, w_qkv, l):
    hb = TM // 16
    xcol = PM_XM // D_M
    return pl.pallas_call(
        _mprep_kernel,
        grid=(NBLK,),
        in_specs=[pl.BlockSpec((TM, D_M), lambda i: (i, xcol)),
                  pl.BlockSpec((16, D_M), lambda i: (jnp.maximum(i * hb - 1, 0), xcol)),
                  pl.BlockSpec((16, D_M), lambda i: (jnp.minimum((i + 1) * hb, R // 16 - 1), xcol)),
                  pl.BlockSpec((1, 3, D_M), lambda i: (l, 0, 0)),
                  pl.BlockSpec((1, 3, H_M, DH_M, DH_M), lambda i: (l, 0, 0, 0, 0))],
        out_specs=[pl.BlockSpec((TM, D_M), lambda i: (i, 0)),
                   pl.BlockSpec((TM, D_M), lambda i: (i, 0)),
                   pl.BlockSpec((2 * D_M, TM), lambda i: (0, i))],
        out_shape=[jax.ShapeDtypeStruct((R, D_M), BF16),
                   jax.ShapeDtypeStruct((R, D_M), BF16),
                   jax.ShapeDtypeStruct((2 * D_M, R), BF16)],
        compiler_params=_cp(("arbitrary",)),
        name="mlstm_prep",
    )(p, p, p, conv_w, w_qkv)


def _split3(x):
    h = x.astype(BF16)
    r = x - h.astype(F32)
    m = r.astype(BF16)
    l = (r - m.astype(F32)).astype(BF16)
    return h, m, l


def _mlstm_kernel(q_ref, k_ref, v_ref, g_ref, gt_ref, b_ref, bt_ref, h_ref, c_ref, m_ref):
    d = pl.program_id(1)
    s = pl.program_id(2)

    @pl.when(s == 0)
    def _():
        c_ref[...] = jnp.zeros_like(c_ref)
        m_ref[...] = jnp.zeros_like(m_ref)

    sign = 1 - 2 * d
    r = lax.broadcasted_iota(jnp.int32, (LC, LC), 0)
    c = lax.broadcasted_iota(jnp.int32, (LC, LC), 1)
    before = (c - r) * sign >= 0
    tri = jnp.where((r - c) * sign >= 0, 1.0, 0.0).astype(BF16)
    g = g_ref[0] + b_ref[0]
    gt = gt_ref[0] + bt_ref[0]
    ig_c = g[:, 0:H_M] * LOG2E
    lf_c = _log_sigmoid(g[:, H_M:2 * H_M]) * LOG2E
    lf_r = _log_sigmoid(gt[H_M:2 * H_M, :]) * LOG2E
    b_c = sum(_dot(tri, part) for part in _split3(lf_c))
    b_r = sum(_dot_nt(part, tri) for part in _split3(lf_r))
    tot = jnp.sum(lf_c, axis=0, keepdims=True)
    src = ig_c - b_c

    for h in range(H_M):
        sl = slice(h * DH_M, (h + 1) * DH_M)
        qh = q_ref[:, sl]
        kh = k_ref[:, sl]
        vth = v_ref[2 * h * DH_M:(2 * h + 2) * DH_M, :]
        bt = b_r[h:h + 1, :]
        tt = tot[:, h:h + 1]
        mp = m_ref[h:h + 1, 0:1]
        dm = jnp.where(before, bt + src[:, h:h + 1], -jnp.inf)
        m_inter = bt + mp
        m_t = jnp.maximum(jnp.max(dm, axis=0, keepdims=True), m_inter)
        sc = _dot_nt(kh, qh) * jnp.exp2(dm - m_t)
        inter = jnp.exp2(m_inter - m_t)
        st = c_ref[h]
        both = _dot(vth, sc.astype(BF16)) + inter * _dot_nt(st.astype(BF16), qh)
        den = both[DH_M:DH_M + 1, :]
        ht = both[0:DH_M, :] / jnp.maximum(jnp.abs(den), jnp.exp2(-m_t))
        h_ref[0, :, sl] = jnp.transpose(ht)
        gg = tt + src[:, h:h + 1]
        m_new = jnp.maximum(tt + mp, jnp.max(gg, axis=0, keepdims=True))
        kw = (kh.astype(F32) * jnp.exp2(gg - m_new)).astype(BF16)
        c_ref[h] = jnp.exp2(tt + mp - m_new) * st + _dot(vth, kw)
        m_ref[h:h + 1, :] = jnp.broadcast_to(m_new, (1, DH_M))


def _mlstm(q, k, v, gd, gtd, bd, btd):
    nl = T // LC
    nsteps = nl + TC // LC

    def rb(b, d, s):
        lat = b * nl + jnp.where(d == 0, s - 1, nl - s)
        return jnp.where(s == 0, NL // LC + b, lat)

    qspec = pl.BlockSpec((LC, D_M), lambda b, d, s: (rb(b, d, s), 0))
    return pl.pallas_call(
        _mlstm_kernel,
        grid=(NB_, 2, nsteps),
        in_specs=[qspec, qspec,
                  pl.BlockSpec((2 * D_M, LC), lambda b, d, s: (0, rb(b, d, s))),
                  pl.BlockSpec((1, LC, 16), lambda b, d, s: (d, rb(b, d, s), 0)),
                  pl.BlockSpec((1, 16, LC), lambda b, d, s: (d, 0, rb(b, d, s))),
                  pl.BlockSpec((1, 1, 16), lambda b, d, s: (d, 0, 0)),
                  pl.BlockSpec((1, 16, 1), lambda b, d, s: (d, 0, 0))],
        out_specs=pl.BlockSpec((1, LC, D_M), lambda b, d, s: (d, rb(b, d, s), 0)),
        out_shape=jax.ShapeDtypeStruct((2, R, D_M), F32),
        scratch_shapes=[pltpu.VMEM((H_M, 2 * DH_M, DH_M), F32),
                        pltpu.VMEM((H_M, DH_M), F32)],
        compiler_params=_cp(("arbitrary", "arbitrary", "arbitrary")),
        name="mlstm_scan",
    )(q, k, v, gd, gtd, bd, btd)


S5_NLAT = T // SL
S5_NCTX = TC // SL
S5_SEG = S5_NLAT // NSEG


def _fold_copy(tok_ref, fold_ref, to_fold):
    def move(tok_idx, fold_rows, t):
        lanes = slice(t * 128, (t + 1) * 128)
        if to_fold:
            fold_ref[fold_rows, lanes] = tok_ref[tok_idx, :]
        else:
            tok_ref[tok_idx, :] = fold_ref[fold_rows, lanes]

    def body(j, carry):
        for b in range(NB_):
            rows = pl.ds(pl.multiple_of(b * S5_NLAT + j * NSEG, NSEG), NSEG)
            for t in range(SL):
                move(pl.ds(b * T + j * SL + t, NSEG, stride=S5_SEG * SL), rows, t)
        return carry

    lax.fori_loop(0, S5_SEG, body, 0)
    for t in range(SL):
        move(pl.ds(NL + t, NCX // SL, stride=SL), slice(NL // SL, NCH), t)


def _expand_block_diag(compact_ref, dense_ref, key_shift, key_mul, key_mask, row_gshift, col_gshift):
    comp = compact_ref[0]
    nrows, kc = comp.shape
    chunk = 512
    r = lax.broadcasted_iota(jnp.int32, (kc, chunk), 0)
    rg = (lax.broadcasted_iota(jnp.int32, (nrows, 1), 0) >> row_gshift) & (SLB - 1)
    for c0 in range(0, dense_ref.shape[1], chunk):
        q = lax.broadcasted_iota(jnp.int32, (kc, chunk), 1) + c0
        sel = jnp.where(r == (q >> key_shift) * key_mul + (q & key_mask), 1.0, 0.0).astype(BF16)
        cg = ((lax.broadcasted_iota(jnp.int32, (1, chunk), 1) + c0) >> col_gshift) & (SLB - 1)
        dense_ref[:, c0:c0 + chunk] = jnp.where(rg == cg, _dot(comp, sel), 0.0).astype(BF16)


def _cmul(ar, ai, sr, si):
    return ar * sr - ai * si, ar * si + ai * sr


def _s5_state_kernel(us_ref, wst_ref, a_ref, s_ref, e_ref, x_ref, w_ref):
    _fold_copy(us_ref.at[0], x_ref, True)
    _expand_block_diag(wst_ref, w_ref, 9, P_S, P_S - 1, 4, 6)
    e_ref[...] = _dot(x_ref[...].astype(BF16), w_ref[...])
    apow = a_ref[0]
    step_a = ((apow[0:1, :], apow[1:2, :]), (apow[2:3, :], apow[3:4, :]))
    nctx, seg = S5_NCTX, S5_SEG

    def advance(rows, d, st, write):
        cr = slice(2 * d * SW, (2 * d + 1) * SW)
        ci = slice((2 * d + 1) * SW, (2 * d + 2) * SW)
        dr = e_ref[rows, cr]
        di = e_ref[rows, ci]
        if write:
            e_ref[rows, cr] = st[0]
            e_ref[rows, ci] = st[1]
        nr, ni = _cmul(step_a[d][0], step_a[d][1], st[0], st[1])
        return nr + dr, ni + di

    def sweep(nsteps, rows_of, init, write):
        def body(kk, sts):
            out = []
            for b in range(NB_):
                out.append(advance(rows_of(b, kk), 0, sts[2 * b], write))
                out.append(advance(rows_of(b, nsteps - 1 - kk), 1, sts[2 * b + 1], write))
            return tuple(out)
        return lax.fori_loop(0, nsteps, body, init)

    z1 = jnp.zeros((1, SW), F32)
    carry = sweep(nctx, lambda b, kk: pl.ds(NL // SL + b * nctx + kk, 1), ((z1, z1),) * (2 * NB_), True)

    seg_rows = lambda b, kk: pl.ds(pl.multiple_of(b * S5_NLAT + kk * NSEG, NSEG), NSEG)
    z8 = jnp.zeros((NSEG, SW), F32)
    ends = sweep(seg, seg_rows, ((z8, z8),) * (2 * NB_), False)
    inits = []
    for ch in range(2 * NB_):
        d = ch % 2
        ar, ai = step_a[d]
        for _ in range(int(math.log2(seg))):
            ar, ai = ar * ar - ai * ai, 2.0 * ar * ai
        cur = carry[ch]
        rows_r, rows_i = [None] * NSEG, [None] * NSEG
        for kseg in (range(NSEG) if d == 0 else range(NSEG - 1, -1, -1)):
            rows_r[kseg], rows_i[kseg] = cur
            nr, ni = _cmul(ar, ai, cur[0], cur[1])
            cur = (nr + ends[ch][0][kseg:kseg + 1, :], ni + ends[ch][1][kseg:kseg + 1, :])
        inits.append((jnp.concatenate(rows_r, axis=0), jnp.concatenate(rows_i, axis=0)))
    sweep(seg, seg_rows, tuple(inits), True)
    s_ref[0] = e_ref[...].astype(BF16)


def _s5_state(us, wst, apow, l):
    return pl.pallas_call(
        _s5_state_kernel,
        grid=(NLB,),
        in_specs=[pl.BlockSpec((1, R, 128), lambda g: (g, 0, 0), pipeline_mode=pl.Buffered(1)),
                  pl.BlockSpec((1, SL * 128, 4 * P_S), lambda g: (l * NLB + g, 0, 0)),
                  pl.BlockSpec((1, 8, SW), lambda g: (l * NLB + g, 0, 0))],
        out_specs=pl.BlockSpec((1, NCH, 4 * SW), lambda g: (g, 0, 0)),
        out_shape=jax.ShapeDtypeStruct((NLB, NCH, 4 * SW), BF16),
        scratch_shapes=[pltpu.VMEM((NCH, 4 * SW), F32),
                        pltpu.VMEM((NCH, SL * 128), F32),
                        pltpu.VMEM((SL * 128, 4 * SW), BF16)],
        compiler_params=_cp(("arbitrary",)),
        name="s5_state",
    )(us, wst, apow)


def _s5_out_kernel(us_ref, toep_ref, s_ref, wo_ref, d_ref, y_ref, x_ref, tw_ref, ow_ref):
    _fold_copy(us_ref.at[0], x_ref, True)
    _expand_block_diag(toep_ref, tw_ref, 7, GROUP, GROUP - 1, 4, 4)
    _expand_block_diag(wo_ref, ow_ref, 7, GROUP, GROUP - 1, 6, 4)
    x = x_ref[...]
    y = _dot(x.astype(BF16), tw_ref[...]) + _dot(s_ref[0], ow_ref[...]) + d_ref[0] * x
    x_ref[...] = jax.nn.gelu(y)
    _fold_copy(y_ref.at[0], x_ref, False)


def _s5_out(us, toep, s, wout, dflat, l):
    return pl.pallas_call(
        _s5_out_kernel,
        grid=(NLB,),
        in_specs=[pl.BlockSpec((1, R, 128), lambda g: (g, 0, 0), pipeline_mode=pl.Buffered(1)),
                  pl.BlockSpec((1, SL * 128, SL * GROUP), lambda g: (l * NLB + g, 0, 0)),
                  pl.BlockSpec((1, NCH, 4 * SW), lambda g: (g, 0, 0), pipeline_mode=pl.Buffered(1)),
                  pl.BlockSpec((1, 4 * SW, SL * GROUP), lambda g: (l * NLB + g, 0, 0)),
                  pl.BlockSpec((1, 1, SL * 128), lambda g: (l * NLB + g, 0, 0))],
        out_specs=pl.BlockSpec((1, R, 128), lambda g: (g, 0, 0)),
        out_shape=jax.ShapeDtypeStruct((NLB, R, 128), F32),
        scratch_shapes=[pltpu.VMEM((NCH, SL * 128), F32),
                        pltpu.VMEM((SL * 128, SL * 128), BF16),
                        pltpu.VMEM((4 * SW, SL * 128), BF16)],
        compiler_params=_cp(("arbitrary",)),
        name="s5_out",
    )(us, toep, s, wout, dflat)


def _glu_kernel(y_ref, w_ref, o_ref):
    y = jnp.concatenate([y_ref[cb] for cb in range(NLB)], axis=1).astype(BF16)
    z = _dot(y, w_ref[0])
    o_ref[...] = (z[:, :D_S] * _sigmoid(z[:, D_S:])).astype(BF16)


def _glu(y, w, l):
    return pl.pallas_call(
        _glu_kernel,
        grid=(NBLK,),
        in_specs=[pl.BlockSpec((NLB, TM, 128), lambda i: (0, i, 0)),
                  pl.BlockSpec((1, D_S, 2 * D_S), lambda i: (l, 0, 0))],
        out_specs=pl.BlockSpec((TM, D_S), lambda i: (i, 0)),
        out_shape=jax.ShapeDtypeStruct((R, D_S), BF16),
        compiler_params=_cp(("arbitrary",)),
        name="s5_glu",
    )(y, w)


def _s5_params(a_re, a_im, log_dt, b_re, b_im, c_re, c_im, dskip):
    dt = jnp.exp(log_dt)[:, :, None]
    lam_r, lam_i = a_re * dt, a_im * dt
    mag = jnp.exp(lam_r)
    ar, ai = mag * jnp.cos(lam_i), mag * jnp.sin(lam_i)
    den = a_re * a_re + a_im * a_im
    nr, ni = ar - 1.0, ai
    cr = (nr * a_re + ni * a_im) / den
    ci = (ni * a_re - nr * a_im) / den
    cpr = c_re[None] * cr[:, :, None, :] - c_im[None] * ci[:, :, None, :]
    cpi = c_re[None] * ci[:, :, None, :] + c_im[None] * cr[:, :, None, :]
    j = jnp.arange(SL + 1, dtype=F32)[:, None, None, None]
    pm = jnp.exp(lam_r[None] * j)
    pr, pi = pm * jnp.cos(lam_i[None] * j), pm * jnp.sin(lam_i[None] * j)
    bt_re, bt_im = b_re.transpose(0, 2, 1), b_im.transpose(0, 2, 1)
    abr = pr[:, :, :, None, :] * bt_re[None, None] - pi[:, :, :, None, :] * bt_im[None, None]
    abi = pr[:, :, :, None, :] * bt_im[None, None] + pi[:, :, :, None, :] * bt_re[None, None]
    kern = (jnp.einsum('dgcp,jdgep->djgec', cpr, abr[:SL]) - jnp.einsum('dgcp,jdgep->djgec', cpi, abi[:SL]))
    zero = jnp.zeros_like(kern[0, 0])
    rows = []
    for t_in in range(SL):
        blks = []
        for t_out in range(SL):
            blk = zero
            if t_out >= t_in:
                blk = blk + kern[0, t_out - t_in]
            if t_in >= t_out:
                blk = blk + kern[1, t_in - t_out]
            blks.append(blk)
        rows.append(jnp.stack(blks, axis=2))
    by_block = lambda a, lead, minor: a.reshape(lead + (NLB, SLB) + minor)
    t5 = by_block(jnp.stack(rows, axis=0), (SL,), (GROUP, SL * GROUP))
    toep = t5.transpose(1, 0, 2, 3, 4).reshape(NLB, SL * 128, SL * GROUP)
    rev = lambda a, lo: jnp.stack([a[lo + SL - 1 - t] for t in range(SL)], axis=0)
    ws = jnp.stack([rev(abr[:, 0], 0), rev(abi[:, 0], 0), abr[:SL, 1], abi[:SL, 1]], axis=3)
    ws = by_block(ws, (SL,), (GROUP, 4 * P_S))
    wst = ws.transpose(1, 0, 2, 3, 4).reshape(NLB, SL * 128, 4 * P_S)
    def readout(pw_r, pw_i, d):
        cr_t, ci_t = cpr[d].transpose(0, 2, 1)[:, :, None, :], cpi[d].transpose(0, 2, 1)[:, :, None, :]
        wr, wi = pw_r.transpose(1, 2, 0)[..., None], pw_i.transpose(1, 2, 0)[..., None]
        return cr_t * wr - ci_t * wi, -(cr_t * wi + ci_t * wr)
    of_re, of_im = readout(pr[1:SL + 1, 0], pi[1:SL + 1, 0], 0)
    ob_re, ob_im = readout(rev(pr[:, 1], 1), rev(pi[:, 1], 1), 1)
    wo = by_block(jnp.stack([of_re, of_im, ob_re, ob_im], axis=0), (4,), (P_S, SL * GROUP))
    wout = wo.transpose(1, 0, 2, 3, 4).reshape(NLB, 4 * SW, SL * GROUP)
    blk = lambda a: a.reshape(NLB, 1, SW)
    apow = jnp.concatenate([blk(pr[SL, 0]), blk(pi[SL, 0]), blk(pr[SL, 1]), blk(pi[SL, 1]),
                            jnp.zeros((NLB, 4, SW), F32)], axis=1)
    dflat = jnp.tile(dskip.reshape(NLB, 1, 128), (1, 1, SL))
    return toep.astype(BF16), wst.astype(BF16), wout.astype(BF16), apow, dflat


def _mix_out_kernel(nsrc, tm, *refs):
    (a_ref, hf_ref, hb_ref, om_ref, gh_ref, s_ref, ga_ref, gm_ref, gs_ref, wa_ref, wm_ref, ws_ref, wo_ref,
     gt_ref, gf_ref, sh_ref, sc_ref, x1_ref, h2_ref) = refs[nsrc:]
    hsum = hf_ref[0] + hb_ref[0]
    parts = []
    for h in range(H_M):
        xh = hsum[:, h * DH_M:(h + 1) * DH_M]
        parts.append(xh * lax.rsqrt(jnp.mean(xh * xh, axis=-1, keepdims=True) + EPS))
    hn = jnp.concatenate(parts, axis=1) * gh_ref[0]
    m = (_sigmoid(om_ref[...].astype(F32)) * hn).astype(BF16)
    t = (_sigmoid(ga_ref[...].astype(F32)) * _dot(a_ref[...], wa_ref[0])
         + _sigmoid(gm_ref[...].astype(F32)) * _dot(m, wm_ref[0])
         + _sigmoid(gs_ref[...].astype(F32)) * _dot(s_ref[...], ws_ref[0]))
    x1 = _stream_rows(pl.program_id(0), tm, refs[:nsrc]) + gt_ref[0] * _dot(t.astype(BF16), wo_ref[0])
    x1_ref[...] = x1
    h2_ref[...] = (_rms(x1, gf_ref[0]) * (1.0 + sc_ref[0]) + sh_ref[0]).astype(BF16)


def _mix_out(a, hdir, p, gh, s, wa, wm, ws, wo, srcs, modrb, g_ffn, l, nrows):
    tm = 256
    per = TM // tm
    gcol = PM_GATES // D
    row = lambda w, c: pl.BlockSpec((tm, w), lambda i: (i, c))
    once = pl.Buffered(1)
    wspec = pl.BlockSpec((1, 1024, D), lambda i: (l, 0, 0), pipeline_mode=once)
    mod = lambda c: pl.BlockSpec((1, 1, D), lambda i: (i // per, 0, c))
    return pl.pallas_call(
        functools.partial(_mix_out_kernel, len(srcs), tm),
        grid=(nrows // tm,),
        in_specs=[*_stream_specs(srcs, tm),
                  row(1024, 0),
                  pl.BlockSpec((1, tm, D_M), lambda i: (0, i, 0)),
                  pl.BlockSpec((1, tm, D_M), lambda i: (1, i, 0)),
                  row(1024, PM_OM // 1024),
                  pl.BlockSpec((1, 1, D_M), lambda i: (l, 0, 0)),
                  row(1024, 0),
                  row(D, gcol), row(D, gcol + 1), row(D, gcol + 2),
                  wspec, wspec, wspec,
                  pl.BlockSpec((1, D, D), lambda i: (l, 0, 0), pipeline_mode=once),
                  mod(2),
                  pl.BlockSpec((1, 1, D), lambda i: (l, 0, 0)), mod(3), mod(4)],
        out_specs=[row(D, 0), row(D, 0)],
        out_shape=[jax.ShapeDtypeStruct((nrows, D), F32), jax.ShapeDtypeStruct((nrows, D), BF16)],
        compiler_params=_cp(("arbitrary",)),
        name="mix_out",
    )(*srcs, a, hdir, hdir, p, gh, s, p, p, p, wa, wm, ws, wo, modrb, g_ffn, modrb, modrb)


def _resid_kernel(t_ref, w_ref, x_ref, gt_ref, o_ref, wb_ref):
    @pl.when(pl.program_id(1) == 0)
    def _():
        wb_ref[...] = w_ref[0].astype(BF16)

    o_ref[...] = x_ref[...] + gt_ref[0] * _dot(t_ref[...], wb_ref[...])


def _resid(t, w, x, modrb, gate_chunk, tn, l, nrows, name):
    kdim = t.shape[1]
    nj = D // tn
    return pl.pallas_call(
        _resid_kernel,
        grid=(nj, nrows // TM),
        in_specs=[pl.BlockSpec((TM, kdim), lambda j, i: (i, 0)),
                  pl.BlockSpec((1, kdim, tn), lambda j, i: (l, 0, j)),
                  pl.BlockSpec((TM, tn), lambda j, i: (i, j)),
                  pl.BlockSpec((1, 1, tn), lambda j, i: (i, 0, gate_chunk * nj + j))],
        out_specs=pl.BlockSpec((TM, tn), lambda j, i: (i, j)),
        out_shape=jax.ShapeDtypeStruct((nrows, D), F32),
        scratch_shapes=[pltpu.VMEM((kdim, tn), BF16)],
        compiler_params=_cp(("arbitrary", "arbitrary")),
        name=name,
    )(t, w, x, modrb)


def _ffn_in_kernel(h_ref, wa_ref, wb_ref, o_ref, was_ref, wbs_ref):
    @pl.when(pl.program_id(1) == 0)
    def _():
        was_ref[...] = wa_ref[0].astype(BF16)
        wbs_ref[...] = wb_ref[0].astype(BF16)

    h = h_ref[...]
    a = _dot(h, was_ref[...])
    b = _dot(h, wbs_ref[...])
    o_ref[...] = (a * _sigmoid(a) * b).astype(BF16)


def _ffn_in(h, w, l, nrows):
    tn = 512
    nj = D_FF // tn
    return pl.pallas_call(
        _ffn_in_kernel,
        grid=(nj, nrows // TM),
        in_specs=[pl.BlockSpec((TM, D), lambda j, i: (i, 0)),
                  pl.BlockSpec((1, D, tn), lambda j, i: (l, 0, j)),
                  pl.BlockSpec((1, D, tn), lambda j, i: (l, 0, nj + j))],
        out_specs=pl.BlockSpec((TM, tn), lambda j, i: (i, j)),
        out_shape=jax.ShapeDtypeStruct((nrows, D_FF), BF16),
        scratch_shapes=[pltpu.VMEM((D, tn), BF16), pltpu.VMEM((D, tn), BF16)],
        compiler_params=_cp(("arbitrary", "arbitrary")),
        name="ffn_in",
    )(h, w, w)


def _final_norm_kernel(x_ref, g_ref, o_ref):
    o_ref[...] = _rms(x_ref[...], g_ref[...])


def _final_norm(x, g):
    return pl.pallas_call(
        _final_norm_kernel,
        grid=(NL // TM,),
        in_specs=[pl.BlockSpec((TM, D), lambda i: (i, 0)),
                  pl.BlockSpec((1, D), lambda i: (0, 0))],
        out_specs=pl.BlockSpec((TM, D), lambda i: (i, 0)),
        out_shape=jax.ShapeDtypeStruct((NL, D), F32),
        compiler_params=_cp(("arbitrary",)),
        name="final_norm",
    )(x, g.reshape(1, D))


def _rope_tables():
    f32 = np.float32
    rows = T // GRID_W
    rr, cc = np.meshgrid(np.arange(rows, dtype=f32), np.arange(GRID_W, dtype=f32), indexing='ij')
    rr, cc = rr.reshape(-1), cc.reshape(-1)
    half = ROPE // 2
    inv = (f32(1.0) / (f32(ROPE_THETA) ** (np.arange(0, half, 2, dtype=f32) / f32(half)))).astype(f32)
    ang = np.stack([rr[:, None] * inv, cc[:, None] * inv], axis=1).astype(f32)
    cos = np.cos(ang).astype(f32)
    sin = np.sin(ang).astype(f32)
    cos_f = np.stack([cos, cos], axis=2).reshape(T, ROPE)
    sin_f = np.stack([-sin, sin], axis=2).reshape(T, ROPE)
    pad = lambda a: np.concatenate([a, np.zeros((a.shape[0], 128 - ROPE), f32)], axis=1)
    cos_l, sin_l = pad(cos_f), pad(sin_f)
    cos_c = pad(np.ones((NCX, ROPE), f32))
    sin_c = np.zeros((NCX, 128), f32)
    return (jnp.asarray(np.concatenate([cos_l] * NB_ + [cos_c], axis=0)),
            jnp.asarray(np.concatenate([sin_l] * NB_ + [sin_c], axis=0)))


def _rope_partner(w):
    s = w.shape[:-1]
    w4 = w.reshape(s + (2, 2, ROPE // 4))
    return jnp.concatenate([w4[..., 1:2, :], w4[..., 0:1, :]], axis=-2).reshape(s + (ROPE,))


def _prep_weights(w_uq, w_ukv):
    depth = w_uq.shape[0]
    uq = w_uq.reshape(depth, Q_LORA, H_A, NOPE + ROPE)
    wqn = uq[..., :NOPE].reshape(depth, Q_LORA, H_A * NOPE)
    qr = uq[..., NOPE:]
    zq = jnp.zeros_like(qr)
    wqr = jnp.concatenate([qr, zq], axis=-1).reshape(depth, Q_LORA, H_A * 128)
    wqp = jnp.concatenate([_rope_partner(qr), zq], axis=-1).reshape(depth, Q_LORA, H_A * 128)
    ukv = w_ukv.reshape(depth, KV_LORA, H_A, NOPE + DV)
    wkn = ukv[..., :NOPE].reshape(depth, KV_LORA, H_A * NOPE)
    wvt = ukv[..., NOPE:].reshape(depth, KV_LORA, H_A * DV).transpose(0, 2, 1)
    return tuple(a.astype(BF16) for a in (wqn, wqr, wqp, wkn, wvt))


def kernel(x, c, ctx, c_ctx, w_ada, b_ada, g_mix, g_ffn, w_in, g_cq, w_uq, g_ckv, w_ukv, conv_m, w_qkv_m, b_gate_m, g_h_m, s5_a_re, s5_a_im, s5_log_dt, s5_b_re, s5_b_im, s5_c_re, s5_c_im, s5_d, w_glu, w_br_a, w_br_m, w_br_s, w_out, w_ffn_in, w_ffn_out, g_final):
    depth = w_ada.shape[0]
    xs = (x.reshape(NL, D), ctx.reshape(NCX, D))
    cvec = jnp.concatenate([c, c_ctx[None], jnp.zeros((8 - NB_ - 1, D), F32)], axis=0)
    mod = _ada(cvec, w_ada, b_ada)
    blocks_per_batch = T // TM
    cosk, sink = _rope_tables()
    mla_w = _prep_weights(w_uq, w_ukv)
    w_in_t = jnp.swapaxes(w_in, 1, 2)
    toep, wst, wout, apow, dflat = (a.reshape((depth * NLB,) + a.shape[2:]) for a in jax.vmap(_s5_params)(
        s5_a_re, s5_a_im, s5_log_dt, s5_b_re, s5_b_im, s5_c_re, s5_c_im, s5_d))
    w_out_b = w_out.astype(BF16)
    w_qkv_b = jnp.concatenate([w_qkv_m[:, :2], jnp.swapaxes(w_qkv_m[:, 2:], -1, -2)], axis=1).astype(BF16)
    w_glu_b = w_glu.astype(BF16)
    w_br_a_b, w_br_m_b, w_br_s_b = w_br_a.astype(BF16), w_br_m.astype(BF16), w_br_s.astype(BF16)
    gain = lambda g: g.reshape(depth, 1, g.shape[-1])
    g_mix, g_ffn, g_cq, g_ckv, g_h_m = gain(g_mix), gain(g_ffn), gain(g_cq), gain(g_ckv), gain(g_h_m)

    for l in range(depth):
        modrb = jnp.concatenate(
            [jnp.broadcast_to(mod[l, b:b + 1], (blocks_per_batch, 6 * D)) for b in range(NB_)]
            + [jnp.broadcast_to(mod[l, NB_:NB_ + 1], (NCX // TM, 6 * D))], axis=0).reshape(NBLK, 1, 6 * D)
        h1 = _prenorm(xs, g_mix, modrb, l)
        p = _inproj(h1, w_in_t, l)
        us, psm = _inproj_side(h1, w_in_t, l)

        last = l == depth - 1
        nrows = NL if last else R
        q, k, vt = _mla_proj(p, psm, cosk, sink, g_cq, g_ckv, *mla_w, l)
        a = _attention(q, k, vt, not last)

        qm, km, vm = _mprep(p, conv_m, w_qkv_b, l)
        gm = psm[:, SM_GM:SM_GM + 4 * H_M]
        gd = jnp.stack([gm[:, :2 * H_M], gm[:, 2 * H_M:]], axis=0)
        bd = b_gate_m[l].reshape(2, 1, 2 * H_M)
        hdir = _mlstm(qm, km, vm, gd, gd.transpose(0, 2, 1), bd, bd.transpose(0, 2, 1))

        st = _s5_state(us, wst, apow, l)
        y = _s5_out(us, toep, st, wout, dflat, l)
        s = _glu(y, w_glu_b, l)

        x1, h2 = _mix_out(a, hdir, p, g_h_m, s, w_br_a_b, w_br_m_b, w_br_s_b, w_out_b, xs, modrb, g_ffn, l, nrows)
        u = _ffn_in(h2, w_ffn_in, l, nrows)
        xs = (_resid(u, w_ffn_out, x1, modrb, 5, 512, l, nrows, "resid_ffn"),)

    return _final_norm(xs[0], g_final).reshape(NB_, T, D)
```

```python
import functools
import math

import numpy as np
import jax
import jax.numpy as jnp
from jax import lax
from jax.experimental import pallas as pl
from jax.experimental.pallas import tpu as pltpu

F32 = jnp.float32
BF16 = jnp.bfloat16

D = 2048
NB_ = 2
T = 4096
TC = 256
GRID_W = 64
EPS = 1e-6
H_A, Q_LORA, KV_LORA, NOPE, ROPE, DV = 8, 512, 512, 128, 64, 128
ROPE_THETA = 10000.0
ATTN_SCALE = (NOPE + ROPE) ** -0.5
H_M, DH_M = 8, 128
D_M = H_M * DH_M
D_S, GROUP, P_S = 1024, 16, 64
G_S = D_S // GROUP
D_FF = ((8 * D // 3 + 255) // 256) * 256
OFF = {}
_o = 0
for _n, _w in (('cq', Q_LORA), ('ckv', KV_LORA), ('krope', ROPE), ('xm', D_M), ('om', D_M),
               ('gm', 4 * H_M), ('us', D_S), ('gates', 3 * D)):
    OFF[_n] = (_o, _w)
    _o += _w

NL = NB_ * T
NCX = NB_ * TC
R = NL + NCX
TM = 512
NBLK = R // TM
LC = 256
TQ = 256
TK = 512
AH = 4
AG = 2
SL = 8
NCH = R // SL
NSEG = 8
SLB = 128 // GROUP
NLB = D_S // 128
SW = SLB * P_S
LOG2E = math.log2(math.e)
PM_CQ, PM_CKV, PM_XM, PM_GATES, PM_OM, PM_W = 0, 512, 1024, 2048, 2048 + 3 * D, 3072 + 3 * D
SM_KR, SM_KP, SM_GM, SM_W = 0, 128, 256, 384
VMEM_LIMIT = 56 * 1024 * 1024
SEQ_STARTS = tuple(b * T for b in range(NB_)) + tuple(NL + b * TC for b in range(NB_))
SEQ_ENDS = tuple(b * T + T - 1 for b in range(NB_)) + tuple(NL + b * TC + TC - 1 for b in range(NB_))


def _cp(sem):
    return pltpu.CompilerParams(dimension_semantics=sem, vmem_limit_bytes=VMEM_LIMIT)


def _dot(a, b):
    return jnp.dot(a, b, preferred_element_type=F32)


def _dot_nt(a, b):
    return lax.dot_general(a, b, (((1,), (1,)), ((), ())), preferred_element_type=F32)


def _sigmoid(x):
    return 1.0 / (1.0 + jnp.exp(-x))


def _log_sigmoid(x):
    return jnp.minimum(x, 0.0) - jnp.log(1.0 + jnp.exp(-jnp.abs(x)))


def _rms(x, g):
    xf = x.astype(F32)
    return xf * lax.rsqrt(jnp.mean(xf * xf, axis=-1, keepdims=True) + EPS) * g


def _ada_kernel(c_ref, w_ref, b_ref, o_ref):
    c = c_ref[...]
    s = (c * _sigmoid(c)).astype(BF16)
    o_ref[0] = _dot(s, w_ref[0].astype(BF16)) + b_ref[0]


def _ada(cvec, w_ada, b_ada):
    depth = w_ada.shape[0]
    tn = 1024
    return pl.pallas_call(
        _ada_kernel,
        grid=(depth, 6 * D // tn),
        in_specs=[pl.BlockSpec((8, D), lambda l, j: (0, 0)),
                  pl.BlockSpec((1, D, tn), lambda l, j: (l, 0, j)),
                  pl.BlockSpec((1, 1, tn), lambda l, j: (l, 0, j))],
        out_specs=pl.BlockSpec((1, 8, tn), lambda l, j: (l, 0, j)),
        out_shape=jax.ShapeDtypeStruct((depth, 8, 6 * D), F32),
        compiler_params=_cp(("arbitrary", "arbitrary")),
        name="ada",
    )(cvec, w_ada, b_ada.reshape(depth, 1, 6 * D))


def _stream_specs(srcs, tm):
    if len(srcs) == 1:
        return [pl.BlockSpec((tm, D), lambda i: (i, 0))]
    nlat = NL // tm
    return [pl.BlockSpec((tm, D), lambda i: (jnp.minimum(i, nlat - 1), 0)),
            pl.BlockSpec((tm, D), lambda i: (jnp.maximum(i - nlat, 0), 0))]


def _stream_rows(i, tm, refs):
    if len(refs) == 1:
        return refs[0][...]
    return jnp.where(i < NL // tm, refs[0][...], refs[1][...])


def _prenorm_kernel(nsrc, *refs):
    g_ref, sh_ref, sc_ref, o_ref = refs[nsrc:]
    is_lat = pl.program_id(0) < NL // TM
    conds = [is_lat, jnp.logical_not(is_lat)] if nsrc > 1 else [None]
    for x_ref, cond in zip(refs[:nsrc], conds):
        def write(x_ref=x_ref):
            o_ref[...] = (_rms(x_ref[...], g_ref[0]) * (1.0 + sc_ref[0]) + sh_ref[0]).astype(BF16)
        write() if cond is None else pl.when(cond)(write)


def _prenorm(srcs, g, modrb, l):
    return pl.pallas_call(
        functools.partial(_prenorm_kernel, len(srcs)),
        grid=(NBLK,),
        in_specs=[*_stream_specs(srcs, TM),
                  pl.BlockSpec((1, 1, D), lambda i: (l, 0, 0)),
                  pl.BlockSpec((1, 1, D), lambda i: (i, 0, 0)),
                  pl.BlockSpec((1, 1, D), lambda i: (i, 0, 1))],
        out_specs=pl.BlockSpec((TM, D), lambda i: (i, 0)),
        out_shape=jax.ShapeDtypeStruct((R, D), BF16),
        compiler_params=_cp(("arbitrary",)),
        name="prenorm",
    )(*srcs, g, modrb, modrb)


IP_TN = 1024
IP_STARTS = ((OFF['cq'][0], OFF['xm'][0]) + tuple(OFF['gates'][0] + k * IP_TN for k in range(3 * D // IP_TN))
             + (OFF['om'][0],))
IP_SHIFTS = tuple(s % IP_TN for s in IP_STARTS)
assert all(s < 128 and s % 16 == 0 for s in IP_SHIFTS) and OFF['ckv'][0] == Q_LORA and PM_W == IP_TN * len(IP_STARTS)


def _ip_window(j):
    idx = IP_STARTS[-1] // IP_TN
    for jj in range(len(IP_STARTS) - 2, -1, -1):
        idx = jnp.where(j == jj, IP_STARTS[jj] // IP_TN, idx)
    return idx


def _inproj_kernel(h_ref, wm_ref, we_ref, o_ref, wb_ref):
    j = pl.program_id(0)

    @pl.when(pl.program_id(1) == 0)
    def _():
        for shift in sorted(set(IP_SHIFTS)):
            hit = functools.reduce(jnp.logical_or, [j == jj for jj, s in enumerate(IP_SHIFTS) if s == shift])

            @pl.when(hit)
            def _(shift=shift):
                wb_ref[0:IP_TN - shift, :] = wm_ref[0, shift:IP_TN, :].astype(BF16)
                if shift:
                    wb_ref[IP_TN - shift:IP_TN, :] = we_ref[0, 0:shift, :].astype(BF16)

    o_ref[...] = _dot_nt(h_ref[...], wb_ref[...]).astype(o_ref.dtype)


def _inproj(h, w_in_t, l):
    return pl.pallas_call(
        _inproj_kernel,
        grid=(len(IP_STARTS), NBLK),
        in_specs=[pl.BlockSpec((TM, D), lambda j, i: (i, 0)),
                  pl.BlockSpec((1, IP_TN, D), lambda j, i: (l, _ip_window(j), 0)),
                  pl.BlockSpec((1, 128, D), lambda j, i: (l, (_ip_window(j) + 1) * (IP_TN // 128), 0))],
        out_specs=pl.BlockSpec((TM, IP_TN), lambda j, i: (i, j)),
        out_shape=jax.ShapeDtypeStruct((R, PM_W), BF16),
        scratch_shapes=[pltpu.VMEM((IP_TN, D), BF16)],
        compiler_params=_cp(("arbitrary", "arbitrary")),
        name="inproj",
    )(h, w_in_t, w_in_t)


US0, KR0, GM0 = OFF['us'][0], OFF['krope'][0], OFF['gm'][0]
assert KR0 % ROPE == 0 and GM0 % (4 * H_M) == 0 and US0 % IP_TN < 128 and US0 % 16 == 0 and OFF['us'][1] == IP_TN


def _inproj_side_kernel(h_ref, wu_ref, wue_ref, wkr_ref, wgm_ref, us_ref, o_ref, wb_ref):
    @pl.when(pl.program_id(0) == 0)
    def _():
        wb_ref[...] = jnp.zeros_like(wb_ref)
        shift = US0 % IP_TN
        wb_ref[0:D_S - shift, :] = wu_ref[0, shift:IP_TN, :].astype(BF16)
        wb_ref[D_S - shift:D_S, :] = wue_ref[0, 0:shift, :].astype(BF16)
        kr = wkr_ref[0].astype(BF16)
        wb_ref[D_S + SM_KR:D_S + SM_KR + ROPE, :] = kr
        half = ROPE // 4
        for blk in range(ROPE // half):
            src = (blk ^ 1) * half
            wb_ref[D_S + SM_KP + blk * half:D_S + SM_KP + (blk + 1) * half, :] = kr[src:src + half, :]
        wb_ref[D_S + SM_GM:D_S + SM_GM + 4 * H_M, :] = wgm_ref[0].astype(BF16)

    res = _dot_nt(h_ref[...], wb_ref[...])
    for cb in range(NLB):
        us_ref[cb] = res[:, cb * 128:(cb + 1) * 128]
    o_ref[...] = res[:, D_S:]


def _inproj_side(h, w_in_t, l):
    once = pl.Buffered(1)
    wblk = lambda rows, row: pl.BlockSpec((1, rows, D), lambda i: (l, row // rows, 0), pipeline_mode=once)
    return pl.pallas_call(
        _inproj_side_kernel,
        grid=(NBLK,),
        in_specs=[pl.BlockSpec((TM, D), lambda i: (i, 0)),
                  wblk(IP_TN, US0 - US0 % IP_TN), wblk(128, US0 - US0 % IP_TN + IP_TN),
                  wblk(ROPE, KR0), wblk(4 * H_M, GM0)],
        out_specs=[pl.BlockSpec((NLB, TM, 128), lambda i: (0, i, 0)),
                   pl.BlockSpec((TM, SM_W), lambda i: (i, 0))],
        out_shape=[jax.ShapeDtypeStruct((NLB, R, 128), F32),
                   jax.ShapeDtypeStruct((R, SM_W), F32)],
        scratch_shapes=[pltpu.VMEM((D_S + SM_W, D), BF16)],
        compiler_params=_cp(("arbitrary",)),
        name="inproj_side",
    )(h, w_in_t, w_in_t, w_in_t, w_in_t)


def _mla_proj_kernel(cq_ref, ckv_ref, kr_ref, kp_ref, cos_ref, sin_ref, gq_ref, gkv_ref,
                     wqn_ref, wqr_ref, wqp_ref, wkn_ref, wvt_ref, q_ref, k_ref, vt_ref):
    hq = _rms(cq_ref[...], gq_ref[0]).astype(BF16)
    hk = _rms(ckv_ref[...], gkv_ref[0]).astype(BF16)
    cos = cos_ref[...]
    sin = sin_ref[...]
    qscale = ATTN_SCALE * LOG2E
    qn = _dot(hq, wqn_ref[0]) * qscale
    qr = _dot(hq, wqr_ref[0])
    qp = _dot(hq, wqp_ref[0])
    kn = _dot(hk, wkn_ref[0])
    vt = _dot_nt(wvt_ref[0], hk)
    kr = (kr_ref[...] * cos + kp_ref[...] * sin).astype(BF16)
    ones = jnp.ones((DV, TM), BF16)
    for h in range(H_A):
        lo = slice(h * 256, h * 256 + 128)
        hi = slice(h * 256 + 128, (h + 1) * 256)
        sl = slice(h * 128, (h + 1) * 128)
        q_ref[:, lo] = qn[:, sl].astype(BF16)
        q_ref[:, hi] = ((qr[:, sl] * cos + qp[:, sl] * sin) * qscale).astype(BF16)
        k_ref[:, lo] = kn[:, sl].astype(BF16)
        k_ref[:, hi] = kr
        vt_ref[lo, :] = vt[sl, :].astype(BF16)
        vt_ref[hi, :] = ones


def _mla_proj(p, psm, cosk, sink, gq, gkv, wqn, wqr, wqp, wkn, wvt, l):
    lw = lambda shp: pl.BlockSpec((1,) + shp, lambda i: (l, 0, 0))
    return pl.pallas_call(
        _mla_proj_kernel,
        grid=(NBLK,),
        in_specs=[pl.BlockSpec((TM, 512), lambda i: (i, PM_CQ // 512)),
                  pl.BlockSpec((TM, 512), lambda i: (i, PM_CKV // 512)),
                  pl.BlockSpec((TM, 128), lambda i: (i, SM_KR // 128)),
                  pl.BlockSpec((TM, 128), lambda i: (i, SM_KP // 128)),
                  pl.BlockSpec((TM, 128), lambda i: (i, 0)),
                  pl.BlockSpec((TM, 128), lambda i: (i, 0)),
                  pl.BlockSpec((1, 1, 512), lambda i: (l, 0, 0)),
                  pl.BlockSpec((1, 1, 512), lambda i: (l, 0, 0)),
                  lw((512, 1024)), lw((512, 1024)), lw((512, 1024)), lw((512, 1024)), lw((1024, 512))],
        out_specs=[pl.BlockSpec((TM, 2048), lambda i: (i, 0)),
                   pl.BlockSpec((TM, 2048), lambda i: (i, 0)),
                   pl.BlockSpec((2048, TM), lambda i: (0, i))],
        out_shape=[jax.ShapeDtypeStruct((R, 2048), BF16),
                   jax.ShapeDtypeStruct((R, 2048), BF16),
                   jax.ShapeDtypeStruct((2048, R), BF16)],
        compiler_params=_cp(("arbitrary",)),
        name="mla_proj",
    )(p, p, psm, psm, cosk, sink, gq, gkv, wqn, wqr, wqp, wkn, wvt)


def _attn_kernel(q_ref, kl_ref, kc_ref, vl_ref, vc_ref, o_ref, m_ref, acc_ref, st_ref):
    qi = pl.program_id(2)
    heads = [slice(hh * 256, (hh + 1) * 256) for hh in range(AH)]
    qs = [q_ref[:, hs] for hs in heads]

    def group(hh, k_chunks, vt, m, acc):
        mx = None
        nk = k_chunks[0].shape[0]
        for c, k in enumerate(k_chunks):
            st = _dot_nt(k, qs[hh])
            st_ref[hh, c, 0:nk, :] = st
            mx = st if mx is None else jnp.maximum(mx, st)
        m_new = jnp.maximum(m, jnp.max(mx, axis=0, keepdims=True))
        p = jnp.concatenate([jnp.exp2(st_ref[hh, c, 0:nk, :] - m_new).astype(BF16)
                             for c in range(len(k_chunks))], axis=0)
        return m_new, jnp.exp2(m - m_new) * acc + _dot(vt, p)

    for hh, hs in enumerate(heads):
        m, acc = group(hh, [kc_ref[:, hs]], vc_ref[hs, :], jnp.full((1, TQ), -jnp.inf, F32),
                       jnp.zeros((2 * DV, TQ), F32))
        m_ref[hh] = m
        acc_ref[hh] = acc

    @pl.when(qi < T // TQ)
    def _():
        st = [(m_ref[hh], acc_ref[hh]) for hh in range(AH)]
        for j in range(T // (TK * AG)):
            chunks = [slice((j * AG + c) * TK, (j * AG + c + 1) * TK) for c in range(AG)]
            span = slice(j * AG * TK, (j + 1) * AG * TK)
            st = [group(hh, [kl_ref[ks, hs] for ks in chunks], vl_ref[hs, span], *st[hh])
                  for hh, hs in enumerate(heads)]
        for hh in range(AH):
            acc_ref[hh] = st[hh][1]

    for hh in range(AH):
        acc = acc_ref[hh]
        o_ref[:, hh * DV:(hh + 1) * DV] = jnp.transpose(acc[0:DV, :] / acc[DV:2 * DV, :]).astype(o_ref.dtype)


def _attention(q, k, vt, ctx_queries):
    nq = T // TQ
    qrow = lambda b, h, i: jnp.where(i < nq, b * nq + i, NL // TQ + b)
    return pl.pallas_call(
        _attn_kernel,
        grid=(NB_, H_A // AH, nq + 1 if ctx_queries else nq),
        in_specs=[pl.BlockSpec((TQ, AH * 256), lambda b, h, i: (qrow(b, h, i), h)),
                  pl.BlockSpec((T, AH * 256), lambda b, h, i: (b, h)),
                  pl.BlockSpec((TC, AH * 256), lambda b, h, i: (NL // TC + b, h)),
                  pl.BlockSpec((AH * 256, T), lambda b, h, i: (h, b)),
                  pl.BlockSpec((AH * 256, TC), lambda b, h, i: (h, NL // TC + b))],
        out_specs=pl.BlockSpec((TQ, AH * DV), lambda b, h, i: (qrow(b, h, i), h)),
        out_shape=jax.ShapeDtypeStruct((R if ctx_queries else NL, H_A * DV), BF16),
        scratch_shapes=[pltpu.VMEM((AH, 1, TQ), F32), pltpu.VMEM((AH, 2 * DV, TQ), F32),
                        pltpu.VMEM((AH, AG, TK, TQ), F32)],
        compiler_params=_cp(("arbitrary", "arbitrary", "arbitrary")),
        name="attention",
    )(q, k, k, vt, vt)


def _row_in(grow, rows):
    hit = grow == rows[0]
    for r in rows[1:]:
        hit = jnp.logical_or(hit, grow == r)
    return hit


def _mprep_kernel(x_ref, xp_ref, xn_ref, cw_ref, wq_ref, q_ref, k_ref, v_ref):
    i = pl.program_id(0)
    xb = x_ref[...]
    x = xb.astype(F32)
    prev_row = xp_ref[15:16, :].astype(F32)
    next_row = xn_ref[0:1, :].astype(F32)
    row = lax.broadcasted_iota(jnp.int32, (TM, 1), 0)
    grow = row + i * TM
    xprev = jnp.where(row == 0, prev_row, pltpu.roll(x, 1, axis=0))
    xprev = jnp.where(_row_in(grow, SEQ_STARTS), 0.0, xprev)
    xnext = jnp.where(row == TM - 1, next_row, pltpu.roll(x, TM - 1, axis=0))
    xnext = jnp.where(_row_in(grow, SEQ_ENDS), 0.0, xnext)
    cw = cw_ref[0]
    xc = xprev * cw[0:1, :] + x * cw[1:2, :] + xnext * cw[2:3, :]
    xcb = (xc * _sigmoid(xc)).astype(BF16)
    ones = jnp.ones((DH_M, TM), BF16)
    for h in range(H_M):
        sl = slice(h * DH_M, (h + 1) * DH_M)
        q_ref[:, sl] = _dot(xcb[:, sl], wq_ref[0, 0, h]).astype(BF16)
        k_ref[:, sl] = (_dot(xcb[:, sl], wq_ref[0, 1, h]) * DH_M ** -0.5).astype(BF16)
        v_ref[2 * h * DH_M:(2 * h + 1) * DH_M, :] = _dot_nt(wq_ref[0, 2, h], xb[:, sl]).astype(BF16)
        v_ref[(2 * h + 1) * DH_M:(2 * h + 2) * DH_M, :] = ones


def _mprep(p, conv_w, w_qkv, l):
    hb = TM // 16
    xcol = PM_XM // D_M
    return pl.pallas_call(
        _mprep_kernel,
        grid=(NBLK,),
        in_specs=[pl.BlockSpec((TM, D_M), lambda i: (i, xcol)),
                  pl.BlockSpec((16, D_M), lambda i: (jnp.maximum(i * hb - 1, 0), xcol)),
                  pl.BlockSpec((16, D_M), lambda i: (jnp.minimum((i + 1) * hb, R // 16 - 1), xcol)),
                  pl.BlockSpec((1, 3, D_M), lambda i: (l, 0, 0)),
                  pl.BlockSpec((1, 3, H_M, DH_M, DH_M), lambda i: (l, 0, 0, 0, 0))],
        out_specs=[pl.BlockSpec((TM, D_M), lambda i: (i, 0)),
                   pl.BlockSpec((TM, D_M), lambda i: (i, 0)),
                   pl.BlockSpec((2 * D_M, TM), lambda i: (0, i))],
        out_shape=[jax.ShapeDtypeStruct((R, D_M), BF16),
                   jax.ShapeDtypeStruct((R, D_M), BF16),
                   jax.ShapeDtypeStruct((2 * D_M, R), BF16)],
        compiler_params=_cp(("arbitrary",)),
        name="mlstm_prep",
    )(p, p, p, conv_w, w_qkv)


def _split3(x):
    h = x.astype(BF16)
    r = x - h.astype(F32)
    m = r.astype(BF16)
    l = (r - m.astype(F32)).astype(BF16)
    return h, m, l


def _mlstm_kernel(q_ref, k_ref, v_ref, g_ref, gt_ref, b_ref, bt_ref, h_ref, c_ref, m_ref):
    d = pl.program_id(1)
    s = pl.program_id(2)

    @pl.when(s == 0)
    def _():
        c_ref[...] = jnp.zeros_like(c_ref)
        m_ref[...] = jnp.zeros_like(m_ref)

    sign = 1 - 2 * d
    r = lax.broadcasted_iota(jnp.int32, (LC, LC), 0)
    c = lax.broadcasted_iota(jnp.int32, (LC, LC), 1)
    before = (c - r) * sign >= 0
    tri = jnp.where((r - c) * sign >= 0, 1.0, 0.0).astype(BF16)
    g = g_ref[0] + b_ref[0]
    gt = gt_ref[0] + bt_ref[0]
    ig_c = g[:, 0:H_M] * LOG2E
    lf_c = _log_sigmoid(g[:, H_M:2 * H_M]) * LOG2E
    lf_r = _log_sigmoid(gt[H_M:2 * H_M, :]) * LOG2E
    b_c = sum(_dot(tri, part) for part in _split3(lf_c))
    b_r = sum(_dot_nt(part, tri) for part in _split3(lf_r))
    tot = jnp.sum(lf_c, axis=0, keepdims=True)
    src = ig_c - b_c

    for h in range(H_M):
        sl = slice(h * DH_M, (h + 1) * DH_M)
        qh = q_ref[:, sl]
        kh = k_ref[:, sl]
        vth = v_ref[2 * h * DH_M:(2 * h + 2) * DH_M, :]
        bt = b_r[h:h + 1, :]
        tt = tot[:, h:h + 1]
        mp = m_ref[h:h + 1, 0:1]
        dm = jnp.where(before, bt + src[:, h:h + 1], -jnp.inf)
        m_inter = bt + mp
        m_t = jnp.maximum(jnp.max(dm, axis=0, keepdims=True), m_inter)
        sc = _dot_nt(kh, qh) * jnp.exp2(dm - m_t)
        inter = jnp.exp2(m_inter - m_t)
        st = c_ref[h]
        both = _dot(vth, sc.astype(BF16)) + inter * _dot_nt(st.astype(BF16), qh)
        den = both[DH_M:DH_M + 1, :]
        ht = both[0:DH_M, :] / jnp.maximum(jnp.abs(den), jnp.exp2(-m_t))
        h_ref[0, :, sl] = jnp.transpose(ht)
        gg = tt + src[:, h:h + 1]
        m_new = jnp.maximum(tt + mp, jnp.max(gg, axis=0, keepdims=True))
        kw = (kh.astype(F32) * jnp.exp2(gg - m_new)).astype(BF16)
        c_ref[h] = jnp.exp2(tt + mp - m_new) * st + _dot(vth, kw)
        m_ref[h:h + 1, :] = jnp.broadcast_to(m_new, (1, DH_M))


def _mlstm(q, k, v, gd, gtd, bd, btd):
    nl = T // LC
    nsteps = nl + TC // LC

    nc = TC // LC

    def rb(b, d, s):
        ctx = NL // LC + b * nc + jnp.where(d == 0, s, nc - 1 - s)
        lat = b * nl + jnp.where(d == 0, s - nc, nl - 1 - (s - nc))
        return jnp.where(s < nc, ctx, lat)

    qspec = pl.BlockSpec((LC, D_M), lambda b, d, s: (rb(b, d, s), 0))
    return pl.pallas_call(
        _mlstm_kernel,
        grid=(NB_, 2, nsteps),
        in_specs=[qspec, qspec,
                  pl.BlockSpec((2 * D_M, LC), lambda b, d, s: (0, rb(b, d, s))),
                  pl.BlockSpec((1, LC, 16), lambda b, d, s: (d, rb(b, d, s), 0)),
                  pl.BlockSpec((1, 16, LC), lambda b, d, s: (d, 0, rb(b, d, s))),
                  pl.BlockSpec((1, 1, 16), lambda b, d, s: (d, 0, 0)),
                  pl.BlockSpec((1, 16, 1), lambda b, d, s: (d, 0, 0))],
        out_specs=pl.BlockSpec((1, LC, D_M), lambda b, d, s: (d, rb(b, d, s), 0)),
        out_shape=jax.ShapeDtypeStruct((2, R, D_M), F32),
        scratch_shapes=[pltpu.VMEM((H_M, 2 * DH_M, DH_M), F32),
                        pltpu.VMEM((H_M, DH_M), F32)],
        compiler_params=_cp(("arbitrary", "arbitrary", "arbitrary")),
        name="mlstm_scan",
    )(q, k, v, gd, gtd, bd, btd)


S5_NLAT = T // SL
S5_NCTX = TC // SL
S5_SEG = S5_NLAT // NSEG


def _fold_copy(tok_ref, fold_ref, to_fold):
    def move(tok_idx, fold_rows, t):
        lanes = slice(t * 128, (t + 1) * 128)
        if to_fold:
            fold_ref[fold_rows, lanes] = tok_ref[tok_idx, :]
        else:
            tok_ref[tok_idx, :] = fold_ref[fold_rows, lanes]

    def body(j, carry):
        for b in range(NB_):
            rows = pl.ds(pl.multiple_of(b * S5_NLAT + j * NSEG, NSEG), NSEG)
            for t in range(SL):
                move(pl.ds(b * T + j * SL + t, NSEG, stride=S5_SEG * SL), rows, t)
        return carry

    lax.fori_loop(0, S5_SEG, body, 0)
    for t in range(SL):
        move(pl.ds(NL + t, NCX // SL, stride=SL), slice(NL // SL, NCH), t)


def _expand_block_diag(compact_ref, dense_ref, key_shift, key_mul, key_mask, row_gshift, col_gshift):
    comp = compact_ref[0]
    nrows, kc = comp.shape
    chunk = 512
    r = lax.broadcasted_iota(jnp.int32, (kc, chunk), 0)
    rg = (lax.broadcasted_iota(jnp.int32, (nrows, 1), 0) >> row_gshift) & (SLB - 1)
    for c0 in range(0, dense_ref.shape[1], chunk):
        q = lax.broadcasted_iota(jnp.int32, (kc, chunk), 1) + c0
        sel = jnp.where(r == (q >> key_shift) * key_mul + (q & key_mask), 1.0, 0.0).astype(BF16)
        cg = ((lax.broadcasted_iota(jnp.int32, (1, chunk), 1) + c0) >> col_gshift) & (SLB - 1)
        dense_ref[:, c0:c0 + chunk] = jnp.where(rg == cg, _dot(comp, sel), 0.0).astype(BF16)


def _cmul(ar, ai, sr, si):
    return ar * sr - ai * si, ar * si + ai * sr


def _s5_state_kernel(us_ref, wst_ref, a_ref, s_ref, e_ref, x_ref, w_ref):
    _fold_copy(us_ref.at[0], x_ref, True)
    _expand_block_diag(wst_ref, w_ref, 9, P_S, P_S - 1, 4, 6)
    e_ref[...] = _dot(x_ref[...].astype(BF16), w_ref[...])
    apow = a_ref[0]
    step_a = ((apow[0:1, :], apow[1:2, :]), (apow[2:3, :], apow[3:4, :]))
    nctx, seg = S5_NCTX, S5_SEG

    def advance(rows, d, st, write):
        cr = slice(2 * d * SW, (2 * d + 1) * SW)
        ci = slice((2 * d + 1) * SW, (2 * d + 2) * SW)
        dr = e_ref[rows, cr]
        di = e_ref[rows, ci]
        if write:
            e_ref[rows, cr] = st[0]
            e_ref[rows, ci] = st[1]
        nr, ni = _cmul(step_a[d][0], step_a[d][1], st[0], st[1])
        return nr + dr, ni + di

    def sweep(nsteps, rows_of, init, write):
        def body(kk, sts):
            out = []
            for b in range(NB_):
                out.append(advance(rows_of(b, kk), 0, sts[2 * b], write))
                out.append(advance(rows_of(b, nsteps - 1 - kk), 1, sts[2 * b + 1], write))
            return tuple(out)
        return lax.fori_loop(0, nsteps, body, init)

    z1 = jnp.zeros((1, SW), F32)
    carry = sweep(nctx, lambda b, kk: pl.ds(NL // SL + b * nctx + kk, 1), ((z1, z1),) * (2 * NB_), True)

    seg_rows = lambda b, kk: pl.ds(pl.multiple_of(b * S5_NLAT + kk * NSEG, NSEG), NSEG)
    z8 = jnp.zeros((NSEG, SW), F32)
    ends = sweep(seg, seg_rows, ((z8, z8),) * (2 * NB_), False)
    inits = []
    for ch in range(2 * NB_):
        d = ch % 2
        ar, ai = step_a[d]
        for _ in range(int(math.log2(seg))):
            ar, ai = ar * ar - ai * ai, 2.0 * ar * ai
        cur = carry[ch]
        rows_r, rows_i = [None] * NSEG, [None] * NSEG
        for kseg in (range(NSEG) if d == 0 else range(NSEG - 1, -1, -1)):
            rows_r[kseg], rows_i[kseg] = cur
            nr, ni = _cmul(ar, ai, cur[0], cur[1])
            cur = (nr + ends[ch][0][kseg:kseg + 1, :], ni + ends[ch][1][kseg:kseg + 1, :])
        inits.append((jnp.concatenate(rows_r, axis=0), jnp.concatenate(rows_i, axis=0)))
    sweep(seg, seg_rows, tuple(inits), True)
    s_ref[0] = e_ref[...].astype(BF16)


def _s5_state(us, wst, apow, l):
    return pl.pallas_call(
        _s5_state_kernel,
        grid=(NLB,),
        in_specs=[pl.BlockSpec((1, R, 128), lambda g: (g, 0, 0), pipeline_mode=pl.Buffered(1)),
                  pl.BlockSpec((1, SL * 128, 4 * P_S), lambda g: (l * NLB + g, 0, 0)),
                  pl.BlockSpec((1, 8, SW), lambda g: (l * NLB + g, 0, 0))],
        out_specs=pl.BlockSpec((1, NCH, 4 * SW), lambda g: (g, 0, 0)),
        out_shape=jax.ShapeDtypeStruct((NLB, NCH, 4 * SW), BF16),
        scratch_shapes=[pltpu.VMEM((NCH, 4 * SW), F32),
                        pltpu.VMEM((NCH, SL * 128), F32),
                        pltpu.VMEM((SL * 128, 4 * SW), BF16)],
        compiler_params=_cp(("arbitrary",)),
        name="s5_state",
    )(us, wst, apow)


def _s5_out_kernel(us_ref, toep_ref, s_ref, wo_ref, d_ref, y_ref, x_ref, tw_ref, ow_ref):
    _fold_copy(us_ref.at[0], x_ref, True)
    _expand_block_diag(toep_ref, tw_ref, 7, GROUP, GROUP - 1, 4, 4)
    _expand_block_diag(wo_ref, ow_ref, 7, GROUP, GROUP - 1, 6, 4)
    x = x_ref[...]
    y = _dot(x.astype(BF16), tw_ref[...]) + _dot(s_ref[0], ow_ref[...]) + d_ref[0] * x
    x_ref[...] = jax.nn.gelu(y)
    _fold_copy(y_ref.at[0], x_ref, False)


def _s5_out(us, toep, s, wout, dflat, l):
    return pl.pallas_call(
        _s5_out_kernel,
        grid=(NLB,),
        in_specs=[pl.BlockSpec((1, R, 128), lambda g: (g, 0, 0), pipeline_mode=pl.Buffered(1)),
                  pl.BlockSpec((1, SL * 128, SL * GROUP), lambda g: (l * NLB + g, 0, 0)),
                  pl.BlockSpec((1, NCH, 4 * SW), lambda g: (g, 0, 0), pipeline_mode=pl.Buffered(1)),
                  pl.BlockSpec((1, 4 * SW, SL * GROUP), lambda g: (l * NLB + g, 0, 0)),
                  pl.BlockSpec((1, 1, SL * 128), lambda g: (l * NLB + g, 0, 0))],
        out_specs=pl.BlockSpec((1, R, 128), lambda g: (g, 0, 0)),
        out_shape=jax.ShapeDtypeStruct((NLB, R, 128), F32),
        scratch_shapes=[pltpu.VMEM((NCH, SL * 128), F32),
                        pltpu.VMEM((SL * 128, SL * 128), BF16),
                        pltpu.VMEM((4 * SW, SL * 128), BF16)],
        compiler_params=_cp(("arbitrary",)),
        name="s5_out",
    )(us, toep, s, wout, dflat)


def _glu_kernel(y_ref, w_ref, o_ref):
    y = jnp.concatenate([y_ref[cb] for cb in range(NLB)], axis=1).astype(BF16)
    z = _dot(y, w_ref[0])
    o_ref[...] = (z[:, :D_S] * _sigmoid(z[:, D_S:])).astype(BF16)


def _glu(y, w, l):
    return pl.pallas_call(
        _glu_kernel,
        grid=(NBLK,),
        in_specs=[pl.BlockSpec((NLB, TM, 128), lambda i: (0, i, 0)),
                  pl.BlockSpec((1, D_S, 2 * D_S), lambda i: (l, 0, 0))],
        out_specs=pl.BlockSpec((TM, D_S), lambda i: (i, 0)),
        out_shape=jax.ShapeDtypeStruct((R, D_S), BF16),
        compiler_params=_cp(("arbitrary",)),
        name="s5_glu",
    )(y, w)


def _s5_params(a_re, a_im, log_dt, b_re, b_im, c_re, c_im, dskip):
    dt = jnp.exp(log_dt)[:, :, None]
    lam_r, lam_i = a_re * dt, a_im * dt
    mag = jnp.exp(lam_r)
    ar, ai = mag * jnp.cos(lam_i), mag * jnp.sin(lam_i)
    den = a_re * a_re + a_im * a_im
    nr, ni = ar - 1.0, ai
    cr = (nr * a_re + ni * a_im) / den
    ci = (ni * a_re - nr * a_im) / den
    cpr = c_re[None] * cr[:, :, None, :] - c_im[None] * ci[:, :, None, :]
    cpi = c_re[None] * ci[:, :, None, :] + c_im[None] * cr[:, :, None, :]
    j = jnp.arange(SL + 1, dtype=F32)[:, None, None, None]
    pm = jnp.exp(lam_r[None] * j)
    pr, pi = pm * jnp.cos(lam_i[None] * j), pm * jnp.sin(lam_i[None] * j)
    bt_re, bt_im = b_re.transpose(0, 2, 1), b_im.transpose(0, 2, 1)
    abr = pr[:, :, :, None, :] * bt_re[None, None] - pi[:, :, :, None, :] * bt_im[None, None]
    abi = pr[:, :, :, None, :] * bt_im[None, None] + pi[:, :, :, None, :] * bt_re[None, None]
    rev = lambda a, lo: jnp.stack([a[lo + SL - 1 - t] for t in range(SL)], axis=0)
    lagk = lambda d, ar_, ai_: (jnp.einsum('gcp,jgep->gejc', cpr[d], ar_) - jnp.einsum('gcp,jgep->gejc', cpi[d], ai_)
                                ).reshape(G_S, GROUP, SL * GROUP)
    catf = lagk(0, abr[:SL, 0], abi[:SL, 0])
    catb = lagk(1, rev(abr[:, 1], 0), rev(abi[:, 1], 0))
    rows = []
    for t_in in range(SL):
        lo, hi = t_in * GROUP, (SL - 1 - t_in) * GROUP
        rows.append(jnp.pad(catf[..., :SL * GROUP - lo], ((0, 0), (0, 0), (lo, 0)))
                    + jnp.pad(catb[..., hi:], ((0, 0), (0, 0), (0, hi))))
    by_block = lambda a, lead, minor: a.reshape(lead + (NLB, SLB) + minor)
    t5 = by_block(jnp.stack(rows, axis=0), (SL,), (GROUP, SL * GROUP))
    toep = t5.transpose(1, 0, 2, 3, 4).reshape(NLB, SL * 128, SL * GROUP)
    ws = jnp.stack([rev(abr[:, 0], 0), rev(abi[:, 0], 0), abr[:SL, 1], abi[:SL, 1]], axis=3)
    ws = by_block(ws, (SL,), (GROUP, 4 * P_S))
    wst = ws.transpose(1, 0, 2, 3, 4).reshape(NLB, SL * 128, 4 * P_S)
    def readout(pw_r, pw_i, d):
        cr_t, ci_t = cpr[d].transpose(0, 2, 1)[:, :, None, :], cpi[d].transpose(0, 2, 1)[:, :, None, :]
        wr, wi = pw_r.transpose(1, 2, 0)[..., None], pw_i.transpose(1, 2, 0)[..., None]
        return cr_t * wr - ci_t * wi, -(cr_t * wi + ci_t * wr)
    of_re, of_im = readout(pr[1:SL + 1, 0], pi[1:SL + 1, 0], 0)
    ob_re, ob_im = readout(rev(pr[:, 1], 1), rev(pi[:, 1], 1), 1)
    wo = by_block(jnp.stack([of_re, of_im, ob_re, ob_im], axis=0), (4,), (P_S, SL * GROUP))
    wout = wo.transpose(1, 0, 2, 3, 4).reshape(NLB, 4 * SW, SL * GROUP)
    blk = lambda a: a.reshape(NLB, 1, SW)
    apow = jnp.concatenate([blk(pr[SL, 0]), blk(pi[SL, 0]), blk(pr[SL, 1]), blk(pi[SL, 1]),
                            jnp.zeros((NLB, 4, SW), F32)], axis=1)
    dflat = jnp.tile(dskip.reshape(NLB, 1, 128), (1, 1, SL))
    return toep.astype(BF16), wst.astype(BF16), wout.astype(BF16), apow, dflat


def _mix_out_kernel(nsrc, tm, *refs):
    (a_ref, hf_ref, hb_ref, om_ref, gh_ref, s_ref, ga_ref, gm_ref, gs_ref, wa_ref, wm_ref, ws_ref, wo_ref,
     gt_ref, gf_ref, sh_ref, sc_ref, x1_ref, h2_ref) = refs[nsrc:]
    hsum = hf_ref[0] + hb_ref[0]
    parts = []
    for h in range(H_M):
        xh = hsum[:, h * DH_M:(h + 1) * DH_M]
        parts.append(xh * lax.rsqrt(jnp.mean(xh * xh, axis=-1, keepdims=True) + EPS))
    hn = jnp.concatenate(parts, axis=1) * gh_ref[0]
    m = (_sigmoid(om_ref[...].astype(F32)) * hn).astype(BF16)
    t = (_sigmoid(ga_ref[...].astype(F32)) * _dot(a_ref[...], wa_ref[0])
         + _sigmoid(gm_ref[...].astype(F32)) * _dot(m, wm_ref[0])
         + _sigmoid(gs_ref[...].astype(F32)) * _dot(s_ref[...], ws_ref[0]))
    x1 = _stream_rows(pl.program_id(0), tm, refs[:nsrc]) + gt_ref[0] * _dot(t.astype(BF16), wo_ref[0])
    x1_ref[...] = x1
    h2_ref[...] = (_rms(x1, gf_ref[0]) * (1.0 + sc_ref[0]) + sh_ref[0]).astype(BF16)


def _mix_out(a, hdir, p, gh, s, wa, wm, ws, wo, srcs, modrb, g_ffn, l, nrows):
    tm = 256
    per = TM // tm
    gcol = PM_GATES // D
    row = lambda w, c: pl.BlockSpec((tm, w), lambda i: (i, c))
    once = pl.Buffered(1)
    wspec = pl.BlockSpec((1, 1024, D), lambda i: (l, 0, 0), pipeline_mode=once)
    mod = lambda c: pl.BlockSpec((1, 1, D), lambda i: (i // per, 0, c))
    return pl.pallas_call(
        functools.partial(_mix_out_kernel, len(srcs), tm),
        grid=(nrows // tm,),
        in_specs=[*_stream_specs(srcs, tm),
                  row(1024, 0),
                  pl.BlockSpec((1, tm, D_M), lambda i: (0, i, 0)),
                  pl.BlockSpec((1, tm, D_M), lambda i: (1, i, 0)),
                  row(1024, PM_OM // 1024),
                  pl.BlockSpec((1, 1, D_M), lambda i: (l, 0, 0)),
                  row(1024, 0),
                  row(D, gcol), row(D, gcol + 1), row(D, gcol + 2),
                  wspec, wspec, wspec,
                  pl.BlockSpec((1, D, D), lambda i: (l, 0, 0), pipeline_mode=once),
                  mod(2),
                  pl.BlockSpec((1, 1, D), lambda i: (l, 0, 0)), mod(3), mod(4)],
        out_specs=[row(D, 0), row(D, 0)],
        out_shape=[jax.ShapeDtypeStruct((nrows, D), F32), jax.ShapeDtypeStruct((nrows, D), BF16)],
        compiler_params=_cp(("arbitrary",)),
        name="mix_out",
    )(*srcs, a, hdir, hdir, p, gh, s, p, p, p, wa, wm, ws, wo, modrb, g_ffn, modrb, modrb)


def _resid_kernel(t_ref, w_ref, x_ref, gt_ref, o_ref, wb_ref):
    @pl.when(pl.program_id(1) == 0)
    def _():
        wb_ref[...] = w_ref[0].astype(BF16)

    o_ref[...] = x_ref[...] + gt_ref[0] * _dot(t_ref[...], wb_ref[...])


def _resid(t, w, x, modrb, gate_chunk, tn, l, nrows, name):
    kdim = t.shape[1]
    nj = D // tn
    return pl.pallas_call(
        _resid_kernel,
        grid=(nj, nrows // TM),
        in_specs=[pl.BlockSpec((TM, kdim), lambda j, i: (i, 0)),
                  pl.BlockSpec((1, kdim, tn), lambda j, i: (l, 0, j)),
                  pl.BlockSpec((TM, tn), lambda j, i: (i, j)),
                  pl.BlockSpec((1, 1, tn), lambda j, i: (i, 0, gate_chunk * nj + j))],
        out_specs=pl.BlockSpec((TM, tn), lambda j, i: (i, j)),
        out_shape=jax.ShapeDtypeStruct((nrows, D), F32),
        scratch_shapes=[pltpu.VMEM((kdim, tn), BF16)],
        compiler_params=_cp(("arbitrary", "arbitrary")),
        name=name,
    )(t, w, x, modrb)


def _ffn_in_kernel(h_ref, wa_ref, wb_ref, o_ref, was_ref, wbs_ref):
    @pl.when(pl.program_id(1) == 0)
    def _():
        was_ref[...] = wa_ref[0].astype(BF16)
        wbs_ref[...] = wb_ref[0].astype(BF16)

    h = h_ref[...]
    a = _dot(h, was_ref[...])
    b = _dot(h, wbs_ref[...])
    o_ref[...] = (a * _sigmoid(a) * b).astype(BF16)


def _ffn_in(h, w, l, nrows):
    tn = 512
    nj = D_FF // tn
    return pl.pallas_call(
        _ffn_in_kernel,
        grid=(nj, nrows // TM),
        in_specs=[pl.BlockSpec((TM, D), lambda j, i: (i, 0)),
                  pl.BlockSpec((1, D, tn), lambda j, i: (l, 0, j)),
                  pl.BlockSpec((1, D, tn), lambda j, i: (l, 0, nj + j))],
        out_specs=pl.BlockSpec((TM, tn), lambda j, i: (i, j)),
        out_shape=jax.ShapeDtypeStruct((nrows, D_FF), BF16),
        scratch_shapes=[pltpu.VMEM((D, tn), BF16), pltpu.VMEM((D, tn), BF16)],
        compiler_params=_cp(("arbitrary", "arbitrary")),
        name="ffn_in",
    )(h, w, w)


def _final_norm_kernel(x_ref, g_ref, o_ref):
    o_ref[...] = _rms(x_ref[...], g_ref[...])


def _final_norm(x, g):
    return pl.pallas_call(
        _final_norm_kernel,
        grid=(NL // TM,),
        in_specs=[pl.BlockSpec((TM, D), lambda i: (i, 0)),
                  pl.BlockSpec((1, D), lambda i: (0, 0))],
        out_specs=pl.BlockSpec((TM, D), lambda i: (i, 0)),
        out_shape=jax.ShapeDtypeStruct((NL, D), F32),
        compiler_params=_cp(("arbitrary",)),
        name="final_norm",
    )(x, g.reshape(1, D))


def _rope_tables():
    f32 = np.float32
    rows = T // GRID_W
    rr, cc = np.meshgrid(np.arange(rows, dtype=f32), np.arange(GRID_W, dtype=f32), indexing='ij')
    rr, cc = rr.reshape(-1), cc.reshape(-1)
    half = ROPE // 2
    inv = (f32(1.0) / (f32(ROPE_THETA) ** (np.arange(0, half, 2, dtype=f32) / f32(half)))).astype(f32)
    ang = np.stack([rr[:, None] * inv, cc[:, None] * inv], axis=1).astype(f32)
    cos = np.cos(ang).astype(f32)
    sin = np.sin(ang).astype(f32)
    cos_f = np.stack([cos, cos], axis=2).reshape(T, ROPE)
    sin_f = np.stack([-sin, sin], axis=2).reshape(T, ROPE)
    pad = lambda a: np.concatenate([a, np.zeros((a.shape[0], 128 - ROPE), f32)], axis=1)
    cos_l, sin_l = pad(cos_f), pad(sin_f)
    cos_c = pad(np.ones((NCX, ROPE), f32))
    sin_c = np.zeros((NCX, 128), f32)
    return (jnp.asarray(np.concatenate([cos_l] * NB_ + [cos_c], axis=0)),
            jnp.asarray(np.concatenate([sin_l] * NB_ + [sin_c], axis=0)))


def _rope_partner(w):
    s = w.shape[:-1]
    w4 = w.reshape(s + (2, 2, ROPE // 4))
    return jnp.concatenate([w4[..., 1:2, :], w4[..., 0:1, :]], axis=-2).reshape(s + (ROPE,))


def _prep_weights(w_uq, w_ukv):
    depth = w_uq.shape[0]
    uq = w_uq.reshape(depth, Q_LORA, H_A, NOPE + ROPE)
    wqn = uq[..., :NOPE].reshape(depth, Q_LORA, H_A * NOPE)
    qr = uq[..., NOPE:]
    zq = jnp.zeros_like(qr)
    wqr = jnp.concatenate([qr, zq], axis=-1).reshape(depth, Q_LORA, H_A * 128)
    wqp = jnp.concatenate([_rope_partner(qr), zq], axis=-1).reshape(depth, Q_LORA, H_A * 128)
    ukv = w_ukv.reshape(depth, KV_LORA, H_A, NOPE + DV)
    wkn = ukv[..., :NOPE].reshape(depth, KV_LORA, H_A * NOPE)
    wvt = ukv[..., NOPE:].reshape(depth, KV_LORA, H_A * DV).transpose(0, 2, 1)
    return tuple(a.astype(BF16) for a in (wqn, wqr, wqp, wkn, wvt))


def kernel(x, c, ctx, c_ctx, w_ada, b_ada, g_mix, g_ffn, w_in, g_cq, w_uq, g_ckv, w_ukv, conv_m, w_qkv_m, b_gate_m, g_h_m, s5_a_re, s5_a_im, s5_log_dt, s5_b_re, s5_b_im, s5_c_re, s5_c_im, s5_d, w_glu, w_br_a, w_br_m, w_br_s, w_out, w_ffn_in, w_ffn_out, g_final):
    depth = w_ada.shape[0]
    xs = (x.reshape(NL, D), ctx.reshape(NCX, D))
    cvec = jnp.concatenate([c, c_ctx[None], jnp.zeros((8 - NB_ - 1, D), F32)], axis=0)
    mod = _ada(cvec, w_ada, b_ada)
    blocks_per_batch = T // TM
    cosk, sink = _rope_tables()
    mla_w = _prep_weights(w_uq, w_ukv)
    w_in_t = jnp.swapaxes(w_in, 1, 2)
    toep, wst, wout, apow, dflat = (a.reshape((depth * NLB,) + a.shape[2:]) for a in jax.vmap(_s5_params)(
        s5_a_re, s5_a_im, s5_log_dt, s5_b_re, s5_b_im, s5_c_re, s5_c_im, s5_d))
    w_out_b = w_out.astype(BF16)
    w_qkv_b = jnp.concatenate([w_qkv_m[:, :2], jnp.swapaxes(w_qkv_m[:, 2:], -1, -2)], axis=1).astype(BF16)
    w_glu_b = w_glu.astype(BF16)
    w_br_a_b, w_br_m_b, w_br_s_b = w_br_a.astype(BF16), w_br_m.astype(BF16), w_br_s.astype(BF16)
    gain = lambda g: g.reshape(depth, 1, g.shape[-1])
    g_mix, g_ffn, g_cq, g_ckv, g_h_m = gain(g_mix), gain(g_ffn), gain(g_cq), gain(g_ckv), gain(g_h_m)

    for l in range(depth):
        modrb = jnp.concatenate(
            [jnp.broadcast_to(mod[l, b:b + 1], (blocks_per_batch, 6 * D)) for b in range(NB_)]
            + [jnp.broadcast_to(mod[l, NB_:NB_ + 1], (NCX // TM, 6 * D))], axis=0).reshape(NBLK, 1, 6 * D)
        h1 = _prenorm(xs, g_mix, modrb, l)
        p = _inproj(h1, w_in_t, l)
        us, psm = _inproj_side(h1, w_in_t, l)

        last = l == depth - 1
        nrows = NL if last else R
        q, k, vt = _mla_proj(p, psm, cosk, sink, g_cq, g_ckv, *mla_w, l)
        a = _attention(q, k, vt, not last)

        qm, km, vm = _mprep(p, conv_m, w_qkv_b, l)
        gm = psm[:, SM_GM:SM_GM + 4 * H_M]
        gd = jnp.stack([gm[:, :2 * H_M], gm[:, 2 * H_M:]], axis=0)
        bd = b_gate_m[l].reshape(2, 1, 2 * H_M)
        hdir = _mlstm(qm, km, vm, gd, gd.transpose(0, 2, 1), bd, bd.transpose(0, 2, 1))

        st = _s5_state(us, wst, apow, l)
        y = _s5_out(us, toep, st, wout, dflat, l)
        s = _glu(y, w_glu_b, l)

        x1, h2 = _mix_out(a, hdir, p, g_h_m, s, w_br_a_b, w_br_m_b, w_br_s_b, w_out_b, xs, modrb, g_ffn, l, nrows)
        u = _ffn_in(h2, w_ffn_in, l, nrows)
        xs = (_resid(u, w_ffn_out, x1, modrb, 5, 512, l, nrows, "resid_ffn"),)

    return _final_norm(xs[0], g_final).reshape(NB_, T, D)
```

```python
import functools
import math

import numpy as np
import jax
import jax.numpy as jnp
from jax import lax
from jax.experimental import pallas as pl
from jax.experimental.pallas import tpu as pltpu

F32 = jnp.float32
BF16 = jnp.bfloat16

D = 2048
NB_ = 2
T = 4096
TC = 256
GRID_W = 64
EPS = 1e-6
H_A, Q_LORA, KV_LORA, NOPE, ROPE, DV = 8, 512, 512, 128, 64, 128
ROPE_THETA = 10000.0
ATTN_SCALE = (NOPE + ROPE) ** -0.5
H_M, DH_M = 8, 128
D_M = H_M * DH_M
D_S, GROUP, P_S = 1024, 16, 64
G_S = D_S // GROUP
D_FF = ((8 * D // 3 + 255) // 256) * 256
OFF = {}
_o = 0
for _n, _w in (('cq', Q_LORA), ('ckv', KV_LORA), ('krope', ROPE), ('xm', D_M), ('om', D_M),
               ('gm', 4 * H_M), ('us', D_S), ('gates', 3 * D)):
    OFF[_n] = (_o, _w)
    _o += _w

NL = NB_ * T
NCX = NB_ * TC
R = NL + NCX
TM = 512
NBLK = R // TM
LC = 256
TQ = 256
TK = 512
AH = 4
AG = 2
SL = 8
NCH = R // SL
NSEG = 8
SLB = 128 // GROUP
NLB = D_S // 128
SW = SLB * P_S
LOG2E = math.log2(math.e)
PM_CQ, PM_CKV, PM_XM, PM_GATES, PM_OM, PM_W = 0, 512, 1024, 2048, 2048 + 3 * D, 3072 + 3 * D
SM_KR, SM_KP, SM_GM, SM_W = 0, 128, 256, 384
VMEM_LIMIT = 56 * 1024 * 1024
SEQ_STARTS = tuple(b * T for b in range(NB_)) + tuple(NL + b * TC for b in range(NB_))
SEQ_ENDS = tuple(b * T + T - 1 for b in range(NB_)) + tuple(NL + b * TC + TC - 1 for b in range(NB_))


def _cp(sem):
    return pltpu.CompilerParams(dimension_semantics=sem, vmem_limit_bytes=VMEM_LIMIT)


def _dot(a, b):
    return jnp.dot(a, b, preferred_element_type=F32)


def _dot_nt(a, b):
    return lax.dot_general(a, b, (((1,), (1,)), ((), ())), preferred_element_type=F32)


def _sigmoid(x):
    return 1.0 / (1.0 + jnp.exp(-x))


def _log_sigmoid(x):
    return jnp.minimum(x, 0.0) - jnp.log(1.0 + jnp.exp(-jnp.abs(x)))


def _rms(x, g):
    xf = x.astype(F32)
    return xf * lax.rsqrt(jnp.mean(xf * xf, axis=-1, keepdims=True) + EPS) * g


def _ada_kernel(c_ref, w_ref, b_ref, o_ref):
    c = c_ref[...]
    s = (c * _sigmoid(c)).astype(BF16)
    o_ref[0] = _dot(s, w_ref[0].astype(BF16)) + b_ref[0]


def _ada(cvec, w_ada, b_ada):
    depth = w_ada.shape[0]
    tn = 1024
    return pl.pallas_call(
        _ada_kernel,
        grid=(depth, 6 * D // tn),
        in_specs=[pl.BlockSpec((8, D), lambda l, j: (0, 0)),
                  pl.BlockSpec((1, D, tn), lambda l, j: (l, 0, j)),
                  pl.BlockSpec((1, 1, tn), lambda l, j: (l, 0, j))],
        out_specs=pl.BlockSpec((1, 8, tn), lambda l, j: (l, 0, j)),
        out_shape=jax.ShapeDtypeStruct((depth, 8, 6 * D), F32),
        compiler_params=_cp(("arbitrary", "arbitrary")),
        name="ada",
    )(cvec, w_ada, b_ada.reshape(depth, 1, 6 * D))


def _stream_specs(srcs, tm):
    if len(srcs) == 1:
        return [pl.BlockSpec((tm, D), lambda i: (i, 0))]
    nlat = NL // tm
    return [pl.BlockSpec((tm, D), lambda i: (jnp.minimum(i, nlat - 1), 0)),
            pl.BlockSpec((tm, D), lambda i: (jnp.maximum(i - nlat, 0), 0))]


def _stream_rows(i, tm, refs):
    if len(refs) == 1:
        return refs[0][...]
    return jnp.where(i < NL // tm, refs[0][...], refs[1][...])


def _prenorm_kernel(nsrc, *refs):
    g_ref, sh_ref, sc_ref, o_ref = refs[nsrc:]
    is_lat = pl.program_id(0) < NL // TM
    conds = [is_lat, jnp.logical_not(is_lat)] if nsrc > 1 else [None]
    for x_ref, cond in zip(refs[:nsrc], conds):
        def write(x_ref=x_ref):
            o_ref[...] = (_rms(x_ref[...], g_ref[0]) * (1.0 + sc_ref[0]) + sh_ref[0]).astype(BF16)
        write() if cond is None else pl.when(cond)(write)


def _prenorm(srcs, g, modrb, l):
    return pl.pallas_call(
        functools.partial(_prenorm_kernel, len(srcs)),
        grid=(NBLK,),
        in_specs=[*_stream_specs(srcs, TM),
                  pl.BlockSpec((1, 1, D), lambda i: (l, 0, 0)),
                  pl.BlockSpec((1, 1, D), lambda i: (i, 0, 0)),
                  pl.BlockSpec((1, 1, D), lambda i: (i, 0, 1))],
        out_specs=pl.BlockSpec((TM, D), lambda i: (i, 0)),
        out_shape=jax.ShapeDtypeStruct((R, D), BF16),
        compiler_params=_cp(("arbitrary",)),
        name="prenorm",
    )(*srcs, g, modrb, modrb)


IP_TN = 1024
IP_STARTS = ((OFF['cq'][0], OFF['xm'][0]) + tuple(OFF['gates'][0] + k * IP_TN for k in range(3 * D // IP_TN))
             + (OFF['om'][0],))
IP_SHIFTS = tuple(s % IP_TN for s in IP_STARTS)
assert all(s < 128 and s % 16 == 0 for s in IP_SHIFTS) and OFF['ckv'][0] == Q_LORA and PM_W == IP_TN * len(IP_STARTS)


def _ip_window(j):
    idx = IP_STARTS[-1] // IP_TN
    for jj in range(len(IP_STARTS) - 2, -1, -1):
        idx = jnp.where(j == jj, IP_STARTS[jj] // IP_TN, idx)
    return idx


def _inproj_kernel(h_ref, wm_ref, we_ref, o_ref, wb_ref):
    j = pl.program_id(0)

    @pl.when(pl.program_id(1) == 0)
    def _():
        for shift in sorted(set(IP_SHIFTS)):
            hit = functools.reduce(jnp.logical_or, [j == jj for jj, s in enumerate(IP_SHIFTS) if s == shift])

            @pl.when(hit)
            def _(shift=shift):
                wb_ref[0:IP_TN - shift, :] = wm_ref[0, shift:IP_TN, :].astype(BF16)
                if shift:
                    wb_ref[IP_TN - shift:IP_TN, :] = we_ref[0, 0:shift, :].astype(BF16)

    o_ref[...] = _dot_nt(h_ref[...], wb_ref[...]).astype(o_ref.dtype)


def _inproj(h, w_in_t, l):
    return pl.pallas_call(
        _inproj_kernel,
        grid=(len(IP_STARTS), NBLK),
        in_specs=[pl.BlockSpec((TM, D), lambda j, i: (i, 0)),
                  pl.BlockSpec((1, IP_TN, D), lambda j, i: (l, _ip_window(j), 0)),
                  pl.BlockSpec((1, 128, D), lambda j, i: (l, (_ip_window(j) + 1) * (IP_TN // 128), 0))],
        out_specs=pl.BlockSpec((TM, IP_TN), lambda j, i: (i, j)),
        out_shape=jax.ShapeDtypeStruct((R, PM_W), BF16),
        scratch_shapes=[pltpu.VMEM((IP_TN, D), BF16)],
        compiler_params=_cp(("arbitrary", "arbitrary")),
        name="inproj",
    )(h, w_in_t, w_in_t)


US0, KR0, GM0 = OFF['us'][0], OFF['krope'][0], OFF['gm'][0]
assert KR0 % ROPE == 0 and GM0 % (4 * H_M) == 0 and US0 % IP_TN < 128 and US0 % 16 == 0 and OFF['us'][1] == IP_TN


def _inproj_side_kernel(h_ref, wu_ref, wue_ref, wkr_ref, wgm_ref, us_ref, o_ref, wb_ref):
    @pl.when(pl.program_id(0) == 0)
    def _():
        wb_ref[...] = jnp.zeros_like(wb_ref)
        shift = US0 % IP_TN
        wb_ref[0:D_S - shift, :] = wu_ref[0, shift:IP_TN, :].astype(BF16)
        wb_ref[D_S - shift:D_S, :] = wue_ref[0, 0:shift, :].astype(BF16)
        kr = wkr_ref[0].astype(BF16)
        wb_ref[D_S + SM_KR:D_S + SM_KR + ROPE, :] = kr
        half = ROPE // 4
        for blk in range(ROPE // half):
            src = (blk ^ 1) * half
            wb_ref[D_S + SM_KP + blk * half:D_S + SM_KP + (blk + 1) * half, :] = kr[src:src + half, :]
        wb_ref[D_S + SM_GM:D_S + SM_GM + 4 * H_M, :] = wgm_ref[0].astype(BF16)

    res = _dot_nt(h_ref[...], wb_ref[...])
    for cb in range(NLB):
        us_ref[cb] = res[:, cb * 128:(cb + 1) * 128]
    o_ref[...] = res[:, D_S:]


def _inproj_side(h, w_in_t, l):
    once = pl.Buffered(1)
    wblk = lambda rows, row: pl.BlockSpec((1, rows, D), lambda i: (l, row // rows, 0), pipeline_mode=once)
    return pl.pallas_call(
        _inproj_side_kernel,
        grid=(NBLK,),
        in_specs=[pl.BlockSpec((TM, D), lambda i: (i, 0)),
                  wblk(IP_TN, US0 - US0 % IP_TN), wblk(128, US0 - US0 % IP_TN + IP_TN),
                  wblk(ROPE, KR0), wblk(4 * H_M, GM0)],
        out_specs=[pl.BlockSpec((NLB, TM, 128), lambda i: (0, i, 0)),
                   pl.BlockSpec((TM, SM_W), lambda i: (i, 0))],
        out_shape=[jax.ShapeDtypeStruct((NLB, R, 128), F32),
                   jax.ShapeDtypeStruct((R, SM_W), F32)],
        scratch_shapes=[pltpu.VMEM((D_S + SM_W, D), BF16)],
        compiler_params=_cp(("arbitrary",)),
        name="inproj_side",
    )(h, w_in_t, w_in_t, w_in_t, w_in_t)


def _mla_proj_kernel(cq_ref, ckv_ref, kr_ref, kp_ref, cos_ref, sin_ref, gq_ref, gkv_ref,
                     wqn_ref, wqr_ref, wqp_ref, wkn_ref, wvt_ref, q_ref, k_ref, vt_ref):
    hq = _rms(cq_ref[...], gq_ref[0]).astype(BF16)
    hk = _rms(ckv_ref[...], gkv_ref[0]).astype(BF16)
    cos = cos_ref[...]
    sin = sin_ref[...]
    qscale = ATTN_SCALE * LOG2E
    qn = _dot(hq, wqn_ref[0]) * qscale
    qr = _dot(hq, wqr_ref[0])
    qp = _dot(hq, wqp_ref[0])
    kn = _dot(hk, wkn_ref[0])
    vt = _dot_nt(wvt_ref[0], hk)
    kr = (kr_ref[...] * cos + kp_ref[...] * sin).astype(BF16)
    ones = jnp.ones((DV, TM), BF16)
    for h in range(H_A):
        lo = slice(h * 256, h * 256 + 128)
        hi = slice(h * 256 + 128, (h + 1) * 256)
        sl = slice(h * 128, (h + 1) * 128)
        q_ref[:, lo] = qn[:, sl].astype(BF16)
        q_ref[:, hi] = ((qr[:, sl] * cos + qp[:, sl] * sin) * qscale).astype(BF16)
        k_ref[:, lo] = kn[:, sl].astype(BF16)
        k_ref[:, hi] = kr
        vt_ref[lo, :] = vt[sl, :].astype(BF16)
        vt_ref[hi, :] = ones


def _mla_proj(p, psm, cosk, sink, gq, gkv, wqn, wqr, wqp, wkn, wvt, l):
    lw = lambda shp: pl.BlockSpec((1,) + shp, lambda i: (l, 0, 0))
    return pl.pallas_call(
        _mla_proj_kernel,
        grid=(NBLK,),
        in_specs=[pl.BlockSpec((TM, 512), lambda i: (i, PM_CQ // 512)),
                  pl.BlockSpec((TM, 512), lambda i: (i, PM_CKV // 512)),
                  pl.BlockSpec((TM, 128), lambda i: (i, SM_KR // 128)),
                  pl.BlockSpec((TM, 128), lambda i: (i, SM_KP // 128)),
                  pl.BlockSpec((TM, 128), lambda i: (i, 0)),
                  pl.BlockSpec((TM, 128), lambda i: (i, 0)),
                  pl.BlockSpec((1, 1, 512), lambda i: (l, 0, 0)),
                  pl.BlockSpec((1, 1, 512), lambda i: (l, 0, 0)),
                  lw((512, 1024)), lw((512, 1024)), lw((512, 1024)), lw((512, 1024)), lw((1024, 512))],
        out_specs=[pl.BlockSpec((TM, 2048), lambda i: (i, 0)),
                   pl.BlockSpec((TM, 2048), lambda i: (i, 0)),
                   pl.BlockSpec((2048, TM), lambda i: (0, i))],
        out_shape=[jax.ShapeDtypeStruct((R, 2048), BF16),
                   jax.ShapeDtypeStruct((R, 2048), BF16),
                   jax.ShapeDtypeStruct((2048, R), BF16)],
        compiler_params=_cp(("arbitrary",)),
        name="mla_proj",
    )(p, p, psm, psm, cosk, sink, gq, gkv, wqn, wqr, wqp, wkn, wvt)


def _attn_kernel(q_ref, kl_ref, kc_ref, vl_ref, vc_ref, o_ref, m_ref, acc_ref, st_ref):
    qi = pl.program_id(2)
    heads = [slice(hh * 256, (hh + 1) * 256) for hh in range(AH)]
    qs = [q_ref[:, hs] for hs in heads]

    def group(hh, k_chunks, vt, m, acc):
        mx = None
        nk = k_chunks[0].shape[0]
        for c, k in enumerate(k_chunks):
            st = _dot_nt(k, qs[hh])
            st_ref[hh, c, 0:nk, :] = st
            mx = st if mx is None else jnp.maximum(mx, st)
        m_new = jnp.maximum(m, jnp.max(mx, axis=0, keepdims=True))
        p = jnp.concatenate([jnp.exp2(st_ref[hh, c, 0:nk, :] - m_new).astype(BF16)
                             for c in range(len(k_chunks))], axis=0)
        return m_new, jnp.exp2(m - m_new) * acc + _dot(vt, p)

    for hh, hs in enumerate(heads):
        m, acc = group(hh, [kc_ref[:, hs]], vc_ref[hs, :], jnp.full((1, TQ), -jnp.inf, F32),
                       jnp.zeros((2 * DV, TQ), F32))
        m_ref[hh] = m
        acc_ref[hh] = acc

    @pl.when(qi < T // TQ)
    def _():
        st = [(m_ref[hh], acc_ref[hh]) for hh in range(AH)]
        for j in range(T // (TK * AG)):
            chunks = [slice((j * AG + c) * TK, (j * AG + c + 1) * TK) for c in range(AG)]
            span = slice(j * AG * TK, (j + 1) * AG * TK)
            st = [group(hh, [kl_ref[ks, hs] for ks in chunks], vl_ref[hs, span], *st[hh])
                  for hh, hs in enumerate(heads)]
        for hh in range(AH):
            acc_ref[hh] = st[hh][1]

    for hh in range(AH):
        acc = acc_ref[hh]
        o_ref[:, hh * DV:(hh + 1) * DV] = jnp.transpose(acc[0:DV, :] / acc[DV:2 * DV, :]).astype(o_ref.dtype)


def _attention(q, k, vt, ctx_queries):
    nq = T // TQ
    qrow = lambda b, h, i: jnp.where(i < nq, b * nq + i, NL // TQ + b)
    return pl.pallas_call(
        _attn_kernel,
        grid=(NB_, H_A // AH, nq + 1 if ctx_queries else nq),
        in_specs=[pl.BlockSpec((TQ, AH * 256), lambda b, h, i: (qrow(b, h, i), h)),
                  pl.BlockSpec((T, AH * 256), lambda b, h, i: (b, h)),
                  pl.BlockSpec((TC, AH * 256), lambda b, h, i: (NL // TC + b, h)),
                  pl.BlockSpec((AH * 256, T), lambda b, h, i: (h, b)),
                  pl.BlockSpec((AH * 256, TC), lambda b, h, i: (h, NL // TC + b))],
        out_specs=pl.BlockSpec((TQ, AH * DV), lambda b, h, i: (qrow(b, h, i), h)),
        out_shape=jax.ShapeDtypeStruct((R if ctx_queries else NL, H_A * DV), BF16),
        scratch_shapes=[pltpu.VMEM((AH, 1, TQ), F32), pltpu.VMEM((AH, 2 * DV, TQ), F32),
                        pltpu.VMEM((AH, AG, TK, TQ), F32)],
        compiler_params=_cp(("arbitrary", "arbitrary", "arbitrary")),
        name="attention",
    )(q, k, k, vt, vt)


def _row_in(grow, rows):
    hit = grow == rows[0]
    for r in rows[1:]:
        hit = jnp.logical_or(hit, grow == r)
    return hit


def _mprep_kernel(x_ref, xp_ref, xn_ref, cw_ref, wq_ref, q_ref, k_ref, v_ref):
    i = pl.program_id(0)
    xb = x_ref[...]
    x = xb.astype(F32)
    prev_row = xp_ref[15:16, :].astype(F32)
    next_row = xn_ref[0:1, :].astype(F32)
    row = lax.broadcasted_iota(jnp.int32, (TM, 1), 0)
    grow = row + i * TM
    xprev = jnp.where(row == 0, prev_row, pltpu.roll(x, 1, axis=0))
    xprev = jnp.where(_row_in(grow, SEQ_STARTS), 0.0, xprev)
    xnext = jnp.where(row == TM - 1, next_row, pltpu.roll(x, TM - 1, axis=0))
    xnext = jnp.where(_row_in(grow, SEQ_ENDS), 0.0, xnext)
    cw = cw_ref[0]
    xc = xprev * cw[0:1, :] + x * cw[1:2, :] + xnext * cw[2:3, :]
    xcb = (xc * _sigmoid(xc)).astype(BF16)
    ones = jnp.ones((DH_M, TM), BF16)
    for h in range(H_M):
        sl = slice(h * DH_M, (h + 1) * DH_M)
        q_ref[:, sl] = _dot(xcb[:, sl], wq_ref[0, 0, h]).astype(BF16)
        k_ref[:, sl] = (_dot(xcb[:, sl], wq_ref[0, 1, h]) * DH_M ** -0.5).astype(BF16)
        v_ref[2 * h * DH_M:(2 * h + 1) * DH_M, :] = _dot_nt(wq_ref[0, 2, h], xb[:, sl]).astype(BF16)
        v_ref[(2 * h + 1) * DH_M:(2 * h + 2) * DH_M, :] = ones


def _mprep(p, conv_w, w_qkv, l):
    hb = TM // 16
    xcol = PM_XM // D_M
    return pl.pallas_call(
        _mprep_kernel,
        grid=(NBLK,),
        in_specs=[pl.BlockSpec((TM, D_M), lambda i: (i, xcol)),
                  pl.BlockSpec((16, D_M), lambda i: (jnp.maximum(i * hb - 1, 0), xcol)),
                  pl.BlockSpec((16, D_M), lambda i: (jnp.minimum((i + 1) * hb, R // 16 - 1), xcol)),
                  pl.BlockSpec((1, 3, D_M), lambda i: (l, 0, 0)),
                  pl.BlockSpec((1, 3, H_M, DH_M, DH_M), lambda i: (l, 0, 0, 0, 0))],
        out_specs=[pl.BlockSpec((TM, D_M), lambda i: (i, 0)),
                   pl.BlockSpec((TM, D_M), lambda i: (i, 0)),
                   pl.BlockSpec((2 * D_M, TM), lambda i: (0, i))],
        out_shape=[jax.ShapeDtypeStruct((R, D_M), BF16),
                   jax.ShapeDtypeStruct((R, D_M), BF16),
                   jax.ShapeDtypeStruct((2 * D_M, R), BF16)],
        compiler_params=_cp(("arbitrary",)),
        name="mlstm_prep",
    )(p, p, p, conv_w, w_qkv)


def _split3(x):
    h = x.astype(BF16)
    r = x - h.astype(F32)
    m = r.astype(BF16)
    l = (r - m.astype(F32)).astype(BF16)
    return h, m, l


def _mlstm_kernel(q_ref, k_ref, v_ref, g_ref, gt_ref, b_ref, bt_ref, h_ref, c_ref, m_ref):
    d = pl.program_id(1)
    s = pl.program_id(2)

    @pl.when(s == 0)
    def _():
        c_ref[...] = jnp.zeros_like(c_ref)
        m_ref[...] = jnp.zeros_like(m_ref)

    sign = 1 - 2 * d
    r = lax.broadcasted_iota(jnp.int32, (LC, LC), 0)
    c = lax.broadcasted_iota(jnp.int32, (LC, LC), 1)
    before = (c - r) * sign >= 0
    tri = jnp.where((r - c) * sign >= 0, 1.0, 0.0).astype(BF16)
    g = g_ref[0] + b_ref[0]
    gt = gt_ref[0] + bt_ref[0]
    ig_c = g[:, 0:H_M] * LOG2E
    lf_c = _log_sigmoid(g[:, H_M:2 * H_M]) * LOG2E
    lf_r = _log_sigmoid(gt[H_M:2 * H_M, :]) * LOG2E
    b_c = sum(_dot(tri, part) for part in _split3(lf_c))
    b_r = sum(_dot_nt(part, tri) for part in _split3(lf_r))
    tot = jnp.sum(lf_c, axis=0, keepdims=True)
    src = ig_c - b_c

    for h in range(H_M):
        sl = slice(h * DH_M, (h + 1) * DH_M)
        qh = q_ref[:, sl]
        kh = k_ref[:, sl]
        vth = v_ref[2 * h * DH_M:(2 * h + 2) * DH_M, :]
        bt = b_r[h:h + 1, :]
        tt = tot[:, h:h + 1]
        mp = m_ref[h:h + 1, 0:1]
        dm = jnp.where(before, bt + src[:, h:h + 1], -jnp.inf)
        m_inter = bt + mp
        m_t = jnp.maximum(jnp.max(dm, axis=0, keepdims=True), m_inter)
        sc = _dot_nt(kh, qh) * jnp.exp2(dm - m_t)
        inter = jnp.exp2(m_inter - m_t)
        st = c_ref[h]
        both = _dot(vth, sc.astype(BF16)) + inter * _dot_nt(st.astype(BF16), qh)
        den = both[DH_M:DH_M + 1, :]
        ht = both[0:DH_M, :] / jnp.maximum(jnp.abs(den), jnp.exp2(-m_t))
        h_ref[0, :, sl] = jnp.transpose(ht)
        gg = tt + src[:, h:h + 1]
        m_new = jnp.maximum(tt + mp, jnp.max(gg, axis=0, keepdims=True))
        kw = (kh.astype(F32) * jnp.exp2(gg - m_new)).astype(BF16)
        c_ref[h] = jnp.exp2(tt + mp - m_new) * st + _dot(vth, kw)
        m_ref[h:h + 1, :] = jnp.broadcast_to(m_new, (1, DH_M))


def _mlstm(q, k, v, gd, gtd, bd, btd):
    nl = T // LC
    nsteps = nl + TC // LC

    nc = TC // LC

    def rb(b, d, s):
        ctx = NL // LC + b * nc + jnp.where(d == 0, s, nc - 1 - s)
        lat = b * nl + jnp.where(d == 0, s - nc, nl - 1 - (s - nc))
        return jnp.where(s < nc, ctx, lat)

    qspec = pl.BlockSpec((LC, D_M), lambda b, d, s: (rb(b, d, s), 0))
    return pl.pallas_call(
        _mlstm_kernel,
        grid=(NB_, 2, nsteps),
        in_specs=[qspec, qspec,
                  pl.BlockSpec((2 * D_M, LC), lambda b, d, s: (0, rb(b, d, s))),
                  pl.BlockSpec((1, LC, 16), lambda b, d, s: (d, rb(b, d, s), 0)),
                  pl.BlockSpec((1, 16, LC), lambda b, d, s: (d, 0, rb(b, d, s))),
                  pl.BlockSpec((1, 1, 16), lambda b, d, s: (d, 0, 0)),
                  pl.BlockSpec((1, 16, 1), lambda b, d, s: (d, 0, 0))],
        out_specs=pl.BlockSpec((1, LC, D_M), lambda b, d, s: (d, rb(b, d, s), 0)),
        out_shape=jax.ShapeDtypeStruct((2, R, D_M), F32),
        scratch_shapes=[pltpu.VMEM((H_M, 2 * DH_M, DH_M), F32),
                        pltpu.VMEM((H_M, DH_M), F32)],
        compiler_params=_cp(("arbitrary", "arbitrary", "arbitrary")),
        name="mlstm_scan",
    )(q, k, v, gd, gtd, bd, btd)


S5_NLAT = T // SL
S5_NCTX = TC // SL
S5_SEG = S5_NLAT // NSEG


def _fold_copy(tok_ref, fold_ref, to_fold):
    def move(tok_idx, fold_rows, t):
        lanes = slice(t * 128, (t + 1) * 128)
        if to_fold:
            fold_ref[fold_rows, lanes] = tok_ref[tok_idx, :]
        else:
            tok_ref[tok_idx, :] = fold_ref[fold_rows, lanes]

    def body(j, carry):
        for b in range(NB_):
            rows = pl.ds(pl.multiple_of(b * S5_NLAT + j * NSEG, NSEG), NSEG)
            for t in range(SL):
                move(pl.ds(b * T + j * SL + t, NSEG, stride=S5_SEG * SL), rows, t)
        return carry

    lax.fori_loop(0, S5_SEG, body, 0)
    for t in range(SL):
        move(pl.ds(NL + t, NCX // SL, stride=SL), slice(NL // SL, NCH), t)


def _expand_block_diag(compact_ref, dense_ref, key_shift, key_mul, key_mask, row_gshift, col_gshift):
    comp = compact_ref[0]
    nrows, kc = comp.shape
    chunk = 512
    r = lax.broadcasted_iota(jnp.int32, (kc, chunk), 0)
    rg = (lax.broadcasted_iota(jnp.int32, (nrows, 1), 0) >> row_gshift) & (SLB - 1)
    for c0 in range(0, dense_ref.shape[1], chunk):
        q = lax.broadcasted_iota(jnp.int32, (kc, chunk), 1) + c0
        sel = jnp.where(r == (q >> key_shift) * key_mul + (q & key_mask), 1.0, 0.0).astype(BF16)
        cg = ((lax.broadcasted_iota(jnp.int32, (1, chunk), 1) + c0) >> col_gshift) & (SLB - 1)
        dense_ref[:, c0:c0 + chunk] = jnp.where(rg == cg, _dot(comp, sel), 0.0).astype(BF16)


def _cmul(ar, ai, sr, si):
    return ar * sr - ai * si, ar * si + ai * sr


def _s5_state_kernel(us_ref, wst_ref, a_ref, s_ref, e_ref, x_ref, w_ref):
    _fold_copy(us_ref.at[0], x_ref, True)
    _expand_block_diag(wst_ref, w_ref, 9, P_S, P_S - 1, 4, 6)
    e_ref[...] = _dot(x_ref[...].astype(BF16), w_ref[...])
    apow = a_ref[0]
    step_a = ((apow[0:1, :], apow[1:2, :]), (apow[2:3, :], apow[3:4, :]))
    nctx, seg = S5_NCTX, S5_SEG

    def advance(rows, d, st, write):
        cr = slice(2 * d * SW, (2 * d + 1) * SW)
        ci = slice((2 * d + 1) * SW, (2 * d + 2) * SW)
        dr = e_ref[rows, cr]
        di = e_ref[rows, ci]
        if write:
            e_ref[rows, cr] = st[0]
            e_ref[rows, ci] = st[1]
        nr, ni = _cmul(step_a[d][0], step_a[d][1], st[0], st[1])
        return nr + dr, ni + di

    def sweep(nsteps, rows_of, init, write):
        def body(kk, sts):
            out = []
            for b in range(NB_):
                out.append(advance(rows_of(b, kk), 0, sts[2 * b], write))
                out.append(advance(rows_of(b, nsteps - 1 - kk), 1, sts[2 * b + 1], write))
            return tuple(out)
        return lax.fori_loop(0, nsteps, body, init)

    z1 = jnp.zeros((1, SW), F32)
    carry = sweep(nctx, lambda b, kk: pl.ds(NL // SL + b * nctx + kk, 1), ((z1, z1),) * (2 * NB_), True)

    seg_rows = lambda b, kk: pl.ds(pl.multiple_of(b * S5_NLAT + kk * NSEG, NSEG), NSEG)
    z8 = jnp.zeros((NSEG, SW), F32)
    ends = sweep(seg, seg_rows, ((z8, z8),) * (2 * NB_), False)
    inits = []
    for ch in range(2 * NB_):
        d = ch % 2
        ar, ai = step_a[d]
        for _ in range(int(math.log2(seg))):
            ar, ai = ar * ar - ai * ai, 2.0 * ar * ai
        cur = carry[ch]
        rows_r, rows_i = [None] * NSEG, [None] * NSEG
        for kseg in (range(NSEG) if d == 0 else range(NSEG - 1, -1, -1)):
            rows_r[kseg], rows_i[kseg] = cur
            nr, ni = _cmul(ar, ai, cur[0], cur[1])
            cur = (nr + ends[ch][0][kseg:kseg + 1, :], ni + ends[ch][1][kseg:kseg + 1, :])
        inits.append((jnp.concatenate(rows_r, axis=0), jnp.concatenate(rows_i, axis=0)))
    sweep(seg, seg_rows, tuple(inits), True)
    s_ref[0] = e_ref[...].astype(BF16)


def _s5_state(us, wst, apow, l):
    return pl.pallas_call(
        _s5_state_kernel,
        grid=(NLB,),
        in_specs=[pl.BlockSpec((1, R, 128), lambda g: (g, 0, 0), pipeline_mode=pl.Buffered(1)),
                  pl.BlockSpec((1, SL * 128, 4 * P_S), lambda g: (l * NLB + g, 0, 0)),
                  pl.BlockSpec((1, 8, SW), lambda g: (l * NLB + g, 0, 0))],
        out_specs=pl.BlockSpec((1, NCH, 4 * SW), lambda g: (g, 0, 0)),
        out_shape=jax.ShapeDtypeStruct((NLB, NCH, 4 * SW), BF16),
        scratch_shapes=[pltpu.VMEM((NCH, 4 * SW), F32),
                        pltpu.VMEM((NCH, SL * 128), F32),
                        pltpu.VMEM((SL * 128, 4 * SW), BF16)],
        compiler_params=_cp(("arbitrary",)),
        name="s5_state",
    )(us, wst, apow)


def _s5_out_kernel(us_ref, toep_ref, s_ref, wo_ref, d_ref, y_ref, x_ref, tw_ref, ow_ref):
    _fold_copy(us_ref.at[0], x_ref, True)
    _expand_block_diag(toep_ref, tw_ref, 7, GROUP, GROUP - 1, 4, 4)
    _expand_block_diag(wo_ref, ow_ref, 7, GROUP, GROUP - 1, 6, 4)
    x = x_ref[...]
    y = _dot(x.astype(BF16), tw_ref[...]) + _dot(s_ref[0], ow_ref[...]) + d_ref[0] * x
    x_ref[...] = jax.nn.gelu(y)
    _fold_copy(y_ref.at[0], x_ref, False)


def _s5_out(us, toep, s, wout, dflat, l):
    return pl.pallas_call(
        _s5_out_kernel,
        grid=(NLB,),
        in_specs=[pl.BlockSpec((1, R, 128), lambda g: (g, 0, 0), pipeline_mode=pl.Buffered(1)),
                  pl.BlockSpec((1, SL * 128, SL * GROUP), lambda g: (l * NLB + g, 0, 0)),
                  pl.BlockSpec((1, NCH, 4 * SW), lambda g: (g, 0, 0), pipeline_mode=pl.Buffered(1)),
                  pl.BlockSpec((1, 4 * SW, SL * GROUP), lambda g: (l * NLB + g, 0, 0)),
                  pl.BlockSpec((1, 1, SL * 128), lambda g: (l * NLB + g, 0, 0))],
        out_specs=pl.BlockSpec((1, R, 128), lambda g: (g, 0, 0)),
        out_shape=jax.ShapeDtypeStruct((NLB, R, 128), F32),
        scratch_shapes=[pltpu.VMEM((NCH, SL * 128), F32),
                        pltpu.VMEM((SL * 128, SL * 128), BF16),
                        pltpu.VMEM((4 * SW, SL * 128), BF16)],
        compiler_params=_cp(("arbitrary",)),
        name="s5_out",
    )(us, toep, s, wout, dflat)


def _glu_kernel(y_ref, w_ref, o_ref):
    y = jnp.concatenate([y_ref[cb] for cb in range(NLB)], axis=1).astype(BF16)
    z = _dot(y, w_ref[0])
    o_ref[...] = (z[:, :D_S] * _sigmoid(z[:, D_S:])).astype(BF16)


def _glu(y, w, l):
    return pl.pallas_call(
        _glu_kernel,
        grid=(NBLK,),
        in_specs=[pl.BlockSpec((NLB, TM, 128), lambda i: (0, i, 0)),
                  pl.BlockSpec((1, D_S, 2 * D_S), lambda i: (l, 0, 0))],
        out_specs=pl.BlockSpec((TM, D_S), lambda i: (i, 0)),
        out_shape=jax.ShapeDtypeStruct((R, D_S), BF16),
        compiler_params=_cp(("arbitrary",)),
        name="s5_glu",
    )(y, w)


def _s5_params(a_re, a_im, log_dt, b_re, b_im, c_re, c_im, dskip):
    dt = jnp.exp(log_dt)[:, :, None]
    lam_r, lam_i = a_re * dt, a_im * dt
    mag = jnp.exp(lam_r)
    ar, ai = mag * jnp.cos(lam_i), mag * jnp.sin(lam_i)
    den = a_re * a_re + a_im * a_im
    nr, ni = ar - 1.0, ai
    cr = (nr * a_re + ni * a_im) / den
    ci = (ni * a_re - nr * a_im) / den
    cpr = c_re[None] * cr[:, :, None, :] - c_im[None] * ci[:, :, None, :]
    cpi = c_re[None] * ci[:, :, None, :] + c_im[None] * cr[:, :, None, :]
    j = jnp.arange(SL + 1, dtype=F32)[:, None, None, None]
    pm = jnp.exp(lam_r[None] * j)
    pr, pi = pm * jnp.cos(lam_i[None] * j), pm * jnp.sin(lam_i[None] * j)
    bt_re, bt_im = b_re.transpose(0, 2, 1), b_im.transpose(0, 2, 1)
    abr = pr[:, :, :, None, :] * bt_re[None, None] - pi[:, :, :, None, :] * bt_im[None, None]
    abi = pr[:, :, :, None, :] * bt_im[None, None] + pi[:, :, :, None, :] * bt_re[None, None]
    rev = lambda a, lo: jnp.stack([a[lo + SL - 1 - t] for t in range(SL)], axis=0)
    lagk = lambda d, ar_, ai_: (jnp.einsum('gcp,jgep->gejc', cpr[d], ar_) - jnp.einsum('gcp,jgep->gejc', cpi[d], ai_)
                                ).reshape(G_S, GROUP, SL * GROUP)
    catf = lagk(0, abr[:SL, 0], abi[:SL, 0])
    catb = lagk(1, rev(abr[:, 1], 0), rev(abi[:, 1], 0))
    rows = []
    for t_in in range(SL):
        lo, hi = t_in * GROUP, (SL - 1 - t_in) * GROUP
        rows.append(jnp.pad(catf[..., :SL * GROUP - lo], ((0, 0), (0, 0), (lo, 0)))
                    + jnp.pad(catb[..., hi:], ((0, 0), (0, 0), (0, hi))))
    by_block = lambda a, lead, minor: a.reshape(lead + (NLB, SLB) + minor)
    t5 = by_block(jnp.stack(rows, axis=0), (SL,), (GROUP, SL * GROUP))
    toep = t5.transpose(1, 0, 2, 3, 4).reshape(NLB, SL * 128, SL * GROUP)
    ws = jnp.stack([rev(abr[:, 0], 0), rev(abi[:, 0], 0), abr[:SL, 1], abi[:SL, 1]], axis=3)
    ws = by_block(ws, (SL,), (GROUP, 4 * P_S))
    wst = ws.transpose(1, 0, 2, 3, 4).reshape(NLB, SL * 128, 4 * P_S)
    def readout(pw_r, pw_i, d):
        cr_t, ci_t = cpr[d].transpose(0, 2, 1)[:, :, None, :], cpi[d].transpose(0, 2, 1)[:, :, None, :]
        wr, wi = pw_r.transpose(1, 2, 0)[..., None], pw_i.transpose(1, 2, 0)[..., None]
        return cr_t * wr - ci_t * wi, -(cr_t * wi + ci_t * wr)
    of_re, of_im = readout(pr[1:SL + 1, 0], pi[1:SL + 1, 0], 0)
    ob_re, ob_im = readout(rev(pr[:, 1], 1), rev(pi[:, 1], 1), 1)
    wo = by_block(jnp.stack([of_re, of_im, ob_re, ob_im], axis=0), (4,), (P_S, SL * GROUP))
    wout = wo.transpose(1, 0, 2, 3, 4).reshape(NLB, 4 * SW, SL * GROUP)
    blk = lambda a: a.reshape(NLB, 1, SW)
    apow = jnp.concatenate([blk(pr[SL, 0]), blk(pi[SL, 0]), blk(pr[SL, 1]), blk(pi[SL, 1]),
                            jnp.zeros((NLB, 4, SW), F32)], axis=1)
    dflat = jnp.tile(dskip.reshape(NLB, 1, 128), (1, 1, SL))
    return toep.astype(BF16), wst.astype(BF16), wout.astype(BF16), apow, dflat


def _mix_out_kernel(nsrc, tm, *refs):
    (a_ref, hf_ref, hb_ref, om_ref, gh_ref, s_ref, ga_ref, gm_ref, gs_ref, wa_ref, wm_ref, ws_ref, wo_ref,
     gt_ref, gf_ref, sh_ref, sc_ref, x1_ref, h2_ref) = refs[nsrc:]
    hsum = hf_ref[0] + hb_ref[0]
    parts = []
    for h in range(H_M):
        xh = hsum[:, h * DH_M:(h + 1) * DH_M]
        parts.append(xh * lax.rsqrt(jnp.mean(xh * xh, axis=-1, keepdims=True) + EPS))
    hn = jnp.concatenate(parts, axis=1) * gh_ref[0]
    m = (_sigmoid(om_ref[...].astype(F32)) * hn).astype(BF16)
    t = (_sigmoid(ga_ref[...].astype(F32)) * _dot(a_ref[...], wa_ref[0])
         + _sigmoid(gm_ref[...].astype(F32)) * _dot(m, wm_ref[0])
         + _sigmoid(gs_ref[...].astype(F32)) * _dot(s_ref[...], ws_ref[0]))
    x1 = _stream_rows(pl.program_id(0), tm, refs[:nsrc]) + gt_ref[0] * _dot(t.astype(BF16), wo_ref[0])
    x1_ref[...] = x1
    h2_ref[...] = (_rms(x1, gf_ref[0]) * (1.0 + sc_ref[0]) + sh_ref[0]).astype(BF16)


def _mix_out(a, hdir, p, gh, s, wa, wm, ws, wo, srcs, modrb, g_ffn, l, nrows):
    tm = 256
    per = TM // tm
    gcol = PM_GATES // D
    row = lambda w, c: pl.BlockSpec((tm, w), lambda i: (i, c))
    once = pl.Buffered(1)
    wspec = pl.BlockSpec((1, 1024, D), lambda i: (l, 0, 0), pipeline_mode=once)
    mod = lambda c: pl.BlockSpec((1, 1, D), lambda i: (i // per, 0, c))
    return pl.pallas_call(
        functools.partial(_mix_out_kernel, len(srcs), tm),
        grid=(nrows // tm,),
        in_specs=[*_stream_specs(srcs, tm),
                  row(1024, 0),
                  pl.BlockSpec((1, tm, D_M), lambda i: (0, i, 0)),
                  pl.BlockSpec((1, tm, D_M), lambda i: (1, i, 0)),
                  row(1024, PM_OM // 1024),
                  pl.BlockSpec((1, 1, D_M), lambda i: (l, 0, 0)),
                  row(1024, 0),
                  row(D, gcol), row(D, gcol + 1), row(D, gcol + 2),
                  wspec, wspec, wspec,
                  pl.BlockSpec((1, D, D), lambda i: (l, 0, 0), pipeline_mode=once),
                  mod(2),
                  pl.BlockSpec((1, 1, D), lambda i: (l, 0, 0)), mod(3), mod(4)],
        out_specs=[row(D, 0), row(D, 0)],
        out_shape=[jax.ShapeDtypeStruct((nrows, D), F32), jax.ShapeDtypeStruct((nrows, D), BF16)],
        compiler_params=_cp(("arbitrary",)),
        name="mix_out",
    )(*srcs, a, hdir, hdir, p, gh, s, p, p, p, wa, wm, ws, wo, modrb, g_ffn, modrb, modrb)


def _resid_kernel(t_ref, w_ref, x_ref, gt_ref, o_ref, wb_ref):
    @pl.when(pl.program_id(1) == 0)
    def _():
        wb_ref[...] = w_ref[0].astype(BF16)

    o_ref[...] = x_ref[...] + gt_ref[0] * _dot(t_ref[...], wb_ref[...])


def _resid(t, w, x, modrb, gate_chunk, tn, l, nrows, name):
    kdim = t.shape[1]
    nj = D // tn
    return pl.pallas_call(
        _resid_kernel,
        grid=(nj, nrows // TM),
        in_specs=[pl.BlockSpec((TM, kdim), lambda j, i: (i, 0)),
                  pl.BlockSpec((1, kdim, tn), lambda j, i: (l, 0, j)),
                  pl.BlockSpec((TM, tn), lambda j, i: (i, j)),
                  pl.BlockSpec((1, 1, tn), lambda j, i: (i, 0, gate_chunk * nj + j))],
        out_specs=pl.BlockSpec((TM, tn), lambda j, i: (i, j)),
        out_shape=jax.ShapeDtypeStruct((nrows, D), F32),
        scratch_shapes=[pltpu.VMEM((kdim, tn), BF16)],
        compiler_params=_cp(("arbitrary", "arbitrary")),
        name=name,
    )(t, w, x, modrb)


def _ffn_in_kernel(h_ref, wa_ref, wb_ref, o_ref, was_ref, wbs_ref):
    @pl.when(pl.program_id(1) == 0)
    def _():
        was_ref[...] = wa_ref[0].astype(BF16)
        wbs_ref[...] = wb_ref[0].astype(BF16)

    h = h_ref[...]
    a = _dot(h, was_ref[...])
    b = _dot(h, wbs_ref[...])
    o_ref[...] = (a * _sigmoid(a) * b).astype(BF16)


def _ffn_in(h, w, l, nrows):
    tn = 512
    nj = D_FF // tn
    tm = 2 * TM if nrows % (2 * TM) == 0 else TM
    return pl.pallas_call(
        _ffn_in_kernel,
        grid=(nj, nrows // tm),
        in_specs=[pl.BlockSpec((tm, D), lambda j, i: (i, 0)),
                  pl.BlockSpec((1, D, tn), lambda j, i: (l, 0, j)),
                  pl.BlockSpec((1, D, tn), lambda j, i: (l, 0, nj + j))],
        out_specs=pl.BlockSpec((tm, tn), lambda j, i: (i, j)),
        out_shape=jax.ShapeDtypeStruct((nrows, D_FF), BF16),
        scratch_shapes=[pltpu.VMEM((D, tn), BF16), pltpu.VMEM((D, tn), BF16)],
        compiler_params=_cp(("arbitrary", "arbitrary")),
        name="ffn_in",
    )(h, w, w)


def _final_norm_kernel(x_ref, g_ref, o_ref):
    o_ref[...] = _rms(x_ref[...], g_ref[...])


def _final_norm(x, g):
    return pl.pallas_call(
        _final_norm_kernel,
        grid=(NL // TM,),
        in_specs=[pl.BlockSpec((TM, D), lambda i: (i, 0)),
                  pl.BlockSpec((1, D), lambda i: (0, 0))],
        out_specs=pl.BlockSpec((TM, D), lambda i: (i, 0)),
        out_shape=jax.ShapeDtypeStruct((NL, D), F32),
        compiler_params=_cp(("arbitrary",)),
        name="final_norm",
    )(x, g.reshape(1, D))


def _rope_tables():
    f32 = np.float32
    rows = T // GRID_W
    rr, cc = np.meshgrid(np.arange(rows, dtype=f32), np.arange(GRID_W, dtype=f32), indexing='ij')
    rr, cc = rr.reshape(-1), cc.reshape(-1)
    half = ROPE // 2
    inv = (f32(1.0) / (f32(ROPE_THETA) ** (np.arange(0, half, 2, dtype=f32) / f32(half)))).astype(f32)
    ang = np.stack([rr[:, None] * inv, cc[:, None] * inv], axis=1).astype(f32)
    cos = np.cos(ang).astype(f32)
    sin = np.sin(ang).astype(f32)
    cos_f = np.stack([cos, cos], axis=2).reshape(T, ROPE)
    sin_f = np.stack([-sin, sin], axis=2).reshape(T, ROPE)
    pad = lambda a: np.concatenate([a, np.zeros((a.shape[0], 128 - ROPE), f32)], axis=1)
    cos_l, sin_l = pad(cos_f), pad(sin_f)
    cos_c = pad(np.ones((NCX, ROPE), f32))
    sin_c = np.zeros((NCX, 128), f32)
    return (jnp.asarray(np.concatenate([cos_l] * NB_ + [cos_c], axis=0)),
            jnp.asarray(np.concatenate([sin_l] * NB_ + [sin_c], axis=0)))


def _rope_partner(w):
    s = w.shape[:-1]
    w4 = w.reshape(s + (2, 2, ROPE // 4))
    return jnp.concatenate([w4[..., 1:2, :], w4[..., 0:1, :]], axis=-2).reshape(s + (ROPE,))


def _prep_weights(w_uq, w_ukv):
    depth = w_uq.shape[0]
    uq = w_uq.reshape(depth, Q_LORA, H_A, NOPE + ROPE)
    wqn = uq[..., :NOPE].reshape(depth, Q_LORA, H_A * NOPE)
    qr = uq[..., NOPE:]
    zq = jnp.zeros_like(qr)
    wqr = jnp.concatenate([qr, zq], axis=-1).reshape(depth, Q_LORA, H_A * 128)
    wqp = jnp.concatenate([_rope_partner(qr), zq], axis=-1).reshape(depth, Q_LORA, H_A * 128)
    ukv = w_ukv.reshape(depth, KV_LORA, H_A, NOPE + DV)
    wkn = ukv[..., :NOPE].reshape(depth, KV_LORA, H_A * NOPE)
    wvt = ukv[..., NOPE:].reshape(depth, KV_LORA, H_A * DV).transpose(0, 2, 1)
    return tuple(a.astype(BF16) for a in (wqn, wqr, wqp, wkn, wvt))


def kernel(x, c, ctx, c_ctx, w_ada, b_ada, g_mix, g_ffn, w_in, g_cq, w_uq, g_ckv, w_ukv, conv_m, w_qkv_m, b_gate_m, g_h_m, s5_a_re, s5_a_im, s5_log_dt, s5_b_re, s5_b_im, s5_c_re, s5_c_im, s5_d, w_glu, w_br_a, w_br_m, w_br_s, w_out, w_ffn_in, w_ffn_out, g_final):
    depth = w_ada.shape[0]
    xs = (x.reshape(NL, D), ctx.reshape(NCX, D))
    cvec = jnp.concatenate([c, c_ctx[None], jnp.zeros((8 - NB_ - 1, D), F32)], axis=0)
    mod = _ada(cvec, w_ada, b_ada)
    blocks_per_batch = T // TM
    cosk, sink = _rope_tables()
    mla_w = _prep_weights(w_uq, w_ukv)
    w_in_t = jnp.swapaxes(w_in, 1, 2)
    toep, wst, wout, apow, dflat = (a.reshape((depth * NLB,) + a.shape[2:]) for a in jax.vmap(_s5_params)(
        s5_a_re, s5_a_im, s5_log_dt, s5_b_re, s5_b_im, s5_c_re, s5_c_im, s5_d))
    w_out_b = w_out.astype(BF16)
    w_qkv_b = jnp.concatenate([w_qkv_m[:, :2], jnp.swapaxes(w_qkv_m[:, 2:], -1, -2)], axis=1).astype(BF16)
    w_glu_b = w_glu.astype(BF16)
    w_br_a_b, w_br_m_b, w_br_s_b = w_br_a.astype(BF16), w_br_m.astype(BF16), w_br_s.astype(BF16)
    gain = lambda g: g.reshape(depth, 1, g.shape[-1])
    g_mix, g_ffn, g_cq, g_ckv, g_h_m = gain(g_mix), gain(g_ffn), gain(g_cq), gain(g_ckv), gain(g_h_m)

    for l in range(depth):
        modrb = jnp.concatenate(
            [jnp.broadcast_to(mod[l, b:b + 1], (blocks_per_batch, 6 * D)) for b in range(NB_)]
            + [jnp.broadcast_to(mod[l, NB_:NB_ + 1], (NCX // TM, 6 * D))], axis=0).reshape(NBLK, 1, 6 * D)
        h1 = _prenorm(xs, g_mix, modrb, l)
        p = _inproj(h1, w_in_t, l)
        us, psm = _inproj_side(h1, w_in_t, l)

        last = l == depth - 1
        nrows = NL if last else R
        q, k, vt = _mla_proj(p, psm, cosk, sink, g_cq, g_ckv, *mla_w, l)
        a = _attention(q, k, vt, not last)

        qm, km, vm = _mprep(p, conv_m, w_qkv_b, l)
        gm = psm[:, SM_GM:SM_GM + 4 * H_M]
        gd = jnp.stack([gm[:, :2 * H_M], gm[:, 2 * H_M:]], axis=0)
        bd = b_gate_m[l].reshape(2, 1, 2 * H_M)
        hdir = _mlstm(qm, km, vm, gd, gd.transpose(0, 2, 1), bd, bd.transpose(0, 2, 1))

        st = _s5_state(us, wst, apow, l)
        y = _s5_out(us, toep, st, wout, dflat, l)
        s = _glu(y, w_glu_b, l)

        x1, h2 = _mix_out(a, hdir, p, g_h_m, s, w_br_a_b, w_br_m_b, w_br_s_b, w_out_b, xs, modrb, g_ffn, l, nrows)
        u = _ffn_in(h2, w_ffn_in, l, nrows)
        xs = (_resid(u, w_ffn_out, x1, modrb, 5, 512, l, nrows, "resid_ffn"),)

    return _final_norm(xs[0], g_final).reshape(NB_, T, D)
```

```python
import functools
import math

import numpy as np
import jax
import jax.numpy as jnp
from jax import lax
from jax.experimental import pallas as pl
from jax.experimental.pallas import tpu as pltpu

F32 = jnp.float32
BF16 = jnp.bfloat16

D = 2048
NB_ = 2
T = 4096
TC = 256
GRID_W = 64
EPS = 1e-6
H_A, Q_LORA, KV_LORA, NOPE, ROPE, DV = 8, 512, 512, 128, 64, 128
ROPE_THETA = 10000.0
ATTN_SCALE = (NOPE + ROPE) ** -0.5
H_M, DH_M = 8, 128
D_M = H_M * DH_M
D_S, GROUP, P_S = 1024, 16, 64
G_S = D_S // GROUP
D_FF = ((8 * D // 3 + 255) // 256) * 256
OFF = {}
_o = 0
for _n, _w in (('cq', Q_LORA), ('ckv', KV_LORA), ('krope', ROPE), ('xm', D_M), ('om', D_M),
               ('gm', 4 * H_M), ('us', D_S), ('gates', 3 * D)):
    OFF[_n] = (_o, _w)
    _o += _w

NL = NB_ * T
NCX = NB_ * TC
R = NL + NCX
TM = 512
NBLK = R // TM
BIG_ROW_BLOCKS = 8
assert R % (16 * BIG_ROW_BLOCKS) == 0 and NL % (16 * BIG_ROW_BLOCKS) == 0
LC = 256
TQ = 256
TK = 512
AH = 4
AG = 2
SL = 8
NCH = R // SL
NSEG = 8
SLB = 128 // GROUP
NLB = D_S // 128
SW = SLB * P_S
LOG2E = math.log2(math.e)
PM_CQ, PM_CKV, PM_XM, PM_GATES, PM_OM, PM_W = 0, 512, 1024, 2048, 2048 + 3 * D, 3072 + 3 * D
SM_KR, SM_KP, SM_GM, SM_W = 0, 128, 256, 384
VMEM_LIMIT = 56 * 1024 * 1024
SEQ_STARTS = tuple(b * T for b in range(NB_)) + tuple(NL + b * TC for b in range(NB_))
SEQ_ENDS = tuple(b * T + T - 1 for b in range(NB_)) + tuple(NL + b * TC + TC - 1 for b in range(NB_))


def _cp(sem):
    return pltpu.CompilerParams(dimension_semantics=sem, vmem_limit_bytes=VMEM_LIMIT)


def _dot(a, b):
    return jnp.dot(a, b, preferred_element_type=F32)


def _dot_nt(a, b):
    return lax.dot_general(a, b, (((1,), (1,)), ((), ())), preferred_element_type=F32)


def _sigmoid(x):
    return 1.0 / (1.0 + jnp.exp(-x))


def _log_sigmoid(x):
    return jnp.minimum(x, 0.0) - jnp.log(1.0 + jnp.exp(-jnp.abs(x)))


def _rms(x, g):
    xf = x.astype(F32)
    return xf * lax.rsqrt(jnp.mean(xf * xf, axis=-1, keepdims=True) + EPS) * g


def _ada_kernel(c_ref, w_ref, b_ref, o_ref):
    c = c_ref[...]
    s = (c * _sigmoid(c)).astype(BF16)
    o_ref[0] = _dot(s, w_ref[0].astype(BF16)) + b_ref[0]


def _ada(cvec, w_ada, b_ada):
    depth = w_ada.shape[0]
    tn = 1024
    return pl.pallas_call(
        _ada_kernel,
        grid=(depth, 6 * D // tn),
        in_specs=[pl.BlockSpec((8, D), lambda l, j: (0, 0)),
                  pl.BlockSpec((1, D, tn), lambda l, j: (l, 0, j)),
                  pl.BlockSpec((1, 1, tn), lambda l, j: (l, 0, j))],
        out_specs=pl.BlockSpec((1, 8, tn), lambda l, j: (l, 0, j)),
        out_shape=jax.ShapeDtypeStruct((depth, 8, 6 * D), F32),
        compiler_params=_cp(("arbitrary", "arbitrary")),
        name="ada",
    )(cvec, w_ada, b_ada.reshape(depth, 1, 6 * D))


def _stream_specs(srcs, tm):
    if len(srcs) == 1:
        return [pl.BlockSpec((tm, D), lambda i: (i, 0))]
    nlat = NL // tm
    return [pl.BlockSpec((tm, D), lambda i: (jnp.minimum(i, nlat - 1), 0)),
            pl.BlockSpec((tm, D), lambda i: (jnp.maximum(i - nlat, 0), 0))]


def _stream_rows(i, tm, refs):
    if len(refs) == 1:
        return refs[0][...]
    return jnp.where(i < NL // tm, refs[0][...], refs[1][...])


def _prenorm_kernel(nsrc, *refs):
    g_ref, sh_ref, sc_ref, o_ref = refs[nsrc:]
    is_lat = pl.program_id(0) < NL // TM
    conds = [is_lat, jnp.logical_not(is_lat)] if nsrc > 1 else [None]
    for x_ref, cond in zip(refs[:nsrc], conds):
        def write(x_ref=x_ref):
            o_ref[...] = (_rms(x_ref[...], g_ref[0]) * (1.0 + sc_ref[0]) + sh_ref[0]).astype(BF16)
        write() if cond is None else pl.when(cond)(write)


def _prenorm(srcs, g, modrb, l):
    return pl.pallas_call(
        functools.partial(_prenorm_kernel, len(srcs)),
        grid=(NBLK,),
        in_specs=[*_stream_specs(srcs, TM),
                  pl.BlockSpec((1, 1, D), lambda i: (l, 0, 0)),
                  pl.BlockSpec((1, 1, D), lambda i: (i, 0, 0)),
                  pl.BlockSpec((1, 1, D), lambda i: (i, 0, 1))],
        out_specs=pl.BlockSpec((TM, D), lambda i: (i, 0)),
        out_shape=jax.ShapeDtypeStruct((R, D), BF16),
        compiler_params=_cp(("arbitrary",)),
        name="prenorm",
    )(*srcs, g, modrb, modrb)


IP_TN = 1024
IP_STARTS = ((OFF['cq'][0], OFF['xm'][0]) + tuple(OFF['gates'][0] + k * IP_TN for k in range(3 * D // IP_TN))
             + (OFF['om'][0],))
IP_SHIFTS = tuple(s % IP_TN for s in IP_STARTS)
assert all(s < 128 and s % 16 == 0 for s in IP_SHIFTS) and OFF['ckv'][0] == Q_LORA and PM_W == IP_TN * len(IP_STARTS)


def _ip_window(j):
    idx = IP_STARTS[-1] // IP_TN
    for jj in range(len(IP_STARTS) - 2, -1, -1):
        idx = jnp.where(j == jj, IP_STARTS[jj] // IP_TN, idx)
    return idx


def _inproj_kernel(h_ref, wm_ref, we_ref, o_ref, wb_ref):
    j = pl.program_id(0)

    @pl.when(pl.program_id(1) == 0)
    def _():
        for shift in sorted(set(IP_SHIFTS)):
            hit = functools.reduce(jnp.logical_or, [j == jj for jj, s in enumerate(IP_SHIFTS) if s == shift])

            @pl.when(hit)
            def _(shift=shift):
                wb_ref[0:IP_TN - shift, :] = wm_ref[0, shift:IP_TN, :].astype(BF16)
                if shift:
                    wb_ref[IP_TN - shift:IP_TN, :] = we_ref[0, 0:shift, :].astype(BF16)

    o_ref[...] = _dot_nt(h_ref[...], wb_ref[...]).astype(o_ref.dtype)


def _inproj(h, w_in_t, l):
    tm = R // BIG_ROW_BLOCKS
    return pl.pallas_call(
        _inproj_kernel,
        grid=(len(IP_STARTS), BIG_ROW_BLOCKS),
        in_specs=[pl.BlockSpec((tm, D), lambda j, i: (i, 0)),
                  pl.BlockSpec((1, IP_TN, D), lambda j, i: (l, _ip_window(j), 0)),
                  pl.BlockSpec((1, 128, D), lambda j, i: (l, (_ip_window(j) + 1) * (IP_TN // 128), 0))],
        out_specs=pl.BlockSpec((tm, IP_TN), lambda j, i: (i, j)),
        out_shape=jax.ShapeDtypeStruct((R, PM_W), BF16),
        scratch_shapes=[pltpu.VMEM((IP_TN, D), BF16)],
        compiler_params=_cp(("arbitrary", "arbitrary")),
        name="inproj",
    )(h, w_in_t, w_in_t)


US0, KR0, GM0 = OFF['us'][0], OFF['krope'][0], OFF['gm'][0]
assert KR0 % ROPE == 0 and GM0 % (4 * H_M) == 0 and US0 % IP_TN < 128 and US0 % 16 == 0 and OFF['us'][1] == IP_TN


def _inproj_side_kernel(h_ref, wu_ref, wue_ref, wkr_ref, wgm_ref, us_ref, o_ref, wb_ref):
    @pl.when(pl.program_id(0) == 0)
    def _():
        wb_ref[...] = jnp.zeros_like(wb_ref)
        shift = US0 % IP_TN
        wb_ref[0:D_S - shift, :] = wu_ref[0, shift:IP_TN, :].astype(BF16)
        wb_ref[D_S - shift:D_S, :] = wue_ref[0, 0:shift, :].astype(BF16)
        kr = wkr_ref[0].astype(BF16)
        wb_ref[D_S + SM_KR:D_S + SM_KR + ROPE, :] = kr
        half = ROPE // 4
        for blk in range(ROPE // half):
            src = (blk ^ 1) * half
            wb_ref[D_S + SM_KP + blk * half:D_S + SM_KP + (blk + 1) * half, :] = kr[src:src + half, :]
        wb_ref[D_S + SM_GM:D_S + SM_GM + 4 * H_M, :] = wgm_ref[0].astype(BF16)

    res = _dot_nt(h_ref[...], wb_ref[...])
    for cb in range(NLB):
        us_ref[cb] = res[:, cb * 128:(cb + 1) * 128]
    o_ref[...] = res[:, D_S:]


def _inproj_side(h, w_in_t, l):
    once = pl.Buffered(1)
    wblk = lambda rows, row: pl.BlockSpec((1, rows, D), lambda i: (l, row // rows, 0), pipeline_mode=once)
    return pl.pallas_call(
        _inproj_side_kernel,
        grid=(NBLK,),
        in_specs=[pl.BlockSpec((TM, D), lambda i: (i, 0)),
                  wblk(IP_TN, US0 - US0 % IP_TN), wblk(128, US0 - US0 % IP_TN + IP_TN),
                  wblk(ROPE, KR0), wblk(4 * H_M, GM0)],
        out_specs=[pl.BlockSpec((NLB, TM, 128), lambda i: (0, i, 0)),
                   pl.BlockSpec((TM, SM_W), lambda i: (i, 0))],
        out_shape=[jax.ShapeDtypeStruct((NLB, R, 128), F32),
                   jax.ShapeDtypeStruct((R, SM_W), F32)],
        scratch_shapes=[pltpu.VMEM((D_S + SM_W, D), BF16)],
        compiler_params=_cp(("arbitrary",)),
        name="inproj_side",
    )(h, w_in_t, w_in_t, w_in_t, w_in_t)


def _mla_proj_kernel(cq_ref, ckv_ref, kr_ref, kp_ref, cos_ref, sin_ref, gq_ref, gkv_ref,
                     wqn_ref, wqr_ref, wqp_ref, wkn_ref, wvt_ref, q_ref, k_ref, vt_ref):
    hq = _rms(cq_ref[...], gq_ref[0]).astype(BF16)
    hk = _rms(ckv_ref[...], gkv_ref[0]).astype(BF16)
    cos = cos_ref[...]
    sin = sin_ref[...]
    qscale = ATTN_SCALE * LOG2E
    qn = _dot(hq, wqn_ref[0]) * qscale
    qr = _dot(hq, wqr_ref[0])
    qp = _dot(hq, wqp_ref[0])
    kn = _dot(hk, wkn_ref[0])
    vt = _dot_nt(wvt_ref[0], hk)
    kr = (kr_ref[...] * cos + kp_ref[...] * sin).astype(BF16)
    ones = jnp.ones((DV, TM), BF16)
    for h in range(H_A):
        lo = slice(h * 256, h * 256 + 128)
        hi = slice(h * 256 + 128, (h + 1) * 256)
        sl = slice(h * 128, (h + 1) * 128)
        q_ref[:, lo] = qn[:, sl].astype(BF16)
        q_ref[:, hi] = ((qr[:, sl] * cos + qp[:, sl] * sin) * qscale).astype(BF16)
        k_ref[:, lo] = kn[:, sl].astype(BF16)
        k_ref[:, hi] = kr
        vt_ref[lo, :] = vt[sl, :].astype(BF16)
        vt_ref[hi, :] = ones


def _mla_proj(p, psm, cosk, sink, gq, gkv, wqn, wqr, wqp, wkn, wvt, l):
    lw = lambda shp: pl.BlockSpec((1,) + shp, lambda i: (l, 0, 0))
    return pl.pallas_call(
        _mla_proj_kernel,
        grid=(NBLK,),
        in_specs=[pl.BlockSpec((TM, 512), lambda i: (i, PM_CQ // 512)),
                  pl.BlockSpec((TM, 512), lambda i: (i, PM_CKV // 512)),
                  pl.BlockSpec((TM, 128), lambda i: (i, SM_KR // 128)),
                  pl.BlockSpec((TM, 128), lambda i: (i, SM_KP // 128)),
                  pl.BlockSpec((TM, 128), lambda i: (i, 0)),
                  pl.BlockSpec((TM, 128), lambda i: (i, 0)),
                  pl.BlockSpec((1, 1, 512), lambda i: (l, 0, 0)),
                  pl.BlockSpec((1, 1, 512), lambda i: (l, 0, 0)),
                  lw((512, 1024)), lw((512, 1024)), lw((512, 1024)), lw((512, 1024)), lw((1024, 512))],
        out_specs=[pl.BlockSpec((TM, 2048), lambda i: (i, 0)),
                   pl.BlockSpec((TM, 2048), lambda i: (i, 0)),
                   pl.BlockSpec((2048, TM), lambda i: (0, i))],
        out_shape=[jax.ShapeDtypeStruct((R, 2048), BF16),
                   jax.ShapeDtypeStruct((R, 2048), BF16),
                   jax.ShapeDtypeStruct((2048, R), BF16)],
        compiler_params=_cp(("arbitrary",)),
        name="mla_proj",
    )(p, p, psm, psm, cosk, sink, gq, gkv, wqn, wqr, wqp, wkn, wvt)


def _attn_kernel(q_ref, kl_ref, kc_ref, vl_ref, vc_ref, o_ref, m_ref, acc_ref, st_ref):
    qi = pl.program_id(2)
    heads = [slice(hh * 256, (hh + 1) * 256) for hh in range(AH)]
    qs = [q_ref[:, hs] for hs in heads]

    def group(hh, k_chunks, vt, m, acc):
        mx = None
        nk = k_chunks[0].shape[0]
        for c, k in enumerate(k_chunks):
            st = _dot_nt(k, qs[hh])
            st_ref[hh, c, 0:nk, :] = st
            mx = st if mx is None else jnp.maximum(mx, st)
        m_new = jnp.maximum(m, jnp.max(mx, axis=0, keepdims=True))
        p = jnp.concatenate([jnp.exp2(st_ref[hh, c, 0:nk, :] - m_new).astype(BF16)
                             for c in range(len(k_chunks))], axis=0)
        return m_new, jnp.exp2(m - m_new) * acc + _dot(vt, p)

    for hh, hs in enumerate(heads):
        m, acc = group(hh, [kc_ref[:, hs]], vc_ref[hs, :], jnp.full((1, TQ), -jnp.inf, F32),
                       jnp.zeros((2 * DV, TQ), F32))
        m_ref[hh] = m
        acc_ref[hh] = acc

    @pl.when(qi < T // TQ)
    def _():
        st = [(m_ref[hh], acc_ref[hh]) for hh in range(AH)]
        for j in range(T // (TK * AG)):
            chunks = [slice((j * AG + c) * TK, (j * AG + c + 1) * TK) for c in range(AG)]
            span = slice(j * AG * TK, (j + 1) * AG * TK)
            st = [group(hh, [kl_ref[ks, hs] for ks in chunks], vl_ref[hs, span], *st[hh])
                  for hh, hs in enumerate(heads)]
        for hh in range(AH):
            acc_ref[hh] = st[hh][1]

    for hh in range(AH):
        acc = acc_ref[hh]
        o_ref[:, hh * DV:(hh + 1) * DV] = jnp.transpose(acc[0:DV, :] / acc[DV:2 * DV, :]).astype(o_ref.dtype)


def _attention(q, k, vt, ctx_queries):
    nq = T // TQ
    qrow = lambda b, h, i: jnp.where(i < nq, b * nq + i, NL // TQ + b)
    return pl.pallas_call(
        _attn_kernel,
        grid=(NB_, H_A // AH, nq + 1 if ctx_queries else nq),
        in_specs=[pl.BlockSpec((TQ, AH * 256), lambda b, h, i: (qrow(b, h, i), h)),
                  pl.BlockSpec((T, AH * 256), lambda b, h, i: (b, h)),
                  pl.BlockSpec((TC, AH * 256), lambda b, h, i: (NL // TC + b, h)),
                  pl.BlockSpec((AH * 256, T), lambda b, h, i: (h, b)),
                  pl.BlockSpec((AH * 256, TC), lambda b, h, i: (h, NL // TC + b))],
        out_specs=pl.BlockSpec((TQ, AH * DV), lambda b, h, i: (qrow(b, h, i), h)),
        out_shape=jax.ShapeDtypeStruct((R if ctx_queries else NL, H_A * DV), BF16),
        scratch_shapes=[pltpu.VMEM((AH, 1, TQ), F32), pltpu.VMEM((AH, 2 * DV, TQ), F32),
                        pltpu.VMEM((AH, AG, TK, TQ), F32)],
        compiler_params=_cp(("arbitrary", "arbitrary", "arbitrary")),
        name="attention",
    )(q, k, k, vt, vt)


def _row_in(grow, rows):
    hit = grow == rows[0]
    for r in rows[1:]:
        hit = jnp.logical_or(hit, grow == r)
    return hit


def _mprep_kernel(x_ref, xp_ref, xn_ref, cw_ref, wq_ref, q_ref, k_ref, v_ref):
    i = pl.program_id(0)
    xb = x_ref[...]
    x = xb.astype(F32)
    prev_row = xp_ref[15:16, :].astype(F32)
    next_row = xn_ref[0:1, :].astype(F32)
    row = lax.broadcasted_iota(jnp.int32, (TM, 1), 0)
    grow = row + i * TM
    xprev = jnp.where(row == 0, prev_row, pltpu.roll(x, 1, axis=0))
    xprev = jnp.where(_row_in(grow, SEQ_STARTS), 0.0, xprev)
    xnext = jnp.where(row == TM - 1, next_row, pltpu.roll(x, TM - 1, axis=0))
    xnext = jnp.where(_row_in(grow, SEQ_ENDS), 0.0, xnext)
    cw = cw_ref[0]
    xc = xprev * cw[0:1, :] + x * cw[1:2, :] + xnext * cw[2:3, :]
    xcb = (xc * _sigmoid(xc)).astype(BF16)
    ones = jnp.ones((DH_M, TM), BF16)
    for h in range(H_M):
        sl = slice(h * DH_M, (h + 1) * DH_M)
        q_ref[:, sl] = _dot(xcb[:, sl], wq_ref[0, 0, h]).astype(BF16)
        k_ref[:, sl] = (_dot(xcb[:, sl], wq_ref[0, 1, h]) * DH_M ** -0.5).astype(BF16)
        v_ref[2 * h * DH_M:(2 * h + 1) * DH_M, :] = _dot_nt(wq_ref[0, 2, h], xb[:, sl]).astype(BF16)
        v_ref[(2 * h + 1) * DH_M:(2 * h + 2) * DH_M, :] = ones


def _mprep(p, conv_w, w_qkv, l):
    hb = TM // 16
    xcol = PM_XM // D_M
    return pl.pallas_call(
        _mprep_kernel,
        grid=(NBLK,),
        in_specs=[pl.BlockSpec((TM, D_M), lambda i: (i, xcol)),
                  pl.BlockSpec((16, D_M), lambda i: (jnp.maximum(i * hb - 1, 0), xcol)),
                  pl.BlockSpec((16, D_M), lambda i: (jnp.minimum((i + 1) * hb, R // 16 - 1), xcol)),
                  pl.BlockSpec((1, 3, D_M), lambda i: (l, 0, 0)),
                  pl.BlockSpec((1, 3, H_M, DH_M, DH_M), lambda i: (l, 0, 0, 0, 0))],
        out_specs=[pl.BlockSpec((TM, D_M), lambda i: (i, 0)),
                   pl.BlockSpec((TM, D_M), lambda i: (i, 0)),
                   pl.BlockSpec((2 * D_M, TM), lambda i: (0, i))],
        out_shape=[jax.ShapeDtypeStruct((R, D_M), BF16),
                   jax.ShapeDtypeStruct((R, D_M), BF16),
                   jax.ShapeDtypeStruct((2 * D_M, R), BF16)],
        compiler_params=_cp(("arbitrary",)),
        name="mlstm_prep",
    )(p, p, p, conv_w, w_qkv)


def _split3(x):
    h = x.astype(BF16)
    r = x - h.astype(F32)
    m = r.astype(BF16)
    l = (r - m.astype(F32)).astype(BF16)
    return h, m, l


def _mlstm_kernel(q_ref, k_ref, v_ref, g_ref, gt_ref, b_ref, bt_ref, h_ref, c_ref, m_ref):
    d = pl.program_id(1)
    s = pl.program_id(2)

    @pl.when(s == 0)
    def _():
        c_ref[...] = jnp.zeros_like(c_ref)
        m_ref[...] = jnp.zeros_like(m_ref)

    sign = 1 - 2 * d
    r = lax.broadcasted_iota(jnp.int32, (LC, LC), 0)
    c = lax.broadcasted_iota(jnp.int32, (LC, LC), 1)
    before = (c - r) * sign >= 0
    tri = jnp.where((r - c) * sign >= 0, 1.0, 0.0).astype(BF16)
    g = g_ref[0] + b_ref[0]
    gt = gt_ref[0] + bt_ref[0]
    ig_c = g[:, 0:H_M] * LOG2E
    lf_c = _log_sigmoid(g[:, H_M:2 * H_M]) * LOG2E
    lf_r = _log_sigmoid(gt[H_M:2 * H_M, :]) * LOG2E
    b_c = sum(_dot(tri, part) for part in _split3(lf_c))
    b_r = sum(_dot_nt(part, tri) for part in _split3(lf_r))
    tot = jnp.sum(lf_c, axis=0, keepdims=True)
    src = ig_c - b_c

    for h in range(H_M):
        sl = slice(h * DH_M, (h + 1) * DH_M)
        qh = q_ref[:, sl]
        kh = k_ref[:, sl]
        vth = v_ref[2 * h * DH_M:(2 * h + 2) * DH_M, :]
        bt = b_r[h:h + 1, :]
        tt = tot[:, h:h + 1]
        mp = m_ref[h:h + 1, 0:1]
        dm = jnp.where(before, bt + src[:, h:h + 1], -jnp.inf)
        m_inter = bt + mp
        m_t = jnp.maximum(jnp.max(dm, axis=0, keepdims=True), m_inter)
        sc = _dot_nt(kh, qh) * jnp.exp2(dm - m_t)
        inter = jnp.exp2(m_inter - m_t)
        st = c_ref[h]
        both = _dot(vth, sc.astype(BF16)) + inter * _dot_nt(st.astype(BF16), qh)
        den = both[DH_M:DH_M + 1, :]
        ht = both[0:DH_M, :] / jnp.maximum(jnp.abs(den), jnp.exp2(-m_t))
        h_ref[0, :, sl] = jnp.transpose(ht)
        gg = tt + src[:, h:h + 1]
        m_new = jnp.maximum(tt + mp, jnp.max(gg, axis=0, keepdims=True))
        kw = (kh.astype(F32) * jnp.exp2(gg - m_new)).astype(BF16)
        c_ref[h] = jnp.exp2(tt + mp - m_new) * st + _dot(vth, kw)
        m_ref[h:h + 1, :] = jnp.broadcast_to(m_new, (1, DH_M))


def _mlstm(q, k, v, gd, gtd, bd, btd):
    nl = T // LC
    nsteps = nl + TC // LC

    nc = TC // LC

    def rb(b, d, s):
        ctx = NL // LC + b * nc + jnp.where(d == 0, s, nc - 1 - s)
        lat = b * nl + jnp.where(d == 0, s - nc, nl - 1 - (s - nc))
        return jnp.where(s < nc, ctx, lat)

    qspec = pl.BlockSpec((LC, D_M), lambda b, d, s: (rb(b, d, s), 0))
    return pl.pallas_call(
        _mlstm_kernel,
        grid=(NB_, 2, nsteps),
        in_specs=[qspec, qspec,
                  pl.BlockSpec((2 * D_M, LC), lambda b, d, s: (0, rb(b, d, s))),
                  pl.BlockSpec((1, LC, 16), lambda b, d, s: (d, rb(b, d, s), 0)),
                  pl.BlockSpec((1, 16, LC), lambda b, d, s: (d, 0, rb(b, d, s))),
                  pl.BlockSpec((1, 1, 16), lambda b, d, s: (d, 0, 0)),
                  pl.BlockSpec((1, 16, 1), lambda b, d, s: (d, 0, 0))],
        out_specs=pl.BlockSpec((1, LC, D_M), lambda b, d, s: (d, rb(b, d, s), 0)),
        out_shape=jax.ShapeDtypeStruct((2, R, D_M), F32),
        scratch_shapes=[pltpu.VMEM((H_M, 2 * DH_M, DH_M), F32),
                        pltpu.VMEM((H_M, DH_M), F32)],
        compiler_params=_cp(("arbitrary", "arbitrary", "arbitrary")),
        name="mlstm_scan",
    )(q, k, v, gd, gtd, bd, btd)


S5_NLAT = T // SL
S5_NCTX = TC // SL
S5_SEG = S5_NLAT // NSEG


def _fold_copy(tok_ref, fold_ref, to_fold):
    def move(tok_idx, fold_rows, t):
        lanes = slice(t * 128, (t + 1) * 128)
        if to_fold:
            fold_ref[fold_rows, lanes] = tok_ref[tok_idx, :]
        else:
            tok_ref[tok_idx, :] = fold_ref[fold_rows, lanes]

    def body(j, carry):
        for b in range(NB_):
            rows = pl.ds(pl.multiple_of(b * S5_NLAT + j * NSEG, NSEG), NSEG)
            for t in range(SL):
                move(pl.ds(b * T + j * SL + t, NSEG, stride=S5_SEG * SL), rows, t)
        return carry

    lax.fori_loop(0, S5_SEG, body, 0)
    for t in range(SL):
        move(pl.ds(NL + t, NCX // SL, stride=SL), slice(NL // SL, NCH), t)


def _expand_block_diag(compact_ref, dense_ref, key_shift, key_mul, key_mask, row_gshift, col_gshift):
    comp = compact_ref[0]
    nrows, kc = comp.shape
    chunk = 512
    r = lax.broadcasted_iota(jnp.int32, (kc, chunk), 0)
    rg = (lax.broadcasted_iota(jnp.int32, (nrows, 1), 0) >> row_gshift) & (SLB - 1)
    for c0 in range(0, dense_ref.shape[1], chunk):
        q = lax.broadcasted_iota(jnp.int32, (kc, chunk), 1) + c0
        sel = jnp.where(r == (q >> key_shift) * key_mul + (q & key_mask), 1.0, 0.0).astype(BF16)
        cg = ((lax.broadcasted_iota(jnp.int32, (1, chunk), 1) + c0) >> col_gshift) & (SLB - 1)
        dense_ref[:, c0:c0 + chunk] = jnp.where(rg == cg, _dot(comp, sel), 0.0).astype(BF16)


def _cmul(ar, ai, sr, si):
    return ar * sr - ai * si, ar * si + ai * sr


def _s5_state_kernel(us_ref, wst_ref, a_ref, s_ref, e_ref, x_ref, w_ref):
    _fold_copy(us_ref.at[0], x_ref, True)
    _expand_block_diag(wst_ref, w_ref, 9, P_S, P_S - 1, 4, 6)
    e_ref[...] = _dot(x_ref[...].astype(BF16), w_ref[...])
    apow = a_ref[0]
    step_a = ((apow[0:1, :], apow[1:2, :]), (apow[2:3, :], apow[3:4, :]))
    nctx, seg = S5_NCTX, S5_SEG

    def advance(rows, d, st, write):
        cr = slice(2 * d * SW, (2 * d + 1) * SW)
        ci = slice((2 * d + 1) * SW, (2 * d + 2) * SW)
        dr = e_ref[rows, cr]
        di = e_ref[rows, ci]
        if write:
            e_ref[rows, cr] = st[0]
            e_ref[rows, ci] = st[1]
        nr, ni = _cmul(step_a[d][0], step_a[d][1], st[0], st[1])
        return nr + dr, ni + di

    def sweep(nsteps, rows_of, init, write):
        def body(kk, sts):
            out = []
            for b in range(NB_):
                out.append(advance(rows_of(b, kk), 0, sts[2 * b], write))
                out.append(advance(rows_of(b, nsteps - 1 - kk), 1, sts[2 * b + 1], write))
            return tuple(out)
        return lax.fori_loop(0, nsteps, body, init)

    z1 = jnp.zeros((1, SW), F32)
    carry = sweep(nctx, lambda b, kk: pl.ds(NL // SL + b * nctx + kk, 1), ((z1, z1),) * (2 * NB_), True)

    seg_rows = lambda b, kk: pl.ds(pl.multiple_of(b * S5_NLAT + kk * NSEG, NSEG), NSEG)
    z8 = jnp.zeros((NSEG, SW), F32)
    ends = sweep(seg, seg_rows, ((z8, z8),) * (2 * NB_), False)
    inits = []
    for ch in range(2 * NB_):
        d = ch % 2
        ar, ai = step_a[d]
        for _ in range(int(math.log2(seg))):
            ar, ai = ar * ar - ai * ai, 2.0 * ar * ai
        cur = carry[ch]
        rows_r, rows_i = [None] * NSEG, [None] * NSEG
        for kseg in (range(NSEG) if d == 0 else range(NSEG - 1, -1, -1)):
            rows_r[kseg], rows_i[kseg] = cur
            nr, ni = _cmul(ar, ai, cur[0], cur[1])
            cur = (nr + ends[ch][0][kseg:kseg + 1, :], ni + ends[ch][1][kseg:kseg + 1, :])
        inits.append((jnp.concatenate(rows_r, axis=0), jnp.concatenate(rows_i, axis=0)))
    sweep(seg, seg_rows, tuple(inits), True)
    s_ref[0] = e_ref[...].astype(BF16)


def _s5_state(us, wst, apow, l):
    return pl.pallas_call(
        _s5_state_kernel,
        grid=(NLB,),
        in_specs=[pl.BlockSpec((1, R, 128), lambda g: (g, 0, 0), pipeline_mode=pl.Buffered(1)),
                  pl.BlockSpec((1, SL * 128, 4 * P_S), lambda g: (l * NLB + g, 0, 0)),
                  pl.BlockSpec((1, 8, SW), lambda g: (l * NLB + g, 0, 0))],
        out_specs=pl.BlockSpec((1, NCH, 4 * SW), lambda g: (g, 0, 0)),
        out_shape=jax.ShapeDtypeStruct((NLB, NCH, 4 * SW), BF16),
        scratch_shapes=[pltpu.VMEM((NCH, 4 * SW), F32),
                        pltpu.VMEM((NCH, SL * 128), F32),
                        pltpu.VMEM((SL * 128, 4 * SW), BF16)],
        compiler_params=_cp(("arbitrary",)),
        name="s5_state",
    )(us, wst, apow)


def _s5_out_kernel(us_ref, toep_ref, s_ref, wo_ref, d_ref, y_ref, x_ref, tw_ref, ow_ref):
    _fold_copy(us_ref.at[0], x_ref, True)
    _expand_block_diag(toep_ref, tw_ref, 7, GROUP, GROUP - 1, 4, 4)
    _expand_block_diag(wo_ref, ow_ref, 7, GROUP, GROUP - 1, 6, 4)
    x = x_ref[...]
    y = _dot(x.astype(BF16), tw_ref[...]) + _dot(s_ref[0], ow_ref[...]) + d_ref[0] * x
    x_ref[...] = jax.nn.gelu(y)
    _fold_copy(y_ref.at[0], x_ref, False)


def _s5_out(us, toep, s, wout, dflat, l):
    return pl.pallas_call(
        _s5_out_kernel,
        grid=(NLB,),
        in_specs=[pl.BlockSpec((1, R, 128), lambda g: (g, 0, 0), pipeline_mode=pl.Buffered(1)),
                  pl.BlockSpec((1, SL * 128, SL * GROUP), lambda g: (l * NLB + g, 0, 0)),
                  pl.BlockSpec((1, NCH, 4 * SW), lambda g: (g, 0, 0), pipeline_mode=pl.Buffered(1)),
                  pl.BlockSpec((1, 4 * SW, SL * GROUP), lambda g: (l * NLB + g, 0, 0)),
                  pl.BlockSpec((1, 1, SL * 128), lambda g: (l * NLB + g, 0, 0))],
        out_specs=pl.BlockSpec((1, R, 128), lambda g: (g, 0, 0)),
        out_shape=jax.ShapeDtypeStruct((NLB, R, 128), F32),
        scratch_shapes=[pltpu.VMEM((NCH, SL * 128), F32),
                        pltpu.VMEM((SL * 128, SL * 128), BF16),
                        pltpu.VMEM((4 * SW, SL * 128), BF16)],
        compiler_params=_cp(("arbitrary",)),
        name="s5_out",
    )(us, toep, s, wout, dflat)


def _glu_kernel(y_ref, w_ref, o_ref):
    y = jnp.concatenate([y_ref[cb] for cb in range(NLB)], axis=1).astype(BF16)
    z = _dot(y, w_ref[0])
    o_ref[...] = (z[:, :D_S] * _sigmoid(z[:, D_S:])).astype(BF16)


def _glu(y, w, l):
    return pl.pallas_call(
        _glu_kernel,
        grid=(NBLK,),
        in_specs=[pl.BlockSpec((NLB, TM, 128), lambda i: (0, i, 0)),
                  pl.BlockSpec((1, D_S, 2 * D_S), lambda i: (l, 0, 0))],
        out_specs=pl.BlockSpec((TM, D_S), lambda i: (i, 0)),
        out_shape=jax.ShapeDtypeStruct((R, D_S), BF16),
        compiler_params=_cp(("arbitrary",)),
        name="s5_glu",
    )(y, w)


def _s5_params(a_re, a_im, log_dt, b_re, b_im, c_re, c_im, dskip):
    dt = jnp.exp(log_dt)[:, :, None]
    lam_r, lam_i = a_re * dt, a_im * dt
    mag = jnp.exp(lam_r)
    ar, ai = mag * jnp.cos(lam_i), mag * jnp.sin(lam_i)
    den = a_re * a_re + a_im * a_im
    nr, ni = ar - 1.0, ai
    cr = (nr * a_re + ni * a_im) / den
    ci = (ni * a_re - nr * a_im) / den
    cpr = c_re[None] * cr[:, :, None, :] - c_im[None] * ci[:, :, None, :]
    cpi = c_re[None] * ci[:, :, None, :] + c_im[None] * cr[:, :, None, :]
    j = jnp.arange(SL + 1, dtype=F32)[:, None, None, None]
    pm = jnp.exp(lam_r[None] * j)
    pr, pi = pm * jnp.cos(lam_i[None] * j), pm * jnp.sin(lam_i[None] * j)
    bt_re, bt_im = b_re.transpose(0, 2, 1), b_im.transpose(0, 2, 1)
    abr = pr[:, :, :, None, :] * bt_re[None, None] - pi[:, :, :, None, :] * bt_im[None, None]
    abi = pr[:, :, :, None, :] * bt_im[None, None] + pi[:, :, :, None, :] * bt_re[None, None]
    rev = lambda a, lo: jnp.stack([a[lo + SL - 1 - t] for t in range(SL)], axis=0)
    lagk = lambda d, ar_, ai_: (jnp.einsum('gcp,jgep->gejc', cpr[d], ar_) - jnp.einsum('gcp,jgep->gejc', cpi[d], ai_)
                                ).reshape(G_S, GROUP, SL * GROUP)
    catf = lagk(0, abr[:SL, 0], abi[:SL, 0])
    catb = lagk(1, rev(abr[:, 1], 0), rev(abi[:, 1], 0))
    rows = []
    for t_in in range(SL):
        lo, hi = t_in * GROUP, (SL - 1 - t_in) * GROUP
        rows.append(jnp.pad(catf[..., :SL * GROUP - lo], ((0, 0), (0, 0), (lo, 0)))
                    + jnp.pad(catb[..., hi:], ((0, 0), (0, 0), (0, hi))))
    by_block = lambda a, lead, minor: a.reshape(lead + (NLB, SLB) + minor)
    t5 = by_block(jnp.stack(rows, axis=0), (SL,), (GROUP, SL * GROUP))
    toep = t5.transpose(1, 0, 2, 3, 4).reshape(NLB, SL * 128, SL * GROUP)
    ws = jnp.stack([rev(abr[:, 0], 0), rev(abi[:, 0], 0), abr[:SL, 1], abi[:SL, 1]], axis=3)
    ws = by_block(ws, (SL,), (GROUP, 4 * P_S))
    wst = ws.transpose(1, 0, 2, 3, 4).reshape(NLB, SL * 128, 4 * P_S)
    def readout(pw_r, pw_i, d):
        cr_t, ci_t = cpr[d].transpose(0, 2, 1)[:, :, None, :], cpi[d].transpose(0, 2, 1)[:, :, None, :]
        wr, wi = pw_r.transpose(1, 2, 0)[..., None], pw_i.transpose(1, 2, 0)[..., None]
        return cr_t * wr - ci_t * wi, -(cr_t * wi + ci_t * wr)
    of_re, of_im = readout(pr[1:SL + 1, 0], pi[1:SL + 1, 0], 0)
    ob_re, ob_im = readout(rev(pr[:, 1], 1), rev(pi[:, 1], 1), 1)
    wo = by_block(jnp.stack([of_re, of_im, ob_re, ob_im], axis=0), (4,), (P_S, SL * GROUP))
    wout = wo.transpose(1, 0, 2, 3, 4).reshape(NLB, 4 * SW, SL * GROUP)
    blk = lambda a: a.reshape(NLB, 1, SW)
    apow = jnp.concatenate([blk(pr[SL, 0]), blk(pi[SL, 0]), blk(pr[SL, 1]), blk(pi[SL, 1]),
                            jnp.zeros((NLB, 4, SW), F32)], axis=1)
    dflat = jnp.tile(dskip.reshape(NLB, 1, 128), (1, 1, SL))
    return toep.astype(BF16), wst.astype(BF16), wout.astype(BF16), apow, dflat


def _mix_out_kernel(nsrc, tm, *refs):
    (a_ref, hf_ref, hb_ref, om_ref, gh_ref, s_ref, ga_ref, gm_ref, gs_ref, wa_ref, wm_ref, ws_ref, wo_ref,
     gt_ref, gf_ref, sh_ref, sc_ref, x1_ref, h2_ref) = refs[nsrc:]
    hsum = hf_ref[0] + hb_ref[0]
    parts = []
    for h in range(H_M):
        xh = hsum[:, h * DH_M:(h + 1) * DH_M]
        parts.append(xh * lax.rsqrt(jnp.mean(xh * xh, axis=-1, keepdims=True) + EPS))
    hn = jnp.concatenate(parts, axis=1) * gh_ref[0]
    m = (_sigmoid(om_ref[...].astype(F32)) * hn).astype(BF16)
    t = (_sigmoid(ga_ref[...].astype(F32)) * _dot(a_ref[...], wa_ref[0])
         + _sigmoid(gm_ref[...].astype(F32)) * _dot(m, wm_ref[0])
         + _sigmoid(gs_ref[...].astype(F32)) * _dot(s_ref[...], ws_ref[0]))
    x1 = _stream_rows(pl.program_id(0), tm, refs[:nsrc]) + gt_ref[0] * _dot(t.astype(BF16), wo_ref[0])
    x1_ref[...] = x1
    h2_ref[...] = (_rms(x1, gf_ref[0]) * (1.0 + sc_ref[0]) + sh_ref[0]).astype(BF16)


def _mix_out(a, hdir, p, gh, s, wa, wm, ws, wo, srcs, modrb, g_ffn, l, nrows):
    tm = 256
    per = TM // tm
    gcol = PM_GATES // D
    row = lambda w, c: pl.BlockSpec((tm, w), lambda i: (i, c))
    once = pl.Buffered(1)
    wspec = pl.BlockSpec((1, 1024, D), lambda i: (l, 0, 0), pipeline_mode=once)
    mod = lambda c: pl.BlockSpec((1, 1, D), lambda i: (i // per, 0, c))
    return pl.pallas_call(
        functools.partial(_mix_out_kernel, len(srcs), tm),
        grid=(nrows // tm,),
        in_specs=[*_stream_specs(srcs, tm),
                  row(1024, 0),
                  pl.BlockSpec((1, tm, D_M), lambda i: (0, i, 0)),
                  pl.BlockSpec((1, tm, D_M), lambda i: (1, i, 0)),
                  row(1024, PM_OM // 1024),
                  pl.BlockSpec((1, 1, D_M), lambda i: (l, 0, 0)),
                  row(1024, 0),
                  row(D, gcol), row(D, gcol + 1), row(D, gcol + 2),
                  wspec, wspec, wspec,
                  pl.BlockSpec((1, D, D), lambda i: (l, 0, 0), pipeline_mode=once),
                  mod(2),
                  pl.BlockSpec((1, 1, D), lambda i: (l, 0, 0)), mod(3), mod(4)],
        out_specs=[row(D, 0), row(D, 0)],
        out_shape=[jax.ShapeDtypeStruct((nrows, D), F32), jax.ShapeDtypeStruct((nrows, D), BF16)],
        compiler_params=_cp(("arbitrary",)),
        name="mix_out",
    )(*srcs, a, hdir, hdir, p, gh, s, p, p, p, wa, wm, ws, wo, modrb, g_ffn, modrb, modrb)


def _resid_kernel(t_ref, w_ref, x_ref, gt_ref, o_ref, wb_ref):
    @pl.when(pl.program_id(1) == 0)
    def _():
        wb_ref[...] = w_ref[0].astype(BF16)

    o_ref[...] = x_ref[...] + gt_ref[0] * _dot(t_ref[...], wb_ref[...])


def _resid(t, w, x, modrb, gate_chunk, tn, l, nrows, name):
    kdim = t.shape[1]
    nj = D // tn
    return pl.pallas_call(
        _resid_kernel,
        grid=(nj, nrows // TM),
        in_specs=[pl.BlockSpec((TM, kdim), lambda j, i: (i, 0)),
                  pl.BlockSpec((1, kdim, tn), lambda j, i: (l, 0, j)),
                  pl.BlockSpec((TM, tn), lambda j, i: (i, j)),
                  pl.BlockSpec((1, 1, tn), lambda j, i: (i, 0, gate_chunk * nj + j))],
        out_specs=pl.BlockSpec((TM, tn), lambda j, i: (i, j)),
        out_shape=jax.ShapeDtypeStruct((nrows, D), F32),
        scratch_shapes=[pltpu.VMEM((kdim, tn), BF16)],
        compiler_params=_cp(("arbitrary", "arbitrary")),
        name=name,
    )(t, w, x, modrb)


def _ffn_in_kernel(h_ref, wa_ref, wb_ref, o_ref, was_ref, wbs_ref):
    @pl.when(pl.program_id(1) == 0)
    def _():
        was_ref[...] = wa_ref[0].astype(BF16)
        wbs_ref[...] = wb_ref[0].astype(BF16)

    h = h_ref[...]
    a = _dot(h, was_ref[...])
    b = _dot(h, wbs_ref[...])
    o_ref[...] = (a * _sigmoid(a) * b).astype(BF16)


def _ffn_in(h, w, l, nrows):
    tn = 512
    nj = D_FF // tn
    tm = nrows // BIG_ROW_BLOCKS
    return pl.pallas_call(
        _ffn_in_kernel,
        grid=(nj, nrows // tm),
        in_specs=[pl.BlockSpec((tm, D), lambda j, i: (i, 0)),
                  pl.BlockSpec((1, D, tn), lambda j, i: (l, 0, j)),
                  pl.BlockSpec((1, D, tn), lambda j, i: (l, 0, nj + j))],
        out_specs=pl.BlockSpec((tm, tn), lambda j, i: (i, j)),
        out_shape=jax.ShapeDtypeStruct((nrows, D_FF), BF16),
        scratch_shapes=[pltpu.VMEM((D, tn), BF16), pltpu.VMEM((D, tn), BF16)],
        compiler_params=_cp(("arbitrary", "arbitrary")),
        name="ffn_in",
    )(h, w, w)


def _final_norm_kernel(x_ref, g_ref, o_ref):
    o_ref[...] = _rms(x_ref[...], g_ref[...])


def _final_norm(x, g):
    return pl.pallas_call(
        _final_norm_kernel,
        grid=(NL // TM,),
        in_specs=[pl.BlockSpec((TM, D), lambda i: (i, 0)),
                  pl.BlockSpec((1, D), lambda i: (0, 0))],
        out_specs=pl.BlockSpec((TM, D), lambda i: (i, 0)),
        out_shape=jax.ShapeDtypeStruct((NL, D), F32),
        compiler_params=_cp(("arbitrary",)),
        name="final_norm",
    )(x, g.reshape(1, D))


def _rope_tables():
    f32 = np.float32
    rows = T // GRID_W
    rr, cc = np.meshgrid(np.arange(rows, dtype=f32), np.arange(GRID_W, dtype=f32), indexing='ij')
    rr, cc = rr.reshape(-1), cc.reshape(-1)
    half = ROPE // 2
    inv = (f32(1.0) / (f32(ROPE_THETA) ** (np.arange(0, half, 2, dtype=f32) / f32(half)))).astype(f32)
    ang = np.stack([rr[:, None] * inv, cc[:, None] * inv], axis=1).astype(f32)
    cos = np.cos(ang).astype(f32)
    sin = np.sin(ang).astype(f32)
    cos_f = np.stack([cos, cos], axis=2).reshape(T, ROPE)
    sin_f = np.stack([-sin, sin], axis=2).reshape(T, ROPE)
    pad = lambda a: np.concatenate([a, np.zeros((a.shape[0], 128 - ROPE), f32)], axis=1)
    cos_l, sin_l = pad(cos_f), pad(sin_f)
    cos_c = pad(np.ones((NCX, ROPE), f32))
    sin_c = np.zeros((NCX, 128), f32)
    return (jnp.asarray(np.concatenate([cos_l] * NB_ + [cos_c], axis=0)),
            jnp.asarray(np.concatenate([sin_l] * NB_ + [sin_c], axis=0)))


def _rope_partner(w):
    s = w.shape[:-1]
    w4 = w.reshape(s + (2, 2, ROPE // 4))
    return jnp.concatenate([w4[..., 1:2, :], w4[..., 0:1, :]], axis=-2).reshape(s + (ROPE,))


def _prep_weights(w_uq, w_ukv):
    depth = w_uq.shape[0]
    uq = w_uq.reshape(depth, Q_LORA, H_A, NOPE + ROPE)
    wqn = uq[..., :NOPE].reshape(depth, Q_LORA, H_A * NOPE)
    qr = uq[..., NOPE:]
    zq = jnp.zeros_like(qr)
    wqr = jnp.concatenate([qr, zq], axis=-1).reshape(depth, Q_LORA, H_A * 128)
    wqp = jnp.concatenate([_rope_partner(qr), zq], axis=-1).reshape(depth, Q_LORA, H_A * 128)
    ukv = w_ukv.reshape(depth, KV_LORA, H_A, NOPE + DV)
    wkn = ukv[..., :NOPE].reshape(depth, KV_LORA, H_A * NOPE)
    wvt = ukv[..., NOPE:].reshape(depth, KV_LORA, H_A * DV).transpose(0, 2, 1)
    return tuple(a.astype(BF16) for a in (wqn, wqr, wqp, wkn, wvt))


def kernel(x, c, ctx, c_ctx, w_ada, b_ada, g_mix, g_ffn, w_in, g_cq, w_uq, g_ckv, w_ukv, conv_m, w_qkv_m, b_gate_m, g_h_m, s5_a_re, s5_a_im, s5_log_dt, s5_b_re, s5_b_im, s5_c_re, s5_c_im, s5_d, w_glu, w_br_a, w_br_m, w_br_s, w_out, w_ffn_in, w_ffn_out, g_final):
    depth = w_ada.shape[0]
    xs = (x.reshape(NL, D), ctx.reshape(NCX, D))
    cvec = jnp.concatenate([c, c_ctx[None], jnp.zeros((8 - NB_ - 1, D), F32)], axis=0)
    mod = _ada(cvec, w_ada, b_ada)
    blocks_per_batch = T // TM
    cosk, sink = _rope_tables()
    mla_w = _prep_weights(w_uq, w_ukv)
    w_in_t = jnp.swapaxes(w_in, 1, 2)
    toep, wst, wout, apow, dflat = (a.reshape((depth * NLB,) + a.shape[2:]) for a in jax.vmap(_s5_params)(
        s5_a_re, s5_a_im, s5_log_dt, s5_b_re, s5_b_im, s5_c_re, s5_c_im, s5_d))
    w_out_b = w_out.astype(BF16)
    w_qkv_b = jnp.concatenate([w_qkv_m[:, :2], jnp.swapaxes(w_qkv_m[:, 2:], -1, -2)], axis=1).astype(BF16)
    w_glu_b = w_glu.astype(BF16)
    w_br_a_b, w_br_m_b, w_br_s_b = w_br_a.astype(BF16), w_br_m.astype(BF16), w_br_s.astype(BF16)
    gain = lambda g: g.reshape(depth, 1, g.shape[-1])
    g_mix, g_ffn, g_cq, g_ckv, g_h_m = gain(g_mix), gain(g_ffn), gain(g_cq), gain(g_ckv), gain(g_h_m)

    for l in range(depth):
        modrb = jnp.concatenate(
            [jnp.broadcast_to(mod[l, b:b + 1], (blocks_per_batch, 6 * D)) for b in range(NB_)]
            + [jnp.broadcast_to(mod[l, NB_:NB_ + 1], (NCX // TM, 6 * D))], axis=0).reshape(NBLK, 1, 6 * D)
        h1 = _prenorm(xs, g_mix, modrb, l)
        p = _inproj(h1, w_in_t, l)
        us, psm = _inproj_side(h1, w_in_t, l)

        last = l == depth - 1
        nrows = NL if last else R
        q, k, vt = _mla_proj(p, psm, cosk, sink, g_cq, g_ckv, *mla_w, l)
        a = _attention(q, k, vt, not last)

        qm, km, vm = _mprep(p, conv_m, w_qkv_b, l)
        gm = psm[:, SM_GM:SM_GM + 4 * H_M]
        gd = jnp.stack([gm[:, :2 * H_M], gm[:, 2 * H_M:]], axis=0)
        bd = b_gate_m[l].reshape(2, 1, 2 * H_M)
        hdir = _mlstm(qm, km, vm, gd, gd.transpose(0, 2, 1), bd, bd.transpose(0, 2, 1))

        st = _s5_state(us, wst, apow, l)
        y = _s5_out(us, toep, st, wout, dflat, l)
        s = _glu(y, w_glu_b, l)

        x1, h2 = _mix_out(a, hdir, p, g_h_m, s, w_br_a_b, w_br_m_b, w_br_s_b, w_out_b, xs, modrb, g_ffn, l, nrows)
        u = _ffn_in(h2, w_ffn_in, l, nrows)
        xs = (_resid(u, w_ffn_out, x1, modrb, 5, 512, l, nrows, "resid_ffn"),)

    return _final_norm(xs[0], g_final).reshape(NB_, T, D)
```
